```python
import math
import jax, jax.numpy as jnp
from jax import lax
import numpy as np

D_MODEL = 1024
BATCH = 4
SEQ = 4096
DEPTH = 4

N_MEM = 256
MIX_HALF = D_MODEL // 2
A_HEADS = 4
A_HEAD_DIM = MIX_HALF // A_HEADS
DILATED_GROUPS = ((128, 1), (512, 4), (2048, 16))
PAD_MULTIPLE = 2048
RG_WIDTH = MIX_HALF
RG_BLOCKS = 4
RG_BLOCK_DIM = RG_WIDTH // RG_BLOCKS
RG_C = 8.0
CONV_WIDTH = 4
POOL_WINDOWS = (2, 4, 8, 16)
POOL_GROUP_DIM = D_MODEL // len(POOL_WINDOWS)
XA_HEADS = 4
XA_HEAD_DIM = D_MODEL // XA_HEADS
D_FF = 4 * D_MODEL
IN_COLS = 3 * MIX_HALF + 2 * RG_WIDTH
N_EVEN = (DEPTH + 1) // 2
N_ODD = DEPTH // 2
EPS = 1e-6

kernel_name = 'hybrid_dilated_rglru_pool_trunk'


def rms_norm(x, g):
    xf = x.astype(jnp.float32)
    y = xf * lax.rsqrt(jnp.mean(xf * xf, axis=-1, keepdims=True) + EPS)
    return (y * g.astype(jnp.float32)).astype(x.dtype)


def dilated_window_attn(q, k, v, window, dilation):
    B, S, H, C = q.shape
    L = window // dilation
    nb = S // window

    def split(t):
        return t.reshape(B, nb, L, dilation, H, C)

    def with_prev(t):
        prev = jnp.pad(t, ((0, 0), (1, 0), (0, 0), (0, 0), (0, 0), (0, 0)))[:, :-1]
        return jnp.concatenate([prev, t], axis=2)

    qb = split(q)
    kc = with_prev(split(k))
    vc = with_prev(split(v))
    s = jnp.einsum('bnqrhc,bnkrhc->bnrhqk', qb, kc).astype(jnp.float32) * (C ** -0.5)
    qi = jnp.arange(L)[:, None]
    kj = jnp.arange(2 * L)[None, :]
    dist = qi + L - kj
    band = (dist >= 0) & (dist <= L)
    first = (jnp.arange(nb) == 0)[:, None, None]
    mask = band[None] & (jnp.logical_not(first) | (kj >= L)[None])
    s = jnp.where(mask[None, :, None, None], s, -jnp.inf)
    m = jnp.max(s, axis=-1, keepdims=True)
    p = jnp.exp(s - m)
    den = jnp.sum(p, axis=-1)
    o = jnp.einsum('bnrhqk,bnkrhc->bnqrhc', p, vc.astype(jnp.float32))
    den_t = den.transpose(0, 1, 4, 2, 3)
    o = o / den_t[..., None]
    lse = m[..., 0].transpose(0, 1, 4, 2, 3) + jnp.log(den_t)
    return o.reshape(B, S, H, C), lse.reshape(B, S, H)


def mixer_attn_rglru(h, w_in, q_g, k_g, conv_w, conv_b, ga_w, ga_b, gx_w, gx_b, lam, w_out):
    B, S, _ = h.shape
    proj = h @ w_in
    q, k, v, xr, gate = jnp.split(
        proj, [MIX_HALF, 2 * MIX_HALF, 3 * MIX_HALF, 3 * MIX_HALF + RG_WIDTH], axis=-1)

    q = rms_norm(q.reshape(B, S, A_HEADS, A_HEAD_DIM), q_g)
    k = rms_norm(k.reshape(B, S, A_HEADS, A_HEAD_DIM), k_g)
    v = v.reshape(B, S, A_HEADS, A_HEAD_DIM)
    Sp = -(-S // PAD_MULTIPLE) * PAD_MULTIPLE
    pad = ((0, 0), (0, Sp - S), (0, 0), (0, 0))
    qp, kp, vp = jnp.pad(q, pad), jnp.pad(k, pad), jnp.pad(v, pad)
    outs, lses = [], []
    for window, dilation in DILATED_GROUPS:
        o_g, lse_g = dilated_window_attn(qp, kp, vp, window, dilation)
        outs.append(o_g)
        lses.append(lse_g)
    wts = jax.nn.softmax(jnp.stack(lses), axis=0)
    o_a = jnp.sum(wts[..., None] * jnp.stack(outs), axis=0)[:, :S]
    o_a = o_a.reshape(B, S, MIX_HALF).astype(h.dtype)

    xpad = jnp.pad(xr, ((0, 0), (CONV_WIDTH - 1, 0), (0, 0)))
    xc = conv_b + sum(xpad[:, j:j + S] * conv_w[j] for j in range(CONV_WIDTH))
    xb = xc.reshape(B, S, RG_BLOCKS, RG_BLOCK_DIM)
    r = jax.nn.sigmoid(jnp.einsum('bsgc,gcd->bsgd', xb, ga_w).reshape(B, S, RG_WIDTH) + ga_b)
    i = jax.nn.sigmoid(jnp.einsum('bsgc,gcd->bsgd', xb, gx_w).reshape(B, S, RG_WIDTH) + gx_b)
    log_a = -RG_C * r.astype(jnp.float32) * jax.nn.softplus(-lam.astype(jnp.float32))
    a = jnp.exp(log_a)
    u = jnp.sqrt(-jnp.expm1(2.0 * log_a)) * (i * xc).astype(jnp.float32)

    def combine(e1, e2):
        a1, b1 = e1
        a2, b2 = e2
        return a1 * a2, a2 * b1 + b2

    _, hs = lax.associative_scan(combine, (a, u), axis=1)
    y_b = hs.astype(h.dtype) * jax.nn.gelu(gate)

    return jnp.concatenate([o_a, y_b], axis=-1) @ w_out


def pool_mixer(h, pool_w, scale):
    B, S, D = h.shape
    hf = h.astype(jnp.float32)
    cs = jnp.pad(jnp.cumsum(hf, axis=1), ((0, 0), (1, 0), (0, 0)))
    t = jnp.arange(S)
    diffs = []
    for g, w in enumerate(POOL_WINDOWS):
        sl = slice(g * POOL_GROUP_DIM, (g + 1) * POOL_GROUP_DIM)
        lo = jnp.maximum(t + 1 - w, 0)
        window_sum = cs[:, 1:, sl] - cs[:, lo, sl]
        count = jnp.minimum(t + 1, w).astype(jnp.float32)[None, :, None]
        diffs.append(window_sum / count - hf[..., sl])
    d = jnp.stack(diffs, axis=2)
    out = jnp.einsum('bsgc,gcd->bsgd', d, pool_w.astype(jnp.float32)).reshape(B, S, D)
    return (out * scale.astype(jnp.float32)).astype(h.dtype)


def memory_xattn(h, mem_n, w_q, w_kv, q_g, k_g, w_o):
    B, S, D = h.shape
    M = mem_n.shape[1]
    q = rms_norm((h @ w_q).reshape(B, S, XA_HEADS, XA_HEAD_DIM), q_g)
    k, v = jnp.split(mem_n @ w_kv, 2, axis=-1)
    k = rms_norm(k.reshape(B, M, XA_HEADS, XA_HEAD_DIM), k_g)
    v = v.reshape(B, M, XA_HEADS, XA_HEAD_DIM)
    s = jnp.einsum('bshc,bmhc->bhsm', q, k).astype(jnp.float32) * (XA_HEAD_DIM ** -0.5)
    p = jax.nn.softmax(s, axis=-1)
    o = jnp.einsum('bhsm,bmhc->bshc', p, v.astype(jnp.float32)).astype(h.dtype)
    return o.reshape(B, S, D) @ w_o


def sq_relu_mlp(h, w1, w2):
    return jnp.square(jax.nn.relu(h @ w1)) @ w2


def setup_inputs(seed: int = 0) -> dict:
    key = jax.random.key(seed)
    ks = jax.random.split(key, 26)
    f32 = jnp.float32

    def nrm(k, shape, fan_in):
        return jax.random.normal(k, shape, f32) * fan_in ** -0.5

    def gain(k, shape):
        return 1.0 + 0.02 * jax.random.normal(k, shape, f32)

    def bias(k, shape):
        return 0.02 * jax.random.normal(k, shape, f32)

    u = jax.random.uniform(ks[15], (N_EVEN, RG_WIDTH), f32, minval=0.9, maxval=0.999)
    s = u ** (1.0 / RG_C)
    lam = jnp.log(s) - jnp.log1p(-s)
    return {
        'x': jax.random.normal(ks[0], (BATCH, SEQ, D_MODEL), f32),
        'mem': jax.random.normal(ks[1], (BATCH, N_MEM, D_MODEL), f32),
        'mem_norm_g': gain(ks[2], (D_MODEL,)),
        'mix_norm_g': gain(ks[3], (DEPTH, D_MODEL)),
        'xattn_norm_g': gain(ks[4], (DEPTH, D_MODEL)),
        'mlp_norm_g': gain(ks[5], (DEPTH, D_MODEL)),
        'ev_w_in': nrm(ks[6], (N_EVEN, D_MODEL, IN_COLS), D_MODEL),
        'ev_q_norm_g': gain(ks[7], (N_EVEN, A_HEAD_DIM)),
        'ev_k_norm_g': gain(ks[8], (N_EVEN, A_HEAD_DIM)),
        'ev_conv_w': nrm(ks[9], (N_EVEN, CONV_WIDTH, RG_WIDTH), CONV_WIDTH),
        'ev_conv_b': bias(ks[10], (N_EVEN, RG_WIDTH)),
        'ev_gate_a_w': nrm(ks[11], (N_EVEN, RG_BLOCKS, RG_BLOCK_DIM, RG_BLOCK_DIM), RG_BLOCK_DIM),
        'ev_gate_a_b': bias(ks[12], (N_EVEN, RG_WIDTH)),
        'ev_gate_x_w': nrm(ks[13], (N_EVEN, RG_BLOCKS, RG_BLOCK_DIM, RG_BLOCK_DIM), RG_BLOCK_DIM),
        'ev_gate_x_b': bias(ks[14], (N_EVEN, RG_WIDTH)),
        'ev_lambda': lam,
        'ev_w_out': nrm(ks[16], (N_EVEN, 2 * MIX_HALF, D_MODEL), 2 * MIX_HALF),
        'od_pool_w': nrm(ks[17], (N_ODD, len(POOL_WINDOWS), POOL_GROUP_DIM, POOL_GROUP_DIM), POOL_GROUP_DIM),
        'od_scale': 0.5 + 0.05 * jax.random.normal(ks[18], (N_ODD, D_MODEL), f32),
        'xa_w_q': nrm(ks[19], (DEPTH, D_MODEL, D_MODEL), D_MODEL),
        'xa_w_kv': nrm(ks[20], (DEPTH, D_MODEL, 2 * D_MODEL), D_MODEL),
        'xa_q_norm_g': gain(ks[21], (DEPTH, XA_HEAD_DIM)),
        'xa_k_norm_g': gain(ks[22], (DEPTH, XA_HEAD_DIM)),
        'xa_w_o': nrm(ks[23], (DEPTH, D_MODEL, D_MODEL), D_MODEL),
        'mlp_w1': nrm(ks[24], (DEPTH, D_MODEL, D_FF), D_MODEL),
        'mlp_w2': nrm(ks[25], (DEPTH, D_FF, D_MODEL), D_FF),
    }


def reference(x, mem, mem_norm_g, mix_norm_g, xattn_norm_g, mlp_norm_g,
              ev_w_in, ev_q_norm_g, ev_k_norm_g, ev_conv_w, ev_conv_b,
              ev_gate_a_w, ev_gate_a_b, ev_gate_x_w, ev_gate_x_b, ev_lambda, ev_w_out,
              od_pool_w, od_scale,
              xa_w_q, xa_w_kv, xa_q_norm_g, xa_k_norm_g, xa_w_o,
              mlp_w1, mlp_w2):
    mem_n = rms_norm(mem, mem_norm_g)
    for l in range(DEPTH):
        h = rms_norm(x, mix_norm_g[l])
        if l % 2 == 0:
            e = l // 2
            x = x + mixer_attn_rglru(h, ev_w_in[e], ev_q_norm_g[e], ev_k_norm_g[e],
                                     ev_conv_w[e], ev_conv_b[e], ev_gate_a_w[e], ev_gate_a_b[e],
                                     ev_gate_x_w[e], ev_gate_x_b[e], ev_lambda[e], ev_w_out[e])
        else:
            o = l // 2
            x = x + pool_mixer(h, od_pool_w[o], od_scale[o])
        x = x + memory_xattn(rms_norm(x, xattn_norm_g[l]), mem_n, xa_w_q[l], xa_w_kv[l],
                             xa_q_norm_g[l], xa_k_norm_g[l], xa_w_o[l])
        x = x + sq_relu_mlp(rms_norm(x, mlp_norm_g[l]), mlp_w1[l], mlp_w2[l])
    return x
```

```python
import functools

import jax
import jax.numpy as jnp
from jax import lax
from jax.experimental import pallas as pl
from jax.experimental.pallas import tpu as pltpu

D_MODEL = 1024
DEPTH = 4
N_MEM = 256
MIX_HALF = D_MODEL // 2
A_HEADS = 4
A_HEAD_DIM = MIX_HALF // A_HEADS
A_BAND = 128
A_DILATIONS = (16, 4, 1)
RG_WIDTH = MIX_HALF
RG_BLOCKS = 4
RG_BLOCK_DIM = RG_WIDTH // RG_BLOCKS
RG_C = 8.0
CONV_WIDTH = 4
POOL_WINDOWS = (2, 4, 8, 16)
POOL_GROUP_DIM = D_MODEL // len(POOL_WINDOWS)
POOL_HALO = 16
XA_HEADS = 4
XA_HEAD_DIM = D_MODEL // XA_HEADS
D_FF = 4 * D_MODEL
IN_COLS = 3 * MIX_HALF + 2 * RG_WIDTH
EPS = 1e-6

SUBLANES = 8
VMEM_LIMIT = 56 * 1024 * 1024

ROW_TILE = 512
MLP_CHUNK = 1024
RG_TILE = 256
POOL_TILE = 512

BF16 = jnp.bfloat16
F32 = jnp.float32


def _rms(x, g):
    ms = jnp.mean(x * x, axis=-1, keepdims=True)
    return x * lax.rsqrt(ms + EPS) * g


def _dot(a, b):
    return jnp.dot(a, b, preferred_element_type=F32)


def _dot_nt(a, b):
    return lax.dot_general(a, b, (((1,), (1,)), ((), ())), preferred_element_type=F32)


def _resident(shape, index_map):
    return pl.BlockSpec(shape, index_map, pipeline_mode=pl.Buffered(1))


def _params(*sem):
    return pltpu.CompilerParams(dimension_semantics=sem, vmem_limit_bytes=VMEM_LIMIT)


def _memkv_kernel(mem_ref, mg_ref, wkv_ref, kg_ref, kt_ref, v_ref):
    mem_n = _rms(mem_ref[...], mg_ref[...]).astype(BF16)
    kv = _dot(mem_n, wkv_ref[...])
    scale = XA_HEAD_DIM ** -0.5
    for h in range(XA_HEADS):
        sl = slice(h * XA_HEAD_DIM, (h + 1) * XA_HEAD_DIM)
        kh = _rms(kv[:, sl], kg_ref[...]) * scale
        kt_ref[sl, :] = kh.T.astype(BF16)
    v_ref[...] = kv[:, D_MODEL:].astype(BF16)


def _memkv(mem, mem_norm_g, w_kv, k_g):
    batch = mem.shape[0]
    return pl.pallas_call(
        _memkv_kernel,
        grid=(DEPTH, batch),
        in_specs=[
            pl.BlockSpec((None, N_MEM, D_MODEL), lambda l, b: (b, 0, 0)),
            pl.BlockSpec((1, D_MODEL), lambda l, b: (0, 0)),
            pl.BlockSpec((None, D_MODEL, 2 * D_MODEL), lambda l, b: (l, 0, 0)),
            pl.BlockSpec((None, 1, XA_HEAD_DIM), lambda l, b: (l, 0, 0)),
        ],
        out_specs=[
            pl.BlockSpec((None, None, D_MODEL, N_MEM), lambda l, b: (l, b, 0, 0)),
            pl.BlockSpec((None, None, N_MEM, D_MODEL), lambda l, b: (l, b, 0, 0)),
        ],
        out_shape=[
            jax.ShapeDtypeStruct((DEPTH, batch, D_MODEL, N_MEM), BF16),
            jax.ShapeDtypeStruct((DEPTH, batch, N_MEM, D_MODEL), BF16),
        ],
        compiler_params=_params("arbitrary", "arbitrary"),
        name="memkv",
    )(mem, mem_norm_g.reshape(1, D_MODEL), w_kv, k_g.reshape(DEPTH, 1, XA_HEAD_DIM))


def _xattn_kernel(x_ref, g_ref, wq_ref, qg_ref, kt_ref, v_ref, wo_ref, o_ref):
    x = x_ref[...]
    h = _rms(x, g_ref[...]).astype(BF16)
    q = _dot(h, wq_ref[...])
    outs = []
    for hd in range(XA_HEADS):
        sl = slice(hd * XA_HEAD_DIM, (hd + 1) * XA_HEAD_DIM)
        qh = _rms(q[:, sl], qg_ref[...]).astype(BF16)
        s = _dot(qh, kt_ref[sl, :])
        m = jnp.max(s, axis=-1, keepdims=True)
        p = jnp.exp(s - m)
        den = jnp.sum(p, axis=-1, keepdims=True)
        oh = _dot(p.astype(BF16), v_ref[:, sl])
        outs.append((oh / den).astype(BF16))
    o = jnp.concatenate(outs, axis=-1)
    o_ref[...] = x + _dot(o, wo_ref[...])


def _xattn(x, layer, g, w_q, q_g, kt, v, w_o):
    batch, seq, _ = x.shape
    return pl.pallas_call(
        _xattn_kernel,
        grid=(batch, seq // ROW_TILE),
        in_specs=[
            pl.BlockSpec((None, ROW_TILE, D_MODEL), lambda b, t: (b, t, 0)),
            pl.BlockSpec((None, 1, D_MODEL), lambda b, t: (layer, 0, 0)),
            _resident((None, D_MODEL, D_MODEL), lambda b, t: (layer, 0, 0)),
            pl.BlockSpec((None, 1, XA_HEAD_DIM), lambda b, t: (layer, 0, 0)),
            pl.BlockSpec((None, None, D_MODEL, N_MEM), lambda b, t: (layer, b, 0, 0)),
            pl.BlockSpec((None, None, N_MEM, D_MODEL), lambda b, t: (layer, b, 0, 0)),
            _resident((None, D_MODEL, D_MODEL), lambda b, t: (layer, 0, 0)),
        ],
        out_specs=pl.BlockSpec((None, ROW_TILE, D_MODEL), lambda b, t: (b, t, 0)),
        out_shape=jax.ShapeDtypeStruct(x.shape, F32),
        compiler_params=_params("parallel", "parallel"),
        name="xattn",
    )(x, g, w_q, q_g, kt, v, w_o)


def _mlp_kernel(x_ref, g_ref, w1_ref, w2_ref, o_ref):
    x = x_ref[...]
    h = _rms(x, g_ref[...]).astype(BF16)
    acc = x
    for c in range(D_FF // MLP_CHUNK):
        sl = slice(c * MLP_CHUNK, (c + 1) * MLP_CHUNK)
        a = jnp.maximum(_dot(h, w1_ref[:, sl]), 0.0)
        acc = acc + _dot((a * a).astype(BF16), w2_ref[sl, :])
    o_ref[...] = acc


def _mlp(x2, layer, g, w1, w2):
    rows = x2.shape[0]
    return pl.pallas_call(
        _mlp_kernel,
        grid=(rows // ROW_TILE,),
        in_specs=[
            pl.BlockSpec((ROW_TILE, D_MODEL), lambda i: (i, 0)),
            pl.BlockSpec((None, 1, D_MODEL), lambda i: (layer, 0, 0)),
            _resident((None, D_MODEL, D_FF), lambda i: (layer, 0, 0)),
            _resident((None, D_FF, D_MODEL), lambda i: (layer, 0, 0)),
        ],
        out_specs=pl.BlockSpec((ROW_TILE, D_MODEL), lambda i: (i, 0)),
        out_shape=jax.ShapeDtypeStruct(x2.shape, F32),
        compiler_params=_params("parallel"),
        name="mlp",
    )(x2, g, w1, w2)


def _inproj_kernel(x_ref, g_ref, w_ref, qg_ref, kg_ref, q_ref, k_ref, v_ref, xr_ref, gate_ref):
    h = _rms(x_ref[...], g_ref[...]).astype(BF16)
    scale = A_HEAD_DIM ** -0.5
    for j, (out_ref, gain_ref, mult) in enumerate(
            ((q_ref, qg_ref, scale), (k_ref, kg_ref, 1.0), (v_ref, None, None),
             (xr_ref, None, None), (gate_ref, None, None))):
        p = _dot(h, w_ref[:, j * MIX_HALF:(j + 1) * MIX_HALF])
        if gain_ref is None:
            out_ref[...] = p
            continue
        for hd in range(A_HEADS):
            sl = slice(hd * A_HEAD_DIM, (hd + 1) * A_HEAD_DIM)
            out_ref[:, sl] = _rms(p[:, sl], gain_ref[...]) * mult


def _inproj(x2, layer, e, g, w_in, q_g, k_g):
    rows = x2.shape[0]
    half = jax.ShapeDtypeStruct((rows, MIX_HALF), F32)
    half_spec = pl.BlockSpec((ROW_TILE, MIX_HALF), lambda i: (i, 0))
    gain_spec = pl.BlockSpec((None, 1, A_HEAD_DIM), lambda i: (e, 0, 0))
    return pl.pallas_call(
        _inproj_kernel,
        grid=(rows // ROW_TILE,),
        in_specs=[
            pl.BlockSpec((ROW_TILE, D_MODEL), lambda i: (i, 0)),
            pl.BlockSpec((None, 1, D_MODEL), lambda i: (layer, 0, 0)),
            _resident((None, D_MODEL, IN_COLS), lambda i: (e, 0, 0)),
            gain_spec, gain_spec,
        ],
        out_specs=[half_spec] * 5,
        out_shape=[half] * 5,
        compiler_params=_params("parallel"),
        name="inproj",
    )(x2, g, w_in, q_g, k_g)


def _attn_kernel(q_ref, k_ref, v_ref, o_ref, m_ref, l_ref, acc_ref, *, seq):
    jq = lax.broadcasted_iota(jnp.int32, (A_BAND, A_BAND), 0)
    jk = lax.broadcasted_iota(jnp.int32, (A_BAND, A_BAND), 1)
    own_mask = jk <= jq
    prev_mask = jk >= jq
    neg = -jnp.inf
    lanes = (A_BAND, A_HEAD_DIM)

    def rows(start, d):
        if d == 1:
            return pl.ds(start, A_BAND)
        return pl.ds(start, A_BAND, stride=d)

    def load_kv(start, d):
        r = rows(start, d)
        return k_ref[r, :].astype(BF16), v_ref[r, :].astype(BF16)

    def chunk(start, d, k_own, v_own, k_prev, v_prev, mode):
        r = rows(start, d)
        q = q_ref[r, :].astype(BF16)
        s = jnp.where(own_mask, _dot_nt(q, k_own), neg)
        m = jnp.max(s, axis=-1, keepdims=True)
        if k_prev is not None:
            sp = jnp.where(prev_mask, _dot_nt(q, k_prev), neg)
            m = jnp.maximum(m, jnp.max(sp, axis=-1, keepdims=True))
        p = jnp.exp(s - m)
        den = jnp.sum(p, axis=-1, keepdims=True)
        acc = _dot(p.astype(BF16), v_own)
        if k_prev is not None:
            pp = jnp.exp(sp - m)
            den = den + jnp.sum(pp, axis=-1, keepdims=True)
            acc = acc + _dot(pp.astype(BF16), v_prev)
        m = jnp.broadcast_to(m, lanes)
        den = jnp.broadcast_to(den, lanes)
        if mode != "init":
            m_old = m_ref[r, :]
            m_new = jnp.maximum(m_old, m)
            a_old = jnp.exp(m_old - m_new)
            a_cur = jnp.exp(m - m_new)
            den = a_old * l_ref[r, :] + a_cur * den
            acc = a_old * acc_ref[r, :] + a_cur * acc
            m = m_new
        if mode == "final":
            o_ref[r, :] = (acc / den).astype(o_ref.dtype)
        else:
            m_ref[r, :] = m
            l_ref[r, :] = den
            acc_ref[r, :] = acc

    def run_pass(d, mode):
        n_chunks = seq // (A_BAND * d)
        pitch = A_BAND * d

        def residue(r, carry):
            k0, v0 = load_kv(r, d)
            chunk(r, d, k0, v0, None, None, mode)

            def step(c, kv_prev):
                start = r + c * pitch
                if d == 1:
                    start = pl.multiple_of(start, A_BAND)
                k1, v1 = load_kv(start, d)
                chunk(start, d, k1, v1, kv_prev[0], kv_prev[1], mode)
                return (k1, v1)

            lax.fori_loop(1, n_chunks, step, (k0, v0))
            return carry

        if d == 1:
            residue(0, 0)
        else:
            lax.fori_loop(0, d, residue, 0)

    modes = ("init",) + ("merge",) * (len(A_DILATIONS) - 2) + ("final",)
    for d, mode in zip(A_DILATIONS, modes):
        run_pass(d, mode)


def _attn(q, k, v):
    batch, seq, _ = q.shape
    assert seq % (A_BAND * max(A_DILATIONS)) == 0
    spec = pl.BlockSpec((None, seq, A_HEAD_DIM), lambda b, h: (b, 0, h))
    return pl.pallas_call(
        functools.partial(_attn_kernel, seq=seq),
        grid=(batch, A_HEADS),
        in_specs=[spec, spec, spec],
        out_specs=spec,
        out_shape=jax.ShapeDtypeStruct((batch, seq, MIX_HALF), BF16),
        scratch_shapes=[pltpu.VMEM((seq, A_HEAD_DIM), F32)] * 3,
        compiler_params=_params("parallel", "parallel"),
        name="dilated_attn",
    )(q, k, v)


def _gelu_tanh(x):
    return 0.5 * x * (1.0 + jnp.tanh(0.7978845608028654 * (x + 0.044715 * x * x * x)))


def _rglru_kernel(xr_ref, gate_ref, cw_ref, cb_ref, wg_ref, gab_ref, gxb_ref, lam_ref,
                  y_ref, xpad_ref, a_ref, b_ref, h_ref, carry_ref):
    halo = SUBLANES

    @pl.when(pl.program_id(1) == 0)
    def _():
        xpad_ref[0:halo, :] = jnp.zeros((halo, RG_WIDTH), F32)
        carry_ref[...] = jnp.zeros((SUBLANES, RG_WIDTH), F32)

    xr = xr_ref[...]
    xpad_ref[halo:halo + RG_TILE, :] = xr
    xc = cb_ref[...] + cw_ref[CONV_WIDTH - 1:CONV_WIDTH, :] * xr
    for j in range(1, CONV_WIDTH):
        xc = xc + cw_ref[CONV_WIDTH - 1 - j:CONV_WIDTH - j, :] * xpad_ref[pl.ds(halo - j, RG_TILE), :]
    xpad_ref[0:halo, :] = xr[RG_TILE - halo:RG_TILE, :]

    lam = -lam_ref[...]
    softplus = jnp.maximum(lam, 0.0) + jnp.log1p(jnp.exp(-jnp.abs(lam)))
    row = lax.broadcasted_iota(jnp.int32, (RG_TILE, RG_BLOCK_DIM), 0) & (SUBLANES - 1)
    for g in range(RG_BLOCKS):
        sl = slice(g * RG_BLOCK_DIM, (g + 1) * RG_BLOCK_DIM)
        xg = xc[:, sl]
        rg = _dot(xg.astype(BF16), wg_ref[g])
        r = jax.nn.sigmoid(rg[:, :RG_BLOCK_DIM] + gab_ref[:, sl])
        i = jax.nn.sigmoid(rg[:, RG_BLOCK_DIM:] + gxb_ref[:, sl])
        log_a = (-RG_C) * r * softplus[:, sl]
        a = jnp.exp(log_a)
        th = jnp.tanh(log_a)
        u = jnp.sqrt(-2.0 * th / (1.0 - th)) * (i * xg)
        k = 1
        while k < SUBLANES:
            valid = row >= k
            a_prev = pltpu.roll(a, k, 0)
            u_prev = pltpu.roll(u, k, 0)
            u = jnp.where(valid, a * u_prev + u, u)
            a = jnp.where(valid, a * a_prev, a)
            k *= 2
        a_ref[:, sl] = a
        b_ref[:, sl] = u

    def group(gi, carry):
        r8 = pl.ds(pl.multiple_of(gi * SUBLANES, SUBLANES), SUBLANES)
        h = b_ref[r8, :] + a_ref[r8, :] * carry
        h_ref[r8, :] = h
        return jnp.broadcast_to(h[SUBLANES - 1:SUBLANES, :], (SUBLANES, RG_WIDTH))

    carry_ref[...] = lax.fori_loop(0, RG_TILE // SUBLANES, group, carry_ref[...])
    y_ref[...] = (h_ref[...] * _gelu_tanh(gate_ref[...])).astype(y_ref.dtype)


def _rglru(xr, gate, e, conv_w, conv_b, w_gates, ga_b, gx_b, lam):
    batch, seq, _ = xr.shape
    tile = pl.BlockSpec((None, RG_TILE, RG_WIDTH), lambda b, t: (b, t, 0))
    vec = pl.BlockSpec((None, 1, RG_WIDTH), lambda b, t: (e, 0, 0))
    return pl.pallas_call(
        _rglru_kernel,
        grid=(batch, seq // RG_TILE),
        in_specs=[
            tile, tile,
            pl.BlockSpec((None, CONV_WIDTH, RG_WIDTH), lambda b, t: (e, 0, 0)),
            vec,
            pl.BlockSpec((None, RG_BLOCKS, RG_BLOCK_DIM, 2 * RG_BLOCK_DIM), lambda b, t: (e, 0, 0, 0)),
            vec, vec, vec,
        ],
        out_specs=tile,
        out_shape=jax.ShapeDtypeStruct((batch, seq, RG_WIDTH), BF16),
        scratch_shapes=[
            pltpu.VMEM((RG_TILE + SUBLANES, RG_WIDTH), F32),
            pltpu.VMEM((RG_TILE, RG_WIDTH), F32),
            pltpu.VMEM((RG_TILE, RG_WIDTH), F32),
            pltpu.VMEM((RG_TILE, RG_WIDTH), F32),
            pltpu.VMEM((SUBLANES, RG_WIDTH), F32),
        ],
        compiler_params=_params("arbitrary", "arbitrary"),
        name="rglru",
    )(xr, gate, conv_w, conv_b, w_gates, ga_b, gx_b, lam)


def _outproj_kernel(x_ref, oa_ref, yb_ref, w_ref, o_ref):
    mixed = jnp.concatenate([oa_ref[...], yb_ref[...]], axis=-1)
    o_ref[...] = x_ref[...] + _dot(mixed, w_ref[...])


def _outproj(x2, o_a, y_b, e, w_out):
    rows = x2.shape[0]
    half_spec = pl.BlockSpec((ROW_TILE, MIX_HALF), lambda i: (i, 0))
    return pl.pallas_call(
        _outproj_kernel,
        grid=(rows // ROW_TILE,),
        in_specs=[
            pl.BlockSpec((ROW_TILE, D_MODEL), lambda i: (i, 0)),
            half_spec, half_spec,
            _resident((None, D_MODEL, D_MODEL), lambda i: (e, 0, 0)),
        ],
        out_specs=pl.BlockSpec((ROW_TILE, D_MODEL), lambda i: (i, 0)),
        out_shape=jax.ShapeDtypeStruct(x2.shape, F32),
        compiler_params=_params("parallel"),
        name="outproj",
    )(x2, o_a, y_b, w_out)


def _pool_kernel(x_ref, g_ref, pw_ref, sc_ref, o_ref, hpad_ref):
    t = pl.program_id(1)

    @pl.when(t == 0)
    def _():
        hpad_ref[0:POOL_HALO, :] = jnp.zeros((POOL_HALO, D_MODEL), F32)

    x = x_ref[...]
    h = _rms(x, g_ref[...])
    hpad_ref[POOL_HALO:POOL_HALO + POOL_TILE, :] = h
    pos = t * POOL_TILE + lax.broadcasted_iota(jnp.int32, (POOL_TILE, 1), 0)
    s = hpad_ref[...]
    outs = []
    width = 1
    for gi, w in enumerate(POOL_WINDOWS):
        while width < w:
            s = s + pltpu.roll(s, width, 0)
            width *= 2
        cnt = jnp.minimum(pos + 1, w).astype(F32)
        d = s[POOL_HALO:, :POOL_GROUP_DIM] / cnt - h[:, gi * POOL_GROUP_DIM:(gi + 1) * POOL_GROUP_DIM]
        outs.append(_dot(d.astype(BF16), pw_ref[gi]))
        if gi + 1 < len(POOL_WINDOWS):
            s = s[:, POOL_GROUP_DIM:]
    o_ref[...] = x + jnp.concatenate(outs, axis=-1) * sc_ref[...]
    hpad_ref[0:POOL_HALO, :] = h[POOL_TILE - POOL_HALO:, :]


def _pool(x, layer, o, g, pool_w, scale):
    batch, seq, _ = x.shape
    tile = pl.BlockSpec((None, POOL_TILE, D_MODEL), lambda b, t: (b, t, 0))
    return pl.pallas_call(
        _pool_kernel,
        grid=(batch, seq // POOL_TILE),
        in_specs=[
            tile,
            pl.BlockSpec((None, 1, D_MODEL), lambda b, t: (layer, 0, 0)),
            pl.BlockSpec((None, len(POOL_WINDOWS), POOL_GROUP_DIM, POOL_GROUP_DIM),
                         lambda b, t: (o, 0, 0, 0)),
            pl.BlockSpec((None, 1, D_MODEL), lambda b, t: (o, 0, 0)),
        ],
        out_specs=tile,
        out_shape=jax.ShapeDtypeStruct(x.shape, F32),
        scratch_shapes=[pltpu.VMEM((POOL_HALO + POOL_TILE, D_MODEL), F32)],
        compiler_params=_params("arbitrary", "arbitrary"),
        name="pool_mixer",
    )(x, g, pool_w, scale)


def kernel(x, mem, mem_norm_g, mix_norm_g, xattn_norm_g, mlp_norm_g, ev_w_in, ev_q_norm_g, ev_k_norm_g, ev_conv_w, ev_conv_b, ev_gate_a_w, ev_gate_a_b, ev_gate_x_w, ev_gate_x_b, ev_lambda, ev_w_out, od_pool_w, od_scale, xa_w_q, xa_w_kv, xa_q_norm_g, xa_k_norm_g, xa_w_o, mlp_w1, mlp_w2):
    batch, seq, _ = x.shape
    rows = batch * seq

    def vec3(a):
        return a.reshape(a.shape[0], 1, a.shape[1])

    w_in, w_out = ev_w_in.astype(BF16), ev_w_out.astype(BF16)
    w_gates = jnp.concatenate([ev_gate_a_w, ev_gate_x_w], axis=-1).astype(BF16)
    pool_w = od_pool_w.astype(BF16)
    w_q, w_kv, w_o = xa_w_q.astype(BF16), xa_w_kv.astype(BF16), xa_w_o.astype(BF16)
    w1, w2 = mlp_w1.astype(BF16), mlp_w2.astype(BF16)
    mix_g, xa_g, mlp_g = vec3(mix_norm_g), vec3(xattn_norm_g), vec3(mlp_norm_g)
    ev_qg, ev_kg = vec3(ev_q_norm_g), vec3(ev_k_norm_g)
    xa_qg = vec3(xa_q_norm_g)
    conv_b, ga_b, gx_b, lam = vec3(ev_conv_b), vec3(ev_gate_a_b), vec3(ev_gate_x_b), vec3(ev_lambda)
    od_sc = vec3(od_scale)

    kt, v_mem = _memkv(mem, mem_norm_g, w_kv, xa_k_norm_g)

    for l in range(DEPTH):
        if l % 2 == 0:
            e = l // 2
            q, k, v, xr, gate = _inproj(x.reshape(rows, D_MODEL), l, e, mix_g, w_in, ev_qg, ev_kg)
            half = (batch, seq, MIX_HALF)
            o_a = _attn(q.reshape(half), k.reshape(half), v.reshape(half))
            y_b = _rglru(xr.reshape(half), gate.reshape(half), e, ev_conv_w, conv_b, w_gates,
                         ga_b, gx_b, lam)
            x = _outproj(x.reshape(rows, D_MODEL), o_a.reshape(rows, MIX_HALF),
                         y_b.reshape(rows, MIX_HALF), e, w_out).reshape(batch, seq, D_MODEL)
        else:
            o = l // 2
            x = _pool(x, l, o, mix_g, pool_w, od_sc)
        x = _xattn(x, l, xa_g, w_q, xa_qg, kt, v_mem, w_o)
        x = _mlp(x.reshape(rows, D_MODEL), l, mlp_g, w1, w2).reshape(batch, seq, D_MODEL)
    return x
```

```python
import functools

import jax
import jax.numpy as jnp
from jax import lax
from jax.experimental import pallas as pl
from jax.experimental.pallas import tpu as pltpu

D_MODEL = 1024
DEPTH = 4
N_MEM = 256
MIX_HALF = D_MODEL // 2
A_HEADS = 4
A_HEAD_DIM = MIX_HALF // A_HEADS
A_BAND = 128
A_DILATIONS = (16, 4, 1)
A_CHUNKS_PER_STEP = 8
RG_WIDTH = MIX_HALF
RG_BLOCKS = 4
RG_BLOCK_DIM = RG_WIDTH // RG_BLOCKS
RG_C = 8.0
CONV_WIDTH = 4
POOL_WINDOWS = (2, 4, 8, 16)
POOL_GROUP_DIM = D_MODEL // len(POOL_WINDOWS)
POOL_HALO = 16
XA_HEADS = 4
XA_HEAD_DIM = D_MODEL // XA_HEADS
D_FF = 4 * D_MODEL
IN_COLS = 3 * MIX_HALF + 2 * RG_WIDTH
EPS = 1e-6

SUBLANES = 8
VMEM_LIMIT = 56 * 1024 * 1024

ROW_TILE = 512
MLP_CHUNK = 1024
RG_TILE = 256
POOL_TILE = 512

BF16 = jnp.bfloat16
F32 = jnp.float32


def _rms(x, g):
    ms = jnp.mean(x * x, axis=-1, keepdims=True)
    return x * lax.rsqrt(ms + EPS) * g


def _dot(a, b):
    return jnp.dot(a, b, preferred_element_type=F32)


def _dot_nt(a, b):
    return lax.dot_general(a, b, (((1,), (1,)), ((), ())), preferred_element_type=F32)


def _resident(shape, index_map):
    return pl.BlockSpec(shape, index_map, pipeline_mode=pl.Buffered(1))


def _params(*sem):
    return pltpu.CompilerParams(dimension_semantics=sem, vmem_limit_bytes=VMEM_LIMIT)


def _memkv_kernel(mem_ref, mg_ref, wkv_ref, kg_ref, kt_ref, v_ref):
    mem_n = _rms(mem_ref[...], mg_ref[...]).astype(BF16)
    kv = _dot(mem_n, wkv_ref[...])
    scale = XA_HEAD_DIM ** -0.5
    for h in range(XA_HEADS):
        sl = slice(h * XA_HEAD_DIM, (h + 1) * XA_HEAD_DIM)
        kh = _rms(kv[:, sl], kg_ref[...]) * scale
        kt_ref[sl, :] = kh.T.astype(BF16)
    v_ref[...] = kv[:, D_MODEL:].astype(BF16)


def _memkv(mem, mem_norm_g, w_kv, k_g):
    batch = mem.shape[0]
    return pl.pallas_call(
        _memkv_kernel,
        grid=(DEPTH, batch),
        in_specs=[
            pl.BlockSpec((None, N_MEM, D_MODEL), lambda l, b: (b, 0, 0)),
            pl.BlockSpec((1, D_MODEL), lambda l, b: (0, 0)),
            pl.BlockSpec((None, D_MODEL, 2 * D_MODEL), lambda l, b: (l, 0, 0)),
            pl.BlockSpec((None, 1, XA_HEAD_DIM), lambda l, b: (l, 0, 0)),
        ],
        out_specs=[
            pl.BlockSpec((None, None, D_MODEL, N_MEM), lambda l, b: (l, b, 0, 0)),
            pl.BlockSpec((None, None, N_MEM, D_MODEL), lambda l, b: (l, b, 0, 0)),
        ],
        out_shape=[
            jax.ShapeDtypeStruct((DEPTH, batch, D_MODEL, N_MEM), BF16),
            jax.ShapeDtypeStruct((DEPTH, batch, N_MEM, D_MODEL), BF16),
        ],
        compiler_params=_params("arbitrary", "arbitrary"),
        name="memkv",
    )(mem, mem_norm_g.reshape(1, D_MODEL), w_kv, k_g.reshape(DEPTH, 1, XA_HEAD_DIM))


def _xattn_kernel(x_ref, g_ref, wq_ref, qg_ref, kt_ref, v_ref, wo_ref, o_ref):
    x = x_ref[...]
    h = _rms(x, g_ref[...]).astype(BF16)
    q = _dot(h, wq_ref[...])
    outs = []
    for hd in range(XA_HEADS):
        sl = slice(hd * XA_HEAD_DIM, (hd + 1) * XA_HEAD_DIM)
        qh = _rms(q[:, sl], qg_ref[...]).astype(BF16)
        s = _dot(qh, kt_ref[sl, :])
        m = jnp.max(s, axis=-1, keepdims=True)
        p = jnp.exp(s - m)
        den = jnp.sum(p, axis=-1, keepdims=True)
        oh = _dot(p.astype(BF16), v_ref[:, sl])
        outs.append((oh / den).astype(BF16))
    o = jnp.concatenate(outs, axis=-1)
    o_ref[...] = x + _dot(o, wo_ref[...])


def _xattn(x, layer, g, w_q, q_g, kt, v, w_o):
    batch, seq, _ = x.shape
    return pl.pallas_call(
        _xattn_kernel,
        grid=(batch, seq // ROW_TILE),
        in_specs=[
            pl.BlockSpec((None, ROW_TILE, D_MODEL), lambda b, t: (b, t, 0)),
            pl.BlockSpec((None, 1, D_MODEL), lambda b, t: (layer, 0, 0)),
            _resident((None, D_MODEL, D_MODEL), lambda b, t: (layer, 0, 0)),
            pl.BlockSpec((None, 1, XA_HEAD_DIM), lambda b, t: (layer, 0, 0)),
            pl.BlockSpec((None, None, D_MODEL, N_MEM), lambda b, t: (layer, b, 0, 0)),
            pl.BlockSpec((None, None, N_MEM, D_MODEL), lambda b, t: (layer, b, 0, 0)),
            _resident((None, D_MODEL, D_MODEL), lambda b, t: (layer, 0, 0)),
        ],
        out_specs=pl.BlockSpec((None, ROW_TILE, D_MODEL), lambda b, t: (b, t, 0)),
        out_shape=jax.ShapeDtypeStruct(x.shape, F32),
        compiler_params=_params("parallel", "parallel"),
        name="xattn",
    )(x, g, w_q, q_g, kt, v, w_o)


def _mlp_kernel(x_ref, g_ref, w1_ref, w2_ref, o_ref):
    x = x_ref[...]
    h = _rms(x, g_ref[...]).astype(BF16)
    acc = x
    for c in range(D_FF // MLP_CHUNK):
        sl = slice(c * MLP_CHUNK, (c + 1) * MLP_CHUNK)
        a = jnp.maximum(_dot(h, w1_ref[:, sl]), 0.0)
        acc = acc + _dot((a * a).astype(BF16), w2_ref[sl, :])
    o_ref[...] = acc


def _mlp(x2, layer, g, w1, w2):
    rows = x2.shape[0]
    return pl.pallas_call(
        _mlp_kernel,
        grid=(rows // ROW_TILE,),
        in_specs=[
            pl.BlockSpec((ROW_TILE, D_MODEL), lambda i: (i, 0)),
            pl.BlockSpec((None, 1, D_MODEL), lambda i: (layer, 0, 0)),
            _resident((None, D_MODEL, D_FF), lambda i: (layer, 0, 0)),
            _resident((None, D_FF, D_MODEL), lambda i: (layer, 0, 0)),
        ],
        out_specs=pl.BlockSpec((ROW_TILE, D_MODEL), lambda i: (i, 0)),
        out_shape=jax.ShapeDtypeStruct(x2.shape, F32),
        compiler_params=_params("parallel"),
        name="mlp",
    )(x2, g, w1, w2)


def _inproj_kernel(x_ref, g_ref, w_ref, qg_ref, kg_ref, q_ref, k_ref, v_ref, xr_ref, gate_ref):
    h = _rms(x_ref[...], g_ref[...]).astype(BF16)
    scale = A_HEAD_DIM ** -0.5
    for j, (out_ref, gain_ref, mult) in enumerate(
            ((q_ref, qg_ref, scale), (k_ref, kg_ref, 1.0), (v_ref, None, None),
             (xr_ref, None, None), (gate_ref, None, None))):
        p = _dot(h, w_ref[:, j * MIX_HALF:(j + 1) * MIX_HALF])
        if gain_ref is None:
            out_ref[...] = p
            continue
        for hd in range(A_HEADS):
            sl = slice(hd * A_HEAD_DIM, (hd + 1) * A_HEAD_DIM)
            out_ref[:, sl] = _rms(p[:, sl], gain_ref[...]) * mult


def _inproj(x2, layer, e, g, w_in, q_g, k_g):
    rows = x2.shape[0]
    half = jax.ShapeDtypeStruct((rows, MIX_HALF), F32)
    half_spec = pl.BlockSpec((ROW_TILE, MIX_HALF), lambda i: (i, 0))
    gain_spec = pl.BlockSpec((None, 1, A_HEAD_DIM), lambda i: (e, 0, 0))
    return pl.pallas_call(
        _inproj_kernel,
        grid=(rows // ROW_TILE,),
        in_specs=[
            pl.BlockSpec((ROW_TILE, D_MODEL), lambda i: (i, 0)),
            pl.BlockSpec((None, 1, D_MODEL), lambda i: (layer, 0, 0)),
            _resident((None, D_MODEL, IN_COLS), lambda i: (e, 0, 0)),
            gain_spec, gain_spec,
        ],
        out_specs=[half_spec] * 5,
        out_shape=[half] * 5,
        compiler_params=_params("parallel"),
        name="inproj",
    )(x2, g, w_in, q_g, k_g)


def _attn_kernel(q_ref, k_ref, v_ref, o_ref, m_ref, l_ref, acc_ref, *, seq):
    jq = lax.broadcasted_iota(jnp.int32, (A_BAND, A_BAND), 0)
    jk = lax.broadcasted_iota(jnp.int32, (A_BAND, A_BAND), 1)
    own_mask = jk <= jq
    prev_mask = jk >= jq
    neg = -jnp.inf
    lanes = (A_BAND, A_HEAD_DIM)
    ones = jnp.ones(lanes, BF16)

    def rows(start, d):
        if d == 1:
            return pl.ds(pl.multiple_of(start, A_BAND), A_BAND)
        return pl.ds(start, A_BAND, stride=d)

    def load_kv(start, d):
        r = rows(start, d)
        return (k_ref[r, :].astype(BF16),
                jnp.concatenate([v_ref[r, :].astype(BF16), ones], axis=1))

    def process(jobs, d, mode):
        rs = [rows(start, d) for start, _, _ in jobs]
        qs = [q_ref[r, :].astype(BF16) for r in rs]
        scores = []
        for q, (_, own, prev) in zip(qs, jobs):
            s = jnp.where(own_mask, _dot_nt(q, own[0]), neg)
            sp = None if prev is None else jnp.where(prev_mask, _dot_nt(q, prev[0]), neg)
            scores.append((s, sp))
        ms = [jnp.max(s if sp is None else jnp.maximum(s, sp), axis=-1, keepdims=True)
              for s, sp in scores]
        pvs = []
        for (s, sp), m, (_, own, prev) in zip(scores, ms, jobs):
            pv = _dot(jnp.exp(s - m).astype(BF16), own[1])
            if sp is not None:
                pv = pv + _dot(jnp.exp(sp - m).astype(BF16), prev[1])
            pvs.append(pv)
        for r, m, pv in zip(rs, ms, pvs):
            acc, den = pv[:, :A_HEAD_DIM], pv[:, A_HEAD_DIM:]
            m = jnp.broadcast_to(m, lanes)
            if mode != "init":
                m_old = m_ref[r, :]
                m_new = jnp.maximum(m_old, m)
                a_old = jnp.exp(m_old - m_new)
                a_cur = jnp.exp(m - m_new)
                den = a_old * l_ref[r, :] + a_cur * den
                acc = a_old * acc_ref[r, :] + a_cur * acc
                m = m_new
            if mode == "final":
                o_ref[r, :] = (acc / den).astype(o_ref.dtype)
            else:
                m_ref[r, :] = m
                l_ref[r, :] = den
                acc_ref[r, :] = acc

    def jobs_of(r, c0, n, d, first):
        pitch = A_BAND * d
        kv = [None if first else load_kv(r + (c0 - 1) * pitch, d)]
        kv += [load_kv(r + (c0 + j) * pitch, d) for j in range(n)]
        return [(r + (c0 + j) * pitch, kv[j + 1], kv[j]) for j in range(n)]

    def group(r, c0, n, d, mode, first):
        process(jobs_of(r, c0, n, d, first), d, mode)

    def run_pass(d, mode):
        n_chunks = seq // (A_BAND * d)
        n = min(n_chunks, A_CHUNKS_PER_STEP)
        if n_chunks == n:
            per_step = A_CHUNKS_PER_STEP // n

            def step(i, carry):
                jobs = []
                for j in range(per_step):
                    jobs += jobs_of(i * per_step + j, 0, n, d, True)
                process(jobs, d, mode)
                return carry

            lax.fori_loop(0, d // per_step, step, 0)
            return

        def residue(r, carry):
            group(r, 0, n, d, mode, True)

            def step(i, c):
                group(r, i * n, n, d, mode, False)
                return c

            return lax.fori_loop(1, n_chunks // n, step, carry)

        if d == 1:
            residue(0, 0)
        else:
            lax.fori_loop(0, d, residue, 0)

    modes = ("init",) + ("merge",) * (len(A_DILATIONS) - 2) + ("final",)
    for d, mode in zip(A_DILATIONS, modes):
        run_pass(d, mode)


def _attn(q, k, v):
    batch, seq, _ = q.shape
    assert seq % (A_BAND * max(A_DILATIONS)) == 0
    spec = pl.BlockSpec((None, seq, A_HEAD_DIM), lambda b, h: (b, 0, h))
    return pl.pallas_call(
        functools.partial(_attn_kernel, seq=seq),
        grid=(batch, A_HEADS),
        in_specs=[spec, spec, spec],
        out_specs=spec,
        out_shape=jax.ShapeDtypeStruct((batch, seq, MIX_HALF), BF16),
        scratch_shapes=[pltpu.VMEM((seq, A_HEAD_DIM), F32)] * 3,
        compiler_params=_params("parallel", "parallel"),
        name="dilated_attn",
    )(q, k, v)


def _gelu_tanh(x):
    return 0.5 * x * (1.0 + jnp.tanh(0.7978845608028654 * (x + 0.044715 * x * x * x)))


def _rglru_kernel(xr_ref, gate_ref, cw_ref, cb_ref, wg_ref, gab_ref, gxb_ref, lam_ref,
                  y_ref, xpad_ref, a_ref, b_ref, h_ref, carry_ref):
    halo = SUBLANES

    @pl.when(pl.program_id(1) == 0)
    def _():
        xpad_ref[0:halo, :] = jnp.zeros((halo, RG_WIDTH), F32)
        carry_ref[...] = jnp.zeros((SUBLANES, RG_WIDTH), F32)

    xr = xr_ref[...]
    xpad_ref[halo:halo + RG_TILE, :] = xr
    xc = cb_ref[...] + cw_ref[CONV_WIDTH - 1:CONV_WIDTH, :] * xr
    for j in range(1, CONV_WIDTH):
        xc = xc + cw_ref[CONV_WIDTH - 1 - j:CONV_WIDTH - j, :] * xpad_ref[pl.ds(halo - j, RG_TILE), :]
    xpad_ref[0:halo, :] = xr[RG_TILE - halo:RG_TILE, :]

    lam = -lam_ref[...]
    softplus = jnp.maximum(lam, 0.0) + jnp.log1p(jnp.exp(-jnp.abs(lam)))
    row = lax.broadcasted_iota(jnp.int32, (RG_TILE, RG_BLOCK_DIM), 0) & (SUBLANES - 1)
    for g in range(RG_BLOCKS):
        sl = slice(g * RG_BLOCK_DIM, (g + 1) * RG_BLOCK_DIM)
        xg = xc[:, sl]
        rg = _dot(xg.astype(BF16), wg_ref[g])
        r = jax.nn.sigmoid(rg[:, :RG_BLOCK_DIM] + gab_ref[:, sl])
        i = jax.nn.sigmoid(rg[:, RG_BLOCK_DIM:] + gxb_ref[:, sl])
        log_a = (-RG_C) * r * softplus[:, sl]
        a = jnp.exp(log_a)
        th = jnp.tanh(log_a)
        u = jnp.sqrt(-2.0 * th / (1.0 - th)) * (i * xg)
        k = 1
        while k < SUBLANES:
            valid = row >= k
            a_prev = pltpu.roll(a, k, 0)
            u_prev = pltpu.roll(u, k, 0)
            u = jnp.where(valid, a * u_prev + u, u)
            a = jnp.where(valid, a * a_prev, a)
            k *= 2
        a_ref[:, sl] = a
        b_ref[:, sl] = u

    def group(gi, carry):
        r8 = pl.ds(pl.multiple_of(gi * SUBLANES, SUBLANES), SUBLANES)
        h = b_ref[r8, :] + a_ref[r8, :] * carry
        h_ref[r8, :] = h
        return jnp.broadcast_to(h[SUBLANES - 1:SUBLANES, :], (SUBLANES, RG_WIDTH))

    carry_ref[...] = lax.fori_loop(0, RG_TILE // SUBLANES, group, carry_ref[...])
    y_ref[...] = (h_ref[...] * _gelu_tanh(gate_ref[...])).astype(y_ref.dtype)


def _rglru(xr, gate, e, conv_w, conv_b, w_gates, ga_b, gx_b, lam):
    batch, seq, _ = xr.shape
    tile = pl.BlockSpec((None, RG_TILE, RG_WIDTH), lambda b, t: (b, t, 0))
    vec = pl.BlockSpec((None, 1, RG_WIDTH), lambda b, t: (e, 0, 0))
    return pl.pallas_call(
        _rglru_kernel,
        grid=(batch, seq // RG_TILE),
        in_specs=[
            tile, tile,
            pl.BlockSpec((None, CONV_WIDTH, RG_WIDTH), lambda b, t: (e, 0, 0)),
            vec,
            pl.BlockSpec((None, RG_BLOCKS, RG_BLOCK_DIM, 2 * RG_BLOCK_DIM), lambda b, t: (e, 0, 0, 0)),
            vec, vec, vec,
        ],
        out_specs=tile,
        out_shape=jax.ShapeDtypeStruct((batch, seq, RG_WIDTH), BF16),
        scratch_shapes=[
            pltpu.VMEM((RG_TILE + SUBLANES, RG_WIDTH), F32),
            pltpu.VMEM((RG_TILE, RG_WIDTH), F32),
            pltpu.VMEM((RG_TILE, RG_WIDTH), F32),
            pltpu.VMEM((RG_TILE, RG_WIDTH), F32),
            pltpu.VMEM((SUBLANES, RG_WIDTH), F32),
        ],
        compiler_params=_params("arbitrary", "arbitrary"),
        name="rglru",
    )(xr, gate, conv_w, conv_b, w_gates, ga_b, gx_b, lam)


def _outproj_kernel(x_ref, oa_ref, yb_ref, w_ref, o_ref):
    mixed = jnp.concatenate([oa_ref[...], yb_ref[...]], axis=-1)
    o_ref[...] = x_ref[...] + _dot(mixed, w_ref[...])


def _outproj(x2, o_a, y_b, e, w_out):
    rows = x2.shape[0]
    half_spec = pl.BlockSpec((ROW_TILE, MIX_HALF), lambda i: (i, 0))
    return pl.pallas_call(
        _outproj_kernel,
        grid=(rows // ROW_TILE,),
        in_specs=[
            pl.BlockSpec((ROW_TILE, D_MODEL), lambda i: (i, 0)),
            half_spec, half_spec,
            _resident((None, D_MODEL, D_MODEL), lambda i: (e, 0, 0)),
        ],
        out_specs=pl.BlockSpec((ROW_TILE, D_MODEL), lambda i: (i, 0)),
        out_shape=jax.ShapeDtypeStruct(x2.shape, F32),
        compiler_params=_params("parallel"),
        name="outproj",
    )(x2, o_a, y_b, w_out)


def _pool_kernel(x_ref, g_ref, pw_ref, sc_ref, o_ref, hpad_ref):
    t = pl.program_id(1)

    @pl.when(t == 0)
    def _():
        hpad_ref[0:POOL_HALO, :] = jnp.zeros((POOL_HALO, D_MODEL), F32)

    x = x_ref[...]
    h = _rms(x, g_ref[...])
    hpad_ref[POOL_HALO:POOL_HALO + POOL_TILE, :] = h
    pos = t * POOL_TILE + lax.broadcasted_iota(jnp.int32, (POOL_TILE, 1), 0)
    s = hpad_ref[...]
    outs = []
    width = 1
    for gi, w in enumerate(POOL_WINDOWS):
        while width < w:
            s = s + pltpu.roll(s, width, 0)
            width *= 2
        cnt = jnp.minimum(pos + 1, w).astype(F32)
        d = s[POOL_HALO:, :POOL_GROUP_DIM] / cnt - h[:, gi * POOL_GROUP_DIM:(gi + 1) * POOL_GROUP_DIM]
        outs.append(_dot(d.astype(BF16), pw_ref[gi]))
        if gi + 1 < len(POOL_WINDOWS):
            s = s[:, POOL_GROUP_DIM:]
    o_ref[...] = x + jnp.concatenate(outs, axis=-1) * sc_ref[...]
    hpad_ref[0:POOL_HALO, :] = h[POOL_TILE - POOL_HALO:, :]


def _pool(x, layer, o, g, pool_w, scale):
    batch, seq, _ = x.shape
    tile = pl.BlockSpec((None, POOL_TILE, D_MODEL), lambda b, t: (b, t, 0))
    return pl.pallas_call(
        _pool_kernel,
        grid=(batch, seq // POOL_TILE),
        in_specs=[
            tile,
            pl.BlockSpec((None, 1, D_MODEL), lambda b, t: (layer, 0, 0)),
            pl.BlockSpec((None, len(POOL_WINDOWS), POOL_GROUP_DIM, POOL_GROUP_DIM),
                         lambda b, t: (o, 0, 0, 0)),
            pl.BlockSpec((None, 1, D_MODEL), lambda b, t: (o, 0, 0)),
        ],
        out_specs=tile,
        out_shape=jax.ShapeDtypeStruct(x.shape, F32),
        scratch_shapes=[pltpu.VMEM((POOL_HALO + POOL_TILE, D_MODEL), F32)],
        compiler_params=_params("arbitrary", "arbitrary"),
        name="pool_mixer",
    )(x, g, pool_w, scale)


def kernel(x, mem, mem_norm_g, mix_norm_g, xattn_norm_g, mlp_norm_g, ev_w_in, ev_q_norm_g, ev_k_norm_g, ev_conv_w, ev_conv_b, ev_gate_a_w, ev_gate_a_b, ev_gate_x_w, ev_gate_x_b, ev_lambda, ev_w_out, od_pool_w, od_scale, xa_w_q, xa_w_kv, xa_q_norm_g, xa_k_norm_g, xa_w_o, mlp_w1, mlp_w2):
    batch, seq, _ = x.shape
    rows = batch * seq

    def vec3(a):
        return a.reshape(a.shape[0], 1, a.shape[1])

    w_in, w_out = ev_w_in.astype(BF16), ev_w_out.astype(BF16)
    w_gates = jnp.concatenate([ev_gate_a_w, ev_gate_x_w], axis=-1).astype(BF16)
    pool_w = od_pool_w.astype(BF16)
    w_q, w_kv, w_o = xa_w_q.astype(BF16), xa_w_kv.astype(BF16), xa_w_o.astype(BF16)
    w1, w2 = mlp_w1.astype(BF16), mlp_w2.astype(BF16)
    mix_g, xa_g, mlp_g = vec3(mix_norm_g), vec3(xattn_norm_g), vec3(mlp_norm_g)
    ev_qg, ev_kg = vec3(ev_q_norm_g), vec3(ev_k_norm_g)
    xa_qg = vec3(xa_q_norm_g)
    conv_b, ga_b, gx_b, lam = vec3(ev_conv_b), vec3(ev_gate_a_b), vec3(ev_gate_x_b), vec3(ev_lambda)
    od_sc = vec3(od_scale)

    kt, v_mem = _memkv(mem, mem_norm_g, w_kv, xa_k_norm_g)

    for l in range(DEPTH):
        if l % 2 == 0:
            e = l // 2
            q, k, v, xr, gate = _inproj(x.reshape(rows, D_MODEL), l, e, mix_g, w_in, ev_qg, ev_kg)
            half = (batch, seq, MIX_HALF)
            o_a = _attn(q.reshape(half), k.reshape(half), v.reshape(half))
            y_b = _rglru(xr.reshape(half), gate.reshape(half), e, ev_conv_w, conv_b, w_gates,
                         ga_b, gx_b, lam)
            x = _outproj(x.reshape(rows, D_MODEL), o_a.reshape(rows, MIX_HALF),
                         y_b.reshape(rows, MIX_HALF), e, w_out).reshape(batch, seq, D_MODEL)
        else:
            o = l // 2
            x = _pool(x, l, o, mix_g, pool_w, od_sc)
        x = _xattn(x, l, xa_g, w_q, xa_qg, kt, v_mem, w_o)
        x = _mlp(x.reshape(rows, D_MODEL), l, mlp_g, w1, w2).reshape(batch, seq, D_MODEL)
    return x
```

```python
import functools

import jax
import jax.numpy as jnp
from jax import lax
from jax.experimental import pallas as pl
from jax.experimental.pallas import tpu as pltpu

D_MODEL = 1024
DEPTH = 4
N_MEM = 256
MIX_HALF = D_MODEL // 2
A_HEADS = 4
A_HEAD_DIM = MIX_HALF // A_HEADS
A_BAND = 128
A_DILATIONS = (16, 4, 1)
A_CHUNKS_PER_STEP = 8
RG_WIDTH = MIX_HALF
RG_BLOCKS = 4
RG_BLOCK_DIM = RG_WIDTH // RG_BLOCKS
RG_C = 8.0
CONV_WIDTH = 4
POOL_WINDOWS = (2, 4, 8, 16)
POOL_GROUP_DIM = D_MODEL // len(POOL_WINDOWS)
POOL_HALO = 16
XA_HEADS = 4
XA_HEAD_DIM = D_MODEL // XA_HEADS
D_FF = 4 * D_MODEL
IN_COLS = 3 * MIX_HALF + 2 * RG_WIDTH
EPS = 1e-6
LOG2_E = 1.4426950408889634

SUBLANES = 8
VMEM_LIMIT = 56 * 1024 * 1024

ROW_TILE = 1024
MLP_CHUNK = 1024
RG_TILE = 512
RG_SUB = 256
POOL_TILE = 512

BF16 = jnp.bfloat16
F32 = jnp.float32


def _rms(x, g):
    ms = jnp.mean(x * x, axis=-1, keepdims=True)
    return x * lax.rsqrt(ms + EPS) * g


def _dot(a, b):
    return jnp.dot(a, b, preferred_element_type=F32)


def _dot_nt(a, b):
    return lax.dot_general(a, b, (((1,), (1,)), ((), ())), preferred_element_type=F32)


def _resident(shape, index_map):
    return pl.BlockSpec(shape, index_map, pipeline_mode=pl.Buffered(1))


def _params(*sem):
    return pltpu.CompilerParams(dimension_semantics=sem, vmem_limit_bytes=VMEM_LIMIT)


def _memkv_kernel(mem_ref, mg_ref, wkv_ref, kg_ref, kt_ref, v_ref):
    mem_n = _rms(mem_ref[...], mg_ref[...]).astype(BF16)
    kv = _dot(mem_n, wkv_ref[...])
    scale = XA_HEAD_DIM ** -0.5 * LOG2_E
    for h in range(XA_HEADS):
        sl = slice(h * XA_HEAD_DIM, (h + 1) * XA_HEAD_DIM)
        kh = _rms(kv[:, sl], kg_ref[...]) * scale
        kt_ref[sl, :] = kh.T.astype(BF16)
    v_ref[...] = kv[:, D_MODEL:].astype(BF16)


def _memkv(mem, mem_norm_g, w_kv, k_g):
    batch = mem.shape[0]
    return pl.pallas_call(
        _memkv_kernel,
        grid=(DEPTH, batch),
        in_specs=[
            pl.BlockSpec((None, N_MEM, D_MODEL), lambda l, b: (b, 0, 0)),
            pl.BlockSpec((1, D_MODEL), lambda l, b: (0, 0)),
            pl.BlockSpec((None, D_MODEL, 2 * D_MODEL), lambda l, b: (l, 0, 0)),
            pl.BlockSpec((None, 1, XA_HEAD_DIM), lambda l, b: (l, 0, 0)),
        ],
        out_specs=[
            pl.BlockSpec((None, None, D_MODEL, N_MEM), lambda l, b: (l, b, 0, 0)),
            pl.BlockSpec((None, None, N_MEM, D_MODEL), lambda l, b: (l, b, 0, 0)),
        ],
        out_shape=[
            jax.ShapeDtypeStruct((DEPTH, batch, D_MODEL, N_MEM), BF16),
            jax.ShapeDtypeStruct((DEPTH, batch, N_MEM, D_MODEL), BF16),
        ],
        compiler_params=_params("arbitrary", "arbitrary"),
        name="memkv",
    )(mem, mem_norm_g.reshape(1, D_MODEL), w_kv, k_g.reshape(DEPTH, 1, XA_HEAD_DIM))


def _xattn_kernel(x_ref, g_ref, wq_ref, qg_ref, kt_ref, v_ref, wo_ref, o_ref):
    x = x_ref[...]
    h = _rms(x, g_ref[...]).astype(BF16)
    q = _dot(h, wq_ref[...])
    outs = []
    for hd in range(XA_HEADS):
        sl = slice(hd * XA_HEAD_DIM, (hd + 1) * XA_HEAD_DIM)
        qh = _rms(q[:, sl], qg_ref[...]).astype(BF16)
        s = _dot(qh, kt_ref[sl, :])
        m = jnp.max(s, axis=-1, keepdims=True)
        p = jnp.exp2(s - m)
        den = jnp.sum(p, axis=-1, keepdims=True)
        oh = _dot(p.astype(BF16), v_ref[:, sl])
        outs.append((oh / den).astype(BF16))
    o = jnp.concatenate(outs, axis=-1)
    o_ref[...] = x + _dot(o, wo_ref[...])


def _xattn(x, layer, g, w_q, q_g, kt, v, w_o):
    batch, seq, _ = x.shape
    return pl.pallas_call(
        _xattn_kernel,
        grid=(batch, seq // ROW_TILE),
        in_specs=[
            pl.BlockSpec((None, ROW_TILE, D_MODEL), lambda b, t: (b, t, 0)),
            pl.BlockSpec((None, 1, D_MODEL), lambda b, t: (layer, 0, 0)),
            _resident((None, D_MODEL, D_MODEL), lambda b, t: (layer, 0, 0)),
            pl.BlockSpec((None, 1, XA_HEAD_DIM), lambda b, t: (layer, 0, 0)),
            pl.BlockSpec((None, None, D_MODEL, N_MEM), lambda b, t: (layer, b, 0, 0)),
            pl.BlockSpec((None, None, N_MEM, D_MODEL), lambda b, t: (layer, b, 0, 0)),
            _resident((None, D_MODEL, D_MODEL), lambda b, t: (layer, 0, 0)),
        ],
        out_specs=pl.BlockSpec((None, ROW_TILE, D_MODEL), lambda b, t: (b, t, 0)),
        out_shape=jax.ShapeDtypeStruct(x.shape, F32),
        compiler_params=_params("parallel", "parallel"),
        name="xattn",
    )(x, g, w_q, q_g, kt, v, w_o)


def _mlp_kernel(x_ref, g_ref, w1_ref, w2_ref, o_ref):
    x = x_ref[...]
    h = _rms(x, g_ref[...]).astype(BF16)
    acc = x
    for c in range(D_FF // MLP_CHUNK):
        sl = slice(c * MLP_CHUNK, (c + 1) * MLP_CHUNK)
        a = jnp.maximum(_dot(h, w1_ref[:, sl]), 0.0)
        acc = acc + _dot((a * a).astype(BF16), w2_ref[sl, :])
    o_ref[...] = acc


def _mlp(x2, layer, g, w1, w2):
    rows = x2.shape[0]
    return pl.pallas_call(
        _mlp_kernel,
        grid=(rows // ROW_TILE,),
        in_specs=[
            pl.BlockSpec((ROW_TILE, D_MODEL), lambda i: (i, 0)),
            pl.BlockSpec((None, 1, D_MODEL), lambda i: (layer, 0, 0)),
            _resident((None, D_MODEL, D_FF), lambda i: (layer, 0, 0)),
            _resident((None, D_FF, D_MODEL), lambda i: (layer, 0, 0)),
        ],
        out_specs=pl.BlockSpec((ROW_TILE, D_MODEL), lambda i: (i, 0)),
        out_shape=jax.ShapeDtypeStruct(x2.shape, F32),
        compiler_params=_params("parallel"),
        name="mlp",
    )(x2, g, w1, w2)


def _inproj_kernel(x_ref, g_ref, w_ref, qg_ref, kg_ref, q_ref, k_ref, v_ref, xr_ref, gate_ref):
    h = _rms(x_ref[...], g_ref[...]).astype(BF16)
    scale = A_HEAD_DIM ** -0.5 * LOG2_E
    for j, (out_ref, gain_ref, mult) in enumerate(
            ((q_ref, qg_ref, scale), (k_ref, kg_ref, 1.0), (v_ref, None, None),
             (xr_ref, None, None), (gate_ref, None, None))):
        p = _dot(h, w_ref[:, j * MIX_HALF:(j + 1) * MIX_HALF])
        if gain_ref is None:
            out_ref[...] = p
            continue
        for hd in range(A_HEADS):
            sl = slice(hd * A_HEAD_DIM, (hd + 1) * A_HEAD_DIM)
            out_ref[:, sl] = _rms(p[:, sl], gain_ref[...]) * mult


def _inproj(x2, layer, e, g, w_in, q_g, k_g):
    rows = x2.shape[0]
    half = jax.ShapeDtypeStruct((rows, MIX_HALF), F32)
    half_spec = pl.BlockSpec((ROW_TILE, MIX_HALF), lambda i: (i, 0))
    gain_spec = pl.BlockSpec((None, 1, A_HEAD_DIM), lambda i: (e, 0, 0))
    return pl.pallas_call(
        _inproj_kernel,
        grid=(rows // ROW_TILE,),
        in_specs=[
            pl.BlockSpec((ROW_TILE, D_MODEL), lambda i: (i, 0)),
            pl.BlockSpec((None, 1, D_MODEL), lambda i: (layer, 0, 0)),
            _resident((None, D_MODEL, IN_COLS), lambda i: (e, 0, 0)),
            gain_spec, gain_spec,
        ],
        out_specs=[half_spec] * 5,
        out_shape=[half] * 5,
        compiler_params=_params("parallel"),
        name="inproj",
    )(x2, g, w_in, q_g, k_g)


def _attn_kernel(q_ref, k_ref, v_ref, o_ref, m_ref, l_ref, acc_ref, *, seq):
    jq = lax.broadcasted_iota(jnp.int32, (A_BAND, A_BAND), 0)
    jk = lax.broadcasted_iota(jnp.int32, (A_BAND, A_BAND), 1)
    own_mask = jk <= jq
    jk2 = lax.broadcasted_iota(jnp.int32, (A_BAND, 2 * A_BAND), 1)
    jq2 = lax.broadcasted_iota(jnp.int32, (A_BAND, 2 * A_BAND), 0)
    both_mask = (jk2 >= jq2) & (jk2 <= jq2 + A_BAND)
    neg = -jnp.inf
    lanes = (A_BAND, A_HEAD_DIM)
    ones = jnp.ones(lanes, BF16)

    def rows(start, d):
        if d == 1:
            return pl.ds(pl.multiple_of(start, A_BAND), A_BAND)
        return pl.ds(start, A_BAND, stride=d)

    def load_kv(start, d):
        r = rows(start, d)
        return (k_ref[r, :].astype(BF16),
                jnp.concatenate([v_ref[r, :].astype(BF16), ones], axis=1))

    def process(jobs, d, mode):
        rs = [rows(start, d) for start, _, _ in jobs]
        qs = [q_ref[r, :].astype(BF16) for r in rs]
        scores = []
        for q, (_, own, prev) in zip(qs, jobs):
            if prev is None:
                s = jnp.where(own_mask, _dot_nt(q, own[0]), neg)
            else:
                s = _dot_nt(q, jnp.concatenate([prev[0], own[0]], axis=0))
                s = jnp.where(both_mask, s, neg)
            scores.append(s)
        ms = [jnp.max(s, axis=-1, keepdims=True) for s in scores]
        pvs = []
        for s, m, (_, own, prev) in zip(scores, ms, jobs):
            v = own[1] if prev is None else jnp.concatenate([prev[1], own[1]], axis=0)
            pvs.append(_dot(jnp.exp2(s - m).astype(BF16), v))
        for r, m, pv in zip(rs, ms, pvs):
            acc, den = pv[:, :A_HEAD_DIM], pv[:, A_HEAD_DIM:]
            m = jnp.broadcast_to(m, lanes)
            if mode != "init":
                m_old = m_ref[r, :]
                m_new = jnp.maximum(m_old, m)
                a_old = jnp.exp2(m_old - m_new)
                a_cur = jnp.exp2(m - m_new)
                den = a_old * l_ref[r, :] + a_cur * den
                acc = a_old * acc_ref[r, :] + a_cur * acc
                m = m_new
            if mode == "final":
                o_ref[r, :] = (acc / den).astype(o_ref.dtype)
            else:
                m_ref[r, :] = m
                l_ref[r, :] = den
                acc_ref[r, :] = acc

    def jobs_of(r, c0, n, d, first):
        pitch = A_BAND * d
        kv = [None if first else load_kv(r + (c0 - 1) * pitch, d)]
        kv += [load_kv(r + (c0 + j) * pitch, d) for j in range(n)]
        return [(r + (c0 + j) * pitch, kv[j + 1], kv[j]) for j in range(n)]

    def group(r, c0, n, d, mode, first):
        process(jobs_of(r, c0, n, d, first), d, mode)

    def run_pass(d, mode):
        n_chunks = seq // (A_BAND * d)
        n = min(n_chunks, A_CHUNKS_PER_STEP)
        if n_chunks == n:
            per_step = A_CHUNKS_PER_STEP // n

            def step(i, carry):
                jobs = []
                for j in range(per_step):
                    jobs += jobs_of(i * per_step + j, 0, n, d, True)
                process(jobs, d, mode)
                return carry

            lax.fori_loop(0, d // per_step, step, 0)
            return

        def residue(r, carry):
            group(r, 0, n, d, mode, True)

            def step(i, c):
                group(r, i * n, n, d, mode, False)
                return c

            return lax.fori_loop(1, n_chunks // n, step, carry)

        if d == 1:
            residue(0, 0)
        else:
            lax.fori_loop(0, d, residue, 0)

    modes = ("init",) + ("merge",) * (len(A_DILATIONS) - 2) + ("final",)
    for d, mode in zip(A_DILATIONS, modes):
        run_pass(d, mode)


def _attn(q, k, v):
    batch, seq, _ = q.shape
    assert seq % (A_BAND * max(A_DILATIONS)) == 0
    spec = pl.BlockSpec((None, seq, A_HEAD_DIM), lambda b, h: (b, 0, h))
    return pl.pallas_call(
        functools.partial(_attn_kernel, seq=seq),
        grid=(batch, A_HEADS),
        in_specs=[spec, spec, spec],
        out_specs=spec,
        out_shape=jax.ShapeDtypeStruct((batch, seq, MIX_HALF), BF16),
        scratch_shapes=[pltpu.VMEM((seq, A_HEAD_DIM), F32)] * 3,
        compiler_params=_params("parallel", "parallel"),
        name="dilated_attn",
    )(q, k, v)


def _gelu_tanh(x):
    c1 = -2.0 * 0.7978845608028654 * LOG2_E
    return x / (1.0 + jnp.exp2(x * (c1 + (c1 * 0.044715) * (x * x))))


def _rglru_out_kernel(x_ref, oa_ref, xr_ref, gate_ref, cw_ref, cb_ref, wg_ref, gab_ref, gxb_ref,
                      lam_ref, wout_ref, o_ref, xpad_ref, carry_ref):
    halo = SUBLANES
    group = (SUBLANES, RG_BLOCK_DIM)

    @pl.when(pl.program_id(1) == 0)
    def _():
        xpad_ref[0:halo, :] = jnp.zeros((halo, RG_WIDTH), F32)
        carry_ref[...] = jnp.zeros((SUBLANES, RG_WIDTH), F32)

    xpad_ref[halo:halo + RG_TILE, :] = xr_ref[...]
    lam = -lam_ref[...]
    softplus = jnp.maximum(lam, 0.0) + jnp.log1p(jnp.exp(-jnp.abs(lam)))
    grouped = (RG_SUB // SUBLANES, SUBLANES, RG_BLOCK_DIM)
    row = lax.broadcasted_iota(jnp.int32, grouped, 1)
    carry =[carry_ref[:, g * RG_BLOCK_DIM:(g + 1) * RG_BLOCK_DIM] for g in range(RG_BLOCKS)]

    for r0 in range(0, RG_TILE, RG_SUB):
        rs = slice(r0, r0 + RG_SUB)
        xc = cb_ref[...] + cw_ref[CONV_WIDTH - 1:CONV_WIDTH, :] * xr_ref[rs, :]
        for j in range(1, CONV_WIDTH):
            xc = xc + (cw_ref[CONV_WIDTH - 1 - j:CONV_WIDTH - j, :]
                       * xpad_ref[pl.ds(halo - j + r0, RG_SUB), :])
        mixed = [oa_ref[rs, :]]
        for g in range(RG_BLOCKS):
            sl = slice(g * RG_BLOCK_DIM, (g + 1) * RG_BLOCK_DIM)
            xg = xc[:, sl]
            rg = _dot(xg.astype(BF16), wg_ref[g])
            r = jax.nn.sigmoid(rg[:, :RG_BLOCK_DIM] + gab_ref[:, sl])
            i = jax.nn.sigmoid(rg[:, RG_BLOCK_DIM:] + gxb_ref[:, sl])
            log_a = (-RG_C) * r * softplus[:, sl]
            a = jnp.exp(log_a)
            th = jnp.tanh(log_a)
            u = lax.rsqrt((th - 1.0) / (2.0 * th)) * (i * xg)
            a = a.reshape(grouped)
            u = u.reshape(grouped)
            k = 1
            while k < SUBLANES:
                valid = row >= k
                a_prev = pltpu.roll(a, k, 1)
                u_prev = pltpu.roll(u, k, 1)
                u = jnp.where(valid, a * u_prev + u, u)
                a = jnp.where(valid, a * a_prev, a)
                k *= 2
            a = a.reshape(RG_SUB, RG_BLOCK_DIM)
            u = u.reshape(RG_SUB, RG_BLOCK_DIM)
            hs = []
            c = carry[g]
            for gi in range(0, RG_SUB, SUBLANES):
                h = u[gi:gi + SUBLANES, :] + a[gi:gi + SUBLANES, :] * c
                hs.append(h)
                c = jnp.broadcast_to(h[SUBLANES - 1:SUBLANES, :], group)
            carry[g] = c
            mixed.append((jnp.concatenate(hs, axis=0) * _gelu_tanh(gate_ref[rs, sl])).astype(BF16))
        o_ref[rs, :] = x_ref[rs, :] + _dot(jnp.concatenate(mixed, axis=-1), wout_ref[...])

    for g in range(RG_BLOCKS):
        carry_ref[:, g * RG_BLOCK_DIM:(g + 1) * RG_BLOCK_DIM] = carry[g]
    xpad_ref[0:halo, :] = xr_ref[RG_TILE - halo:RG_TILE, :]


def _rglru_out(x, o_a, xr, gate, e, conv_w, conv_b, w_gates, ga_b, gx_b, lam, w_out):
    batch, seq, _ = x.shape
    full = pl.BlockSpec((None, RG_TILE, D_MODEL), lambda b, t: (b, t, 0))
    half = pl.BlockSpec((None, RG_TILE, RG_WIDTH), lambda b, t: (b, t, 0))
    vec = pl.BlockSpec((None, 1, RG_WIDTH), lambda b, t: (e, 0, 0))
    return pl.pallas_call(
        _rglru_out_kernel,
        grid=(batch, seq // RG_TILE),
        in_specs=[
            full, half, half, half,
            pl.BlockSpec((None, CONV_WIDTH, RG_WIDTH), lambda b, t: (e, 0, 0)),
            vec,
            pl.BlockSpec((None, RG_BLOCKS, RG_BLOCK_DIM, 2 * RG_BLOCK_DIM), lambda b, t: (e, 0, 0, 0)),
            vec, vec, vec,
            _resident((None, D_MODEL, D_MODEL), lambda b, t: (e, 0, 0)),
        ],
        out_specs=full,
        out_shape=jax.ShapeDtypeStruct(x.shape, F32),
        scratch_shapes=[
            pltpu.VMEM((RG_TILE + SUBLANES, RG_WIDTH), F32),
            pltpu.VMEM((SUBLANES, RG_WIDTH), F32),
        ],
        compiler_params=_params("arbitrary", "arbitrary"),
        name="rglru_outproj",
    )(x, o_a, xr, gate, conv_w, conv_b, w_gates, ga_b, gx_b, lam, w_out)


def _pool_kernel(x_ref, g_ref, pw_ref, sc_ref, o_ref, hpad_ref):
    t = pl.program_id(1)

    @pl.when(t == 0)
    def _():
        hpad_ref[0:POOL_HALO, :] = jnp.zeros((POOL_HALO, D_MODEL), F32)

    x = x_ref[...]
    h = _rms(x, g_ref[...])
    hpad_ref[POOL_HALO:POOL_HALO + POOL_TILE, :] = h
    pos = t * POOL_TILE + lax.broadcasted_iota(jnp.int32, (POOL_TILE, 1), 0)
    s = hpad_ref[...]
    outs = []
    width = 1
    for gi, w in enumerate(POOL_WINDOWS):
        while width < w:
            s = s + pltpu.roll(s, width, 0)
            width *= 2
        cnt = jnp.minimum(pos + 1, w).astype(F32)
        d = s[POOL_HALO:, :POOL_GROUP_DIM] / cnt - h[:, gi * POOL_GROUP_DIM:(gi + 1) * POOL_GROUP_DIM]
        outs.append(_dot(d.astype(BF16), pw_ref[gi]))
        if gi + 1 < len(POOL_WINDOWS):
            s = s[:, POOL_GROUP_DIM:]
    o_ref[...] = x + jnp.concatenate(outs, axis=-1) * sc_ref[...]
    hpad_ref[0:POOL_HALO, :] = h[POOL_TILE - POOL_HALO:, :]


def _pool(x, layer, o, g, pool_w, scale):
    batch, seq, _ = x.shape
    tile = pl.BlockSpec((None, POOL_TILE, D_MODEL), lambda b, t: (b, t, 0))
    return pl.pallas_call(
        _pool_kernel,
        grid=(batch, seq // POOL_TILE),
        in_specs=[
            tile,
            pl.BlockSpec((None, 1, D_MODEL), lambda b, t: (layer, 0, 0)),
            pl.BlockSpec((None, len(POOL_WINDOWS), POOL_GROUP_DIM, POOL_GROUP_DIM),
                         lambda b, t: (o, 0, 0, 0)),
            pl.BlockSpec((None, 1, D_MODEL), lambda b, t: (o, 0, 0)),
        ],
        out_specs=tile,
        out_shape=jax.ShapeDtypeStruct(x.shape, F32),
        scratch_shapes=[pltpu.VMEM((POOL_HALO + POOL_TILE, D_MODEL), F32)],
        compiler_params=_params("arbitrary", "arbitrary"),
        name="pool_mixer",
    )(x, g, pool_w, scale)


def kernel(x, mem, mem_norm_g, mix_norm_g, xattn_norm_g, mlp_norm_g, ev_w_in, ev_q_norm_g, ev_k_norm_g, ev_conv_w, ev_conv_b, ev_gate_a_w, ev_gate_a_b, ev_gate_x_w, ev_gate_x_b, ev_lambda, ev_w_out, od_pool_w, od_scale, xa_w_q, xa_w_kv, xa_q_norm_g, xa_k_norm_g, xa_w_o, mlp_w1, mlp_w2):
    batch, seq, _ = x.shape
    rows = batch * seq

    def vec3(a):
        return a.reshape(a.shape[0], 1, a.shape[1])

    w_in, w_out = ev_w_in.astype(BF16), ev_w_out.astype(BF16)
    w_gates = jnp.concatenate([ev_gate_a_w, ev_gate_x_w], axis=-1).astype(BF16)
    pool_w = od_pool_w.astype(BF16)
    w_q, w_kv, w_o = xa_w_q.astype(BF16), xa_w_kv.astype(BF16), xa_w_o.astype(BF16)
    w1, w2 = mlp_w1.astype(BF16), mlp_w2.astype(BF16)
    mix_g, xa_g, mlp_g = vec3(mix_norm_g), vec3(xattn_norm_g), vec3(mlp_norm_g)
    ev_qg, ev_kg = vec3(ev_q_norm_g), vec3(ev_k_norm_g)
    xa_qg = vec3(xa_q_norm_g)
    conv_b, ga_b, gx_b, lam = vec3(ev_conv_b), vec3(ev_gate_a_b), vec3(ev_gate_x_b), vec3(ev_lambda)
    od_sc = vec3(od_scale)

    kt, v_mem = _memkv(mem, mem_norm_g, w_kv, xa_k_norm_g)

    for l in range(DEPTH):
        if l % 2 == 0:
            e = l // 2
            q, k, v, xr, gate = _inproj(x.reshape(rows, D_MODEL), l, e, mix_g, w_in, ev_qg, ev_kg)
            half = (batch, seq, MIX_HALF)
            o_a = _attn(q.reshape(half), k.reshape(half), v.reshape(half))
            x = _rglru_out(x, o_a, xr.reshape(half), gate.reshape(half), e, ev_conv_w, conv_b,
                           w_gates, ga_b, gx_b, lam, w_out)
        else:
            o = l // 2
            x = _pool(x, l, o, mix_g, pool_w, od_sc)
        x = _xattn(x, l, xa_g, w_q, xa_qg, kt, v_mem, w_o)
        x = _mlp(x.reshape(rows, D_MODEL), l, mlp_g, w1, w2).reshape(batch, seq, D_MODEL)
    return x
```

```python
import functools

import jax
import jax.numpy as jnp
from jax import lax
from jax.experimental import pallas as pl
from jax.experimental.pallas import tpu as pltpu

D_MODEL = 1024
DEPTH = 4
N_MEM = 256
MIX_HALF = D_MODEL // 2
A_HEADS = 4
A_HEAD_DIM = MIX_HALF // A_HEADS
A_BAND = 128
A_DILATIONS = (16, 4, 1)
A_CHUNKS_PER_STEP = 8
RG_WIDTH = MIX_HALF
RG_BLOCKS = 4
RG_BLOCK_DIM = RG_WIDTH // RG_BLOCKS
RG_C = 8.0
CONV_WIDTH = 4
POOL_WINDOWS = (2, 4, 8, 16)
POOL_GROUP_DIM = D_MODEL // len(POOL_WINDOWS)
POOL_HALO = 16
XA_HEADS = 4
XA_HEAD_DIM = D_MODEL // XA_HEADS
D_FF = 4 * D_MODEL
IN_COLS = 3 * MIX_HALF + 2 * RG_WIDTH
EPS = 1e-6
LOG2_E = 1.4426950408889634

SUBLANES = 8
VMEM_LIMIT = 56 * 1024 * 1024

ROW_TILE = 1024
MLP_TILE = 512
MLP_CHUNK = 1024
RG_TILE = 512
RG_SUB = 256
POOL_TILE = 512

BF16 = jnp.bfloat16
F32 = jnp.float32


def _rms(x, g):
    ms = jnp.mean(x * x, axis=-1, keepdims=True)
    return x * lax.rsqrt(ms + EPS) * g


def _dot(a, b):
    return jnp.dot(a, b, preferred_element_type=F32)


def _dot_nt(a, b):
    return lax.dot_general(a, b, (((1,), (1,)), ((), ())), preferred_element_type=F32)


def _resident(shape, index_map):
    return pl.BlockSpec(shape, index_map, pipeline_mode=pl.Buffered(1))


def _params(*sem):
    return pltpu.CompilerParams(dimension_semantics=sem, vmem_limit_bytes=VMEM_LIMIT)


def _memkv_kernel(mem_ref, mg_ref, wkv_ref, kg_ref, kt_ref, v_ref):
    batch = mem_ref.shape[0]
    mem = mem_ref[...].reshape(batch * N_MEM, D_MODEL)
    mem_n = _rms(mem, mg_ref[...]).astype(BF16)
    kv = _dot(mem_n, wkv_ref[...].astype(BF16))
    scale = XA_HEAD_DIM ** -0.5 * LOG2_E
    for h in range(XA_HEADS):
        sl = slice(h * XA_HEAD_DIM, (h + 1) * XA_HEAD_DIM)
        kh = _rms(kv[:, sl], kg_ref[...]) * scale
        for b in range(batch):
            kt_ref[b, sl, :] = kh[b * N_MEM:(b + 1) * N_MEM, :].T.astype(BF16)
    v_ref[...] = kv[:, D_MODEL:].astype(BF16).reshape(batch, N_MEM, D_MODEL)


def _memkv(mem, mem_norm_g, w_kv, k_g):
    batch = mem.shape[0]
    return pl.pallas_call(
        _memkv_kernel,
        grid=(DEPTH,),
        in_specs=[
            pl.BlockSpec((batch, N_MEM, D_MODEL), lambda l: (0, 0, 0)),
            pl.BlockSpec((1, D_MODEL), lambda l: (0, 0)),
            pl.BlockSpec((None, D_MODEL, 2 * D_MODEL), lambda l: (l, 0, 0)),
            pl.BlockSpec((None, 1, XA_HEAD_DIM), lambda l: (l, 0, 0)),
        ],
        out_specs=[
            pl.BlockSpec((None, batch, D_MODEL, N_MEM), lambda l: (l, 0, 0, 0)),
            pl.BlockSpec((None, batch, N_MEM, D_MODEL), lambda l: (l, 0, 0, 0)),
        ],
        out_shape=[
            jax.ShapeDtypeStruct((DEPTH, batch, D_MODEL, N_MEM), BF16),
            jax.ShapeDtypeStruct((DEPTH, batch, N_MEM, D_MODEL), BF16),
        ],
        compiler_params=_params("arbitrary"),
        name="memkv",
    )(mem, mem_norm_g.reshape(1, D_MODEL), w_kv, k_g.reshape(DEPTH, 1, XA_HEAD_DIM))


def _xattn_kernel(x_ref, g_ref, wq_ref, qg_ref, kt_ref, v_ref, wo_ref, o_ref):
    x = x_ref[...]
    h = _rms(x, g_ref[...]).astype(BF16)
    q = _dot(h, wq_ref[...].astype(BF16))
    outs = []
    for hd in range(XA_HEADS):
        sl = slice(hd * XA_HEAD_DIM, (hd + 1) * XA_HEAD_DIM)
        qh = _rms(q[:, sl], qg_ref[...]).astype(BF16)
        s = _dot(qh, kt_ref[sl, :])
        m = jnp.max(s, axis=-1, keepdims=True)
        p = jnp.exp2(s - m)
        den = jnp.sum(p, axis=-1, keepdims=True)
        oh = _dot(p.astype(BF16), v_ref[:, sl])
        outs.append((oh / den).astype(BF16))
    o = jnp.concatenate(outs, axis=-1)
    o_ref[...] = x + _dot(o, wo_ref[...].astype(BF16))


def _xattn(x, layer, g, w_q, q_g, kt, v, w_o):
    batch, seq, _ = x.shape
    return pl.pallas_call(
        _xattn_kernel,
        grid=(batch, seq // ROW_TILE),
        in_specs=[
            pl.BlockSpec((None, ROW_TILE, D_MODEL), lambda b, t: (b, t, 0)),
            pl.BlockSpec((None, 1, D_MODEL), lambda b, t: (layer, 0, 0)),
            _resident((None, D_MODEL, D_MODEL), lambda b, t: (layer, 0, 0)),
            pl.BlockSpec((None, 1, XA_HEAD_DIM), lambda b, t: (layer, 0, 0)),
            pl.BlockSpec((None, None, D_MODEL, N_MEM), lambda b, t: (layer, b, 0, 0)),
            pl.BlockSpec((None, None, N_MEM, D_MODEL), lambda b, t: (layer, b, 0, 0)),
            _resident((None, D_MODEL, D_MODEL), lambda b, t: (layer, 0, 0)),
        ],
        out_specs=pl.BlockSpec((None, ROW_TILE, D_MODEL), lambda b, t: (b, t, 0)),
        out_shape=jax.ShapeDtypeStruct(x.shape, F32),
        compiler_params=_params("parallel", "parallel"),
        name="xattn",
    )(x, g, w_q, q_g, kt, v, w_o)


def _mlp_kernel(x_ref, g_ref, w1_ref, w2_ref, o_ref):
    x = x_ref[...]
    h = _rms(x, g_ref[...]).astype(BF16)
    acc = x
    for c in range(D_FF // MLP_CHUNK):
        sl = slice(c * MLP_CHUNK, (c + 1) * MLP_CHUNK)
        a = jnp.maximum(_dot(h, w1_ref[:, sl].astype(BF16)), 0.0)
        acc = acc + _dot((a * a).astype(BF16), w2_ref[sl, :].astype(BF16))
    o_ref[...] = acc


def _mlp(x2, layer, g, w1, w2):
    rows = x2.shape[0]
    return pl.pallas_call(
        _mlp_kernel,
        grid=(rows // MLP_TILE,),
        in_specs=[
            pl.BlockSpec((MLP_TILE, D_MODEL), lambda i: (i, 0)),
            pl.BlockSpec((None, 1, D_MODEL), lambda i: (layer, 0, 0)),
            _resident((None, D_MODEL, D_FF), lambda i: (layer, 0, 0)),
            _resident((None, D_FF, D_MODEL), lambda i: (layer, 0, 0)),
        ],
        out_specs=pl.BlockSpec((MLP_TILE, D_MODEL), lambda i: (i, 0)),
        out_shape=jax.ShapeDtypeStruct(x2.shape, F32),
        compiler_params=_params("parallel"),
        name="mlp",
    )(x2, g, w1, w2)


def _inproj_kernel(x_ref, g_ref, w_ref, qg_ref, kg_ref, q_ref, k_ref, v_ref, xr_ref, gate_ref):
    h = _rms(x_ref[...], g_ref[...]).astype(BF16)
    scale = A_HEAD_DIM ** -0.5 * LOG2_E
    for j, (out_ref, gain_ref, mult) in enumerate(
            ((q_ref, qg_ref, scale), (k_ref, kg_ref, 1.0), (v_ref, None, None),
             (xr_ref, None, None), (gate_ref, None, None))):
        p = _dot(h, w_ref[:, j * MIX_HALF:(j + 1) * MIX_HALF].astype(BF16))
        if gain_ref is None:
            out_ref[...] = p
            continue
        for hd in range(A_HEADS):
            sl = slice(hd * A_HEAD_DIM, (hd + 1) * A_HEAD_DIM)
            out_ref[:, sl] = _rms(p[:, sl], gain_ref[...]) * mult


def _inproj(x2, layer, e, g, w_in, q_g, k_g):
    rows = x2.shape[0]
    half = jax.ShapeDtypeStruct((rows, MIX_HALF), F32)
    half_spec = pl.BlockSpec((ROW_TILE, MIX_HALF), lambda i: (i, 0))
    gain_spec = pl.BlockSpec((None, 1, A_HEAD_DIM), lambda i: (e, 0, 0))
    return pl.pallas_call(
        _inproj_kernel,
        grid=(rows // ROW_TILE,),
        in_specs=[
            pl.BlockSpec((ROW_TILE, D_MODEL), lambda i: (i, 0)),
            pl.BlockSpec((None, 1, D_MODEL), lambda i: (layer, 0, 0)),
            _resident((None, D_MODEL, IN_COLS), lambda i: (e, 0, 0)),
            gain_spec, gain_spec,
        ],
        out_specs=[half_spec] * 5,
        out_shape=[half] * 5,
        compiler_params=_params("parallel"),
        name="inproj",
    )(x2, g, w_in, q_g, k_g)


def _attn_kernel(q_ref, k_ref, v_ref, o_ref, m_ref, l_ref, acc_ref, *, seq):
    jq = lax.broadcasted_iota(jnp.int32, (A_BAND, A_BAND), 0)
    jk = lax.broadcasted_iota(jnp.int32, (A_BAND, A_BAND), 1)
    own_mask = jk <= jq
    jk2 = lax.broadcasted_iota(jnp.int32, (A_BAND, 2 * A_BAND), 1)
    jq2 = lax.broadcasted_iota(jnp.int32, (A_BAND, 2 * A_BAND), 0)
    both_mask = (jk2 >= jq2) & (jk2 <= jq2 + A_BAND)
    neg = -jnp.inf
    lanes = (A_BAND, A_HEAD_DIM)
    ones = jnp.ones(lanes, BF16)

    def rows(start, d):
        if d == 1:
            return pl.ds(pl.multiple_of(start, A_BAND), A_BAND)
        return pl.ds(start, A_BAND, stride=d)

    def load_kv(start, d):
        r = rows(start, d)
        return (k_ref[r, :].astype(BF16),
                jnp.concatenate([v_ref[r, :].astype(BF16), ones], axis=1))

    def process(jobs, d, mode):
        rs = [rows(start, d) for start, _, _ in jobs]
        qs = [q_ref[r, :].astype(BF16) for r in rs]
        scores = []
        for q, (_, own, prev) in zip(qs, jobs):
            if prev is None:
                s = jnp.where(own_mask, _dot_nt(q, own[0]), neg)
            else:
                s = _dot_nt(q, jnp.concatenate([prev[0], own[0]], axis=0))
                s = jnp.where(both_mask, s, neg)
            scores.append(s)
        ms = [jnp.max(s, axis=-1, keepdims=True) for s in scores]
        pvs = []
        for s, m, (_, own, prev) in zip(scores, ms, jobs):
            v = own[1] if prev is None else jnp.concatenate([prev[1], own[1]], axis=0)
            pvs.append(_dot(jnp.exp2(s - m).astype(BF16), v))
        for r, m, pv in zip(rs, ms, pvs):
            acc, den = pv[:, :A_HEAD_DIM], pv[:, A_HEAD_DIM:]
            m = jnp.broadcast_to(m, lanes)
            if mode != "init":
                m_old = m_ref[r, :]
                m_new = jnp.maximum(m_old, m)
                a_old = jnp.exp2(m_old - m_new)
                a_cur = jnp.exp2(m - m_new)
                den = a_old * l_ref[r, :] + a_cur * den
                acc = a_old * acc_ref[r, :] + a_cur * acc
                m = m_new
            if mode == "final":
                o_ref[r, :] = (acc / den).astype(o_ref.dtype)
            else:
                m_ref[r, :] = m
                l_ref[r, :] = den
                acc_ref[r, :] = acc

    def jobs_of(r, c0, n, d, first):
        pitch = A_BAND * d
        kv = [None if first else load_kv(r + (c0 - 1) * pitch, d)]
        kv += [load_kv(r + (c0 + j) * pitch, d) for j in range(n)]
        return [(r + (c0 + j) * pitch, kv[j + 1], kv[j]) for j in range(n)]

    def group(r, c0, n, d, mode, first):
        process(jobs_of(r, c0, n, d, first), d, mode)

    def run_pass(d, mode):
        n_chunks = seq // (A_BAND * d)
        n = min(n_chunks, A_CHUNKS_PER_STEP)
        if n_chunks == n:
            per_step = A_CHUNKS_PER_STEP // n

            def step(i, carry):
                jobs = []
                for j in range(per_step):
                    jobs += jobs_of(i * per_step + j, 0, n, d, True)
                process(jobs, d, mode)
                return carry

            lax.fori_loop(0, d // per_step, step, 0)
            return

        def residue(r, carry):
            group(r, 0, n, d, mode, True)

            def step(i, c):
                group(r, i * n, n, d, mode, False)
                return c

            return lax.fori_loop(1, n_chunks // n, step, carry)

        if d == 1:
            residue(0, 0)
        else:
            lax.fori_loop(0, d, residue, 0)

    modes = ("init",) + ("merge",) * (len(A_DILATIONS) - 2) + ("final",)
    for d, mode in zip(A_DILATIONS, modes):
        run_pass(d, mode)


def _attn(q, k, v):
    batch, seq, _ = q.shape
    assert seq % (A_BAND * max(A_DILATIONS)) == 0
    spec = pl.BlockSpec((None, seq, A_HEAD_DIM), lambda b, h: (b, 0, h))
    return pl.pallas_call(
        functools.partial(_attn_kernel, seq=seq),
        grid=(batch, A_HEADS),
        in_specs=[spec, spec, spec],
        out_specs=spec,
        out_shape=jax.ShapeDtypeStruct((batch, seq, MIX_HALF), BF16),
        scratch_shapes=[pltpu.VMEM((seq, A_HEAD_DIM), F32)] * 3,
        compiler_params=_params("parallel", "parallel"),
        name="dilated_attn",
    )(q, k, v)


def _gelu_tanh(x):
    c1 = -2.0 * 0.7978845608028654 * LOG2_E
    return x / (1.0 + jnp.exp2(x * (c1 + (c1 * 0.044715) * (x * x))))


def _rglru_out_kernel(x_ref, oa_ref, xr_ref, gate_ref, cw_ref, cb_ref, wg_ref, gab_ref, gxb_ref,
                      lam_ref, wout_ref, o_ref, xpad_ref, carry_ref):
    halo = SUBLANES
    group = (SUBLANES, RG_BLOCK_DIM)

    @pl.when(pl.program_id(1) == 0)
    def _():
        xpad_ref[0:halo, :] = jnp.zeros((halo, RG_WIDTH), F32)
        carry_ref[...] = jnp.zeros((SUBLANES, RG_WIDTH), F32)

    xpad_ref[halo:halo + RG_TILE, :] = xr_ref[...]
    w_out = wout_ref[...].astype(BF16)
    wg = [wg_ref[g].astype(BF16) for g in range(RG_BLOCKS)]
    lam = -lam_ref[...]
    softplus = jnp.maximum(lam, 0.0) + jnp.log1p(jnp.exp(-jnp.abs(lam)))
    grouped = (RG_SUB // SUBLANES, SUBLANES, RG_BLOCK_DIM)
    row = lax.broadcasted_iota(jnp.int32, grouped, 1)
    carry =[carry_ref[:, g * RG_BLOCK_DIM:(g + 1) * RG_BLOCK_DIM] for g in range(RG_BLOCKS)]

    for r0 in range(0, RG_TILE, RG_SUB):
        rs = slice(r0, r0 + RG_SUB)
        xc = cb_ref[...] + cw_ref[CONV_WIDTH - 1:CONV_WIDTH, :] * xr_ref[rs, :]
        for j in range(1, CONV_WIDTH):
            xc = xc + (cw_ref[CONV_WIDTH - 1 - j:CONV_WIDTH - j, :]
                       * xpad_ref[pl.ds(halo - j + r0, RG_SUB), :])
        mixed = [oa_ref[rs, :]]
        for g in range(RG_BLOCKS):
            sl = slice(g * RG_BLOCK_DIM, (g + 1) * RG_BLOCK_DIM)
            xg = xc[:, sl]
            rg = _dot(xg.astype(BF16), wg[g])
            r = jax.nn.sigmoid(rg[:, :RG_BLOCK_DIM] + gab_ref[:, sl])
            i = jax.nn.sigmoid(rg[:, RG_BLOCK_DIM:] + gxb_ref[:, sl])
            log_a = (-RG_C) * r * softplus[:, sl]
            a = jnp.exp(log_a)
            th = jnp.tanh(log_a)
            u = lax.rsqrt((th - 1.0) / (2.0 * th)) * (i * xg)
            a = a.reshape(grouped)
            u = u.reshape(grouped)
            k = 1
            while k < SUBLANES:
                valid = row >= k
                a_prev = pltpu.roll(a, k, 1)
                u_prev = pltpu.roll(u, k, 1)
                u = jnp.where(valid, a * u_prev + u, u)
                a = jnp.where(valid, a * a_prev, a)
                k *= 2
            a = a.reshape(RG_SUB, RG_BLOCK_DIM)
            u = u.reshape(RG_SUB, RG_BLOCK_DIM)
            hs = []
            c = carry[g]
            for gi in range(0, RG_SUB, SUBLANES):
                h = u[gi:gi + SUBLANES, :] + a[gi:gi + SUBLANES, :] * c
                hs.append(h)
                c = jnp.broadcast_to(h[SUBLANES - 1:SUBLANES, :], group)
            carry[g] = c
            mixed.append((jnp.concatenate(hs, axis=0) * _gelu_tanh(gate_ref[rs, sl])).astype(BF16))
        o_ref[rs, :] = x_ref[rs, :] + _dot(jnp.concatenate(mixed, axis=-1), w_out)

    for g in range(RG_BLOCKS):
        carry_ref[:, g * RG_BLOCK_DIM:(g + 1) * RG_BLOCK_DIM] = carry[g]
    xpad_ref[0:halo, :] = xr_ref[RG_TILE - halo:RG_TILE, :]


def _rglru_out(x, o_a, xr, gate, e, conv_w, conv_b, w_gates, ga_b, gx_b, lam, w_out):
    batch, seq, _ = x.shape
    full = pl.BlockSpec((None, RG_TILE, D_MODEL), lambda b, t: (b, t, 0))
    half = pl.BlockSpec((None, RG_TILE, RG_WIDTH), lambda b, t: (b, t, 0))
    vec = pl.BlockSpec((None, 1, RG_WIDTH), lambda b, t: (e, 0, 0))
    return pl.pallas_call(
        _rglru_out_kernel,
        grid=(batch, seq // RG_TILE),
        in_specs=[
            full, half, half, half,
            pl.BlockSpec((None, CONV_WIDTH, RG_WIDTH), lambda b, t: (e, 0, 0)),
            vec,
            pl.BlockSpec((None, RG_BLOCKS, RG_BLOCK_DIM, 2 * RG_BLOCK_DIM), lambda b, t: (e, 0, 0, 0)),
            vec, vec, vec,
            _resident((None, D_MODEL, D_MODEL), lambda b, t: (e, 0, 0)),
        ],
        out_specs=full,
        out_shape=jax.ShapeDtypeStruct(x.shape, F32),
        scratch_shapes=[
            pltpu.VMEM((RG_TILE + SUBLANES, RG_WIDTH), F32),
            pltpu.VMEM((SUBLANES, RG_WIDTH), F32),
        ],
        compiler_params=_params("arbitrary", "arbitrary"),
        name="rglru_outproj",
    )(x, o_a, xr, gate, conv_w, conv_b, w_gates, ga_b, gx_b, lam, w_out)


def _pool_kernel(x_ref, g_ref, pw_ref, sc_ref, o_ref, hpad_ref):
    t = pl.program_id(1)

    @pl.when(t == 0)
    def _():
        hpad_ref[0:POOL_HALO, :] = jnp.zeros((POOL_HALO, D_MODEL), F32)

    x = x_ref[...]
    h = _rms(x, g_ref[...])
    hpad_ref[POOL_HALO:POOL_HALO + POOL_TILE, :] = h
    pos = t * POOL_TILE + lax.broadcasted_iota(jnp.int32, (POOL_TILE, 1), 0)
    s = hpad_ref[...]
    outs = []
    width = 1
    for gi, w in enumerate(POOL_WINDOWS):
        while width < w:
            s = s + pltpu.roll(s, width, 0)
            width *= 2
        cnt = jnp.minimum(pos + 1, w).astype(F32)
        d = s[POOL_HALO:, :POOL_GROUP_DIM] / cnt - h[:, gi * POOL_GROUP_DIM:(gi + 1) * POOL_GROUP_DIM]
        outs.append(_dot(d.astype(BF16), pw_ref[gi].astype(BF16)))
        if gi + 1 < len(POOL_WINDOWS):
            s = s[:, POOL_GROUP_DIM:]
    o_ref[...] = x + jnp.concatenate(outs, axis=-1) * sc_ref[...]
    hpad_ref[0:POOL_HALO, :] = h[POOL_TILE - POOL_HALO:, :]


def _pool(x, layer, o, g, pool_w, scale):
    batch, seq, _ = x.shape
    tile = pl.BlockSpec((None, POOL_TILE, D_MODEL), lambda b, t: (b, t, 0))
    return pl.pallas_call(
        _pool_kernel,
        grid=(batch, seq // POOL_TILE),
        in_specs=[
            tile,
            pl.BlockSpec((None, 1, D_MODEL), lambda b, t: (layer, 0, 0)),
            pl.BlockSpec((None, len(POOL_WINDOWS), POOL_GROUP_DIM, POOL_GROUP_DIM),
                         lambda b, t: (o, 0, 0, 0)),
            pl.BlockSpec((None, 1, D_MODEL), lambda b, t: (o, 0, 0)),
        ],
        out_specs=tile,
        out_shape=jax.ShapeDtypeStruct(x.shape, F32),
        scratch_shapes=[pltpu.VMEM((POOL_HALO + POOL_TILE, D_MODEL), F32)],
        compiler_params=_params("arbitrary", "arbitrary"),
        name="pool_mixer",
    )(x, g, pool_w, scale)


def kernel(x, mem, mem_norm_g, mix_norm_g, xattn_norm_g, mlp_norm_g, ev_w_in, ev_q_norm_g, ev_k_norm_g, ev_conv_w, ev_conv_b, ev_gate_a_w, ev_gate_a_b, ev_gate_x_w, ev_gate_x_b, ev_lambda, ev_w_out, od_pool_w, od_scale, xa_w_q, xa_w_kv, xa_q_norm_g, xa_k_norm_g, xa_w_o, mlp_w1, mlp_w2):
    batch, seq, _ = x.shape
    rows = batch * seq

    def vec3(a):
        return a.reshape(a.shape[0], 1, a.shape[1])

    w_in, w_out = ev_w_in, ev_w_out
    w_gates = jnp.concatenate([ev_gate_a_w, ev_gate_x_w], axis=-1)
    pool_w = od_pool_w
    w_q, w_kv, w_o = xa_w_q, xa_w_kv, xa_w_o
    w1, w2 = mlp_w1, mlp_w2
    mix_g, xa_g, mlp_g = vec3(mix_norm_g), vec3(xattn_norm_g), vec3(mlp_norm_g)
    ev_qg, ev_kg = vec3(ev_q_norm_g), vec3(ev_k_norm_g)
    xa_qg = vec3(xa_q_norm_g)
    conv_b, ga_b, gx_b, lam = vec3(ev_conv_b), vec3(ev_gate_a_b), vec3(ev_gate_x_b), vec3(ev_lambda)
    od_sc = vec3(od_scale)

    kt, v_mem = _memkv(mem, mem_norm_g, w_kv, xa_k_norm_g)

    for l in range(DEPTH):
        if l % 2 == 0:
            e = l // 2
            q, k, v, xr, gate = _inproj(x.reshape(rows, D_MODEL), l, e, mix_g, w_in, ev_qg, ev_kg)
            half = (batch, seq, MIX_HALF)
            o_a = _attn(q.reshape(half), k.reshape(half), v.reshape(half))
            x = _rglru_out(x, o_a, xr.reshape(half), gate.reshape(half), e, ev_conv_w, conv_b,
                           w_gates, ga_b, gx_b, lam, w_out)
        else:
            o = l // 2
            x = _pool(x, l, o, mix_g, pool_w, od_sc)
        x = _xattn(x, l, xa_g, w_q, xa_qg, kt, v_mem, w_o)
        x = _mlp(x.reshape(rows, D_MODEL), l, mlp_g, w1, w2).reshape(batch, seq, D_MODEL)
    return x
```

```python
import functools

import jax
import jax.numpy as jnp
from jax import lax
from jax.experimental import pallas as pl
from jax.experimental.pallas import tpu as pltpu

D_MODEL = 1024
DEPTH = 4
N_MEM = 256
MIX_HALF = D_MODEL // 2
A_HEADS = 4
A_HEAD_DIM = MIX_HALF // A_HEADS
A_BAND = 128
A_DILATIONS = (16, 4, 1)
A_CHUNKS_PER_STEP = 8
RG_WIDTH = MIX_HALF
RG_BLOCKS = 4
RG_BLOCK_DIM = RG_WIDTH // RG_BLOCKS
RG_C = 8.0
CONV_WIDTH = 4
POOL_WINDOWS = (2, 4, 8, 16)
POOL_GROUP_DIM = D_MODEL // len(POOL_WINDOWS)
POOL_HALO = 16
XA_HEADS = 4
XA_HEAD_DIM = D_MODEL // XA_HEADS
D_FF = 4 * D_MODEL
IN_COLS = 3 * MIX_HALF + 2 * RG_WIDTH
EPS = 1e-6
LOG2_E = 1.4426950408889634

SUBLANES = 8
VMEM_LIMIT = 56 * 1024 * 1024

ROW_TILE = 1024
XA_SUB = 512
MLP_TILE = 512
MLP_CHUNK = 1024
RG_TILE = 512
RG_SUB = 256

BF16 = jnp.bfloat16
F32 = jnp.float32


def _rms(x, g):
    ms = jnp.mean(x * x, axis=-1, keepdims=True)
    return x * lax.rsqrt(ms + EPS) * g


def _dot(a, b):
    return jnp.dot(a, b, preferred_element_type=F32)


def _dot_nt(a, b):
    return lax.dot_general(a, b, (((1,), (1,)), ((), ())), preferred_element_type=F32)


def _resident(shape, index_map):
    return pl.BlockSpec(shape, index_map, pipeline_mode=pl.Buffered(1))


def _params(*sem):
    return pltpu.CompilerParams(dimension_semantics=sem, vmem_limit_bytes=VMEM_LIMIT)


def _memkv_kernel(mem_ref, mg_ref, wkv_ref, kg_ref, kt_ref, v_ref):
    batch = mem_ref.shape[0]
    mem = mem_ref[...].reshape(batch * N_MEM, D_MODEL)
    mem_n = _rms(mem, mg_ref[...]).astype(BF16)
    kv = _dot(mem_n, wkv_ref[...].astype(BF16))
    scale = XA_HEAD_DIM ** -0.5 * LOG2_E
    for h in range(XA_HEADS):
        sl = slice(h * XA_HEAD_DIM, (h + 1) * XA_HEAD_DIM)
        kh = _rms(kv[:, sl], kg_ref[...]) * scale
        for b in range(batch):
            kt_ref[b, sl, :] = kh[b * N_MEM:(b + 1) * N_MEM, :].T.astype(BF16)
    v_ref[...] = kv[:, D_MODEL:].astype(BF16).reshape(batch, N_MEM, D_MODEL)


def _memkv(mem, mem_norm_g, w_kv, k_g):
    batch = mem.shape[0]
    return pl.pallas_call(
        _memkv_kernel,
        grid=(DEPTH,),
        in_specs=[
            pl.BlockSpec((batch, N_MEM, D_MODEL), lambda l: (0, 0, 0)),
            pl.BlockSpec((1, D_MODEL), lambda l: (0, 0)),
            pl.BlockSpec((None, D_MODEL, 2 * D_MODEL), lambda l: (l, 0, 0)),
            pl.BlockSpec((None, 1, XA_HEAD_DIM), lambda l: (l, 0, 0)),
        ],
        out_specs=[
            pl.BlockSpec((None, batch, D_MODEL, N_MEM), lambda l: (l, 0, 0, 0)),
            pl.BlockSpec((None, batch, N_MEM, D_MODEL), lambda l: (l, 0, 0, 0)),
        ],
        out_shape=[
            jax.ShapeDtypeStruct((DEPTH, batch, D_MODEL, N_MEM), BF16),
            jax.ShapeDtypeStruct((DEPTH, batch, N_MEM, D_MODEL), BF16),
        ],
        compiler_params=_params("arbitrary"),
        name="memkv",
    )(mem, mem_norm_g.reshape(1, D_MODEL), w_kv, k_g.reshape(DEPTH, 1, XA_HEAD_DIM))


def _pool_rows(x, r0, pos0, mg_ref, pw_ref, sc_ref, hpad_ref):
    rows = x.shape[0]
    h = _rms(x, mg_ref[...])
    hpad_ref[POOL_HALO + r0:POOL_HALO + r0 + rows, :] = h
    pos = pos0 + lax.broadcasted_iota(jnp.int32, (rows, 1), 0)
    s = hpad_ref[r0:r0 + POOL_HALO + rows, :]
    outs = []
    width = 1
    for gi, w in enumerate(POOL_WINDOWS):
        while width < w:
            s = s + pltpu.roll(s, width, 0)
            width *= 2
        cnt = jnp.minimum(pos + 1, w).astype(F32)
        d = s[POOL_HALO:, :POOL_GROUP_DIM] / cnt - h[:, gi * POOL_GROUP_DIM:(gi + 1) * POOL_GROUP_DIM]
        outs.append(_dot(d.astype(BF16), pw_ref[gi].astype(BF16)))
        if gi + 1 < len(POOL_WINDOWS):
            s = s[:, POOL_GROUP_DIM:]
    return x + jnp.concatenate(outs, axis=-1) * sc_ref[...]


def _xattn_kernel(*refs, pool):
    if pool:
        x_ref, mg_ref, pw_ref, sc_ref, g_ref, wq_ref, qg_ref, kt_ref, v_ref, wo_ref, o_ref, hpad_ref = refs
        t = pl.program_id(1)

        @pl.when(t == 0)
        def _():
            hpad_ref[0:POOL_HALO, :] = jnp.zeros((POOL_HALO, D_MODEL), F32)
    else:
        x_ref, g_ref, wq_ref, qg_ref, kt_ref, v_ref, wo_ref, o_ref = refs
    w_q = wq_ref[...].astype(BF16)
    w_o = wo_ref[...].astype(BF16)
    heads = [slice(hd * XA_HEAD_DIM, (hd + 1) * XA_HEAD_DIM) for hd in range(XA_HEADS)]
    for r0 in range(0, ROW_TILE, XA_SUB):
        rs = slice(r0, r0 + XA_SUB)
        x = x_ref[rs, :]
        if pool:
            x = _pool_rows(x, r0, t * ROW_TILE + r0, mg_ref, pw_ref, sc_ref, hpad_ref)
        h = _rms(x, g_ref[...]).astype(BF16)
        qs = [_dot(h, w_q[:, sl]) for sl in heads]
        qs = [_rms(q, qg_ref[...]).astype(BF16) for q in qs]
        ss = [_dot(q, kt_ref[sl, :]) for q, sl in zip(qs, heads)]
        ps = [jnp.exp2(s - jnp.max(s, axis=-1, keepdims=True)) for s in ss]
        dens = [jnp.sum(p, axis=-1, keepdims=True) for p in ps]
        os_ = [_dot(p.astype(BF16), v_ref[:, sl]) for p, sl in zip(ps, heads)]
        os_ = [(o / den).astype(BF16) for o, den in zip(os_, dens)]
        out = x
        for o, sl in zip(os_, heads):
            out = out + _dot(o, w_o[sl, :])
        o_ref[rs, :] = out
    if pool:
        hpad_ref[0:POOL_HALO, :] = hpad_ref[ROW_TILE:ROW_TILE + POOL_HALO, :]


def _xattn(x, layer, g, w_q, q_g, kt, v, w_o, pool=None):
    batch, seq, _ = x.shape
    tile = pl.BlockSpec((None, ROW_TILE, D_MODEL), lambda b, t: (b, t, 0))
    in_specs = [
        pl.BlockSpec((None, 1, D_MODEL), lambda b, t: (layer, 0, 0)),
        _resident((None, D_MODEL, D_MODEL), lambda b, t: (layer, 0, 0)),
        pl.BlockSpec((None, 1, XA_HEAD_DIM), lambda b, t: (layer, 0, 0)),
        pl.BlockSpec((None, None, D_MODEL, N_MEM), lambda b, t: (layer, b, 0, 0)),
        pl.BlockSpec((None, None, N_MEM, D_MODEL), lambda b, t: (layer, b, 0, 0)),
        _resident((None, D_MODEL, D_MODEL), lambda b, t: (layer, 0, 0)),
    ]
    args = (g, w_q, q_g, kt, v, w_o)
    scratch = []
    if pool is not None:
        o = layer // 2
        in_specs = [
            pl.BlockSpec((None, 1, D_MODEL), lambda b, t: (layer, 0, 0)),
            pl.BlockSpec((None, len(POOL_WINDOWS), POOL_GROUP_DIM, POOL_GROUP_DIM),
                         lambda b, t: (o, 0, 0, 0)),
            pl.BlockSpec((None, 1, D_MODEL), lambda b, t: (o, 0, 0)),
        ] + in_specs
        args = tuple(pool) + args
        scratch = [pltpu.VMEM((POOL_HALO + ROW_TILE, D_MODEL), F32)]
    return pl.pallas_call(
        functools.partial(_xattn_kernel, pool=pool is not None),
        grid=(batch, seq // ROW_TILE),
        in_specs=[tile] + in_specs,
        out_specs=tile,
        out_shape=jax.ShapeDtypeStruct(x.shape, F32),
        scratch_shapes=scratch,
        compiler_params=_params("parallel", "arbitrary" if pool is not None else "parallel"),
        name="xattn" if pool is None else "pool_xattn",
    )(x, *args)


def _mlp_kernel(x_ref, g_ref, w1_ref, w2_ref, o_ref):
    x = x_ref[...]
    h = _rms(x, g_ref[...]).astype(BF16)
    acc = x
    for c in range(D_FF // MLP_CHUNK):
        sl = slice(c * MLP_CHUNK, (c + 1) * MLP_CHUNK)
        a = jnp.maximum(_dot(h, w1_ref[:, sl].astype(BF16)), 0.0)
        acc = acc + _dot((a * a).astype(BF16), w2_ref[sl, :].astype(BF16))
    o_ref[...] = acc


def _mlp(x2, layer, g, w1, w2):
    rows = x2.shape[0]
    return pl.pallas_call(
        _mlp_kernel,
        grid=(rows // MLP_TILE,),
        in_specs=[
            pl.BlockSpec((MLP_TILE, D_MODEL), lambda i: (i, 0)),
            pl.BlockSpec((None, 1, D_MODEL), lambda i: (layer, 0, 0)),
            _resident((None, D_MODEL, D_FF), lambda i: (layer, 0, 0)),
            _resident((None, D_FF, D_MODEL), lambda i: (layer, 0, 0)),
        ],
        out_specs=pl.BlockSpec((MLP_TILE, D_MODEL), lambda i: (i, 0)),
        out_shape=jax.ShapeDtypeStruct(x2.shape, F32),
        compiler_params=_params("parallel"),
        name="mlp",
    )(x2, g, w1, w2)


def _inproj_kernel(x_ref, g_ref, w_ref, qg_ref, kg_ref, q_ref, k_ref, v_ref, xr_ref, gate_ref):
    h = _rms(x_ref[...], g_ref[...]).astype(BF16)
    scale = A_HEAD_DIM ** -0.5 * LOG2_E
    for j, (out_ref, gain_ref, mult) in enumerate(
            ((q_ref, qg_ref, scale), (k_ref, kg_ref, 1.0), (v_ref, None, None),
             (xr_ref, None, None), (gate_ref, None, None))):
        p = _dot(h, w_ref[:, j * MIX_HALF:(j + 1) * MIX_HALF].astype(BF16))
        if gain_ref is None:
            out_ref[...] = p
            continue
        for hd in range(A_HEADS):
            sl = slice(hd * A_HEAD_DIM, (hd + 1) * A_HEAD_DIM)
            out_ref[:, sl] = _rms(p[:, sl], gain_ref[...]) * mult


def _inproj(x2, layer, e, g, w_in, q_g, k_g):
    rows = x2.shape[0]
    half = jax.ShapeDtypeStruct((rows, MIX_HALF), F32)
    half_spec = pl.BlockSpec((ROW_TILE, MIX_HALF), lambda i: (i, 0))
    gain_spec = pl.BlockSpec((None, 1, A_HEAD_DIM), lambda i: (e, 0, 0))
    return pl.pallas_call(
        _inproj_kernel,
        grid=(rows // ROW_TILE,),
        in_specs=[
            pl.BlockSpec((ROW_TILE, D_MODEL), lambda i: (i, 0)),
            pl.BlockSpec((None, 1, D_MODEL), lambda i: (layer, 0, 0)),
            _resident((None, D_MODEL, IN_COLS), lambda i: (e, 0, 0)),
            gain_spec, gain_spec,
        ],
        out_specs=[half_spec] * 5,
        out_shape=[half] * 5,
        compiler_params=_params("parallel"),
        name="inproj",
    )(x2, g, w_in, q_g, k_g)


def _attn_kernel(q_ref, k_ref, v_ref, o_ref, m_ref, l_ref, acc_ref, *, seq):
    jq = lax.broadcasted_iota(jnp.int32, (A_BAND, A_BAND), 0)
    jk = lax.broadcasted_iota(jnp.int32, (A_BAND, A_BAND), 1)
    own_mask = jk <= jq
    jk2 = lax.broadcasted_iota(jnp.int32, (A_BAND, 2 * A_BAND), 1)
    jq2 = lax.broadcasted_iota(jnp.int32, (A_BAND, 2 * A_BAND), 0)
    both_mask = (jk2 >= jq2) & (jk2 <= jq2 + A_BAND)
    neg = -jnp.inf
    lanes = (A_BAND, A_HEAD_DIM)
    ones = jnp.ones(lanes, BF16)

    def rows(start, d):
        if d == 1:
            return pl.ds(pl.multiple_of(start, A_BAND), A_BAND)
        return pl.ds(start, A_BAND, stride=d)

    def load_kv(start, d):
        r = rows(start, d)
        return (k_ref[r, :].astype(BF16),
                jnp.concatenate([v_ref[r, :].astype(BF16), ones], axis=1))

    def process(jobs, d, mode):
        rs = [rows(start, d) for start, _, _ in jobs]
        qs = [q_ref[r, :].astype(BF16) for r in rs]
        scores = []
        for q, (_, own, prev) in zip(qs, jobs):
            if prev is None:
                s = jnp.where(own_mask, _dot_nt(q, own[0]), neg)
            else:
                s = _dot_nt(q, jnp.concatenate([prev[0], own[0]], axis=0))
                s = jnp.where(both_mask, s, neg)
            scores.append(s)
        ms = [jnp.max(s, axis=-1, keepdims=True) for s in scores]
        pvs = []
        for s, m, (_, own, prev) in zip(scores, ms, jobs):
            v = own[1] if prev is None else jnp.concatenate([prev[1], own[1]], axis=0)
            pvs.append(_dot(jnp.exp2(s - m).astype(BF16), v))
        for r, m, pv in zip(rs, ms, pvs):
            acc, den = pv[:, :A_HEAD_DIM], pv[:, A_HEAD_DIM:]
            m = jnp.broadcast_to(m, lanes)
            if mode != "init":
                m_old = m_ref[r, :]
                m_new = jnp.maximum(m_old, m)
                a_old = jnp.exp2(m_old - m_new)
                a_cur = jnp.exp2(m - m_new)
                den = a_old * l_ref[r, :] + a_cur * den
                acc = a_old * acc_ref[r, :] + a_cur * acc
                m = m_new
            if mode == "final":
                o_ref[r, :] = (acc / den).astype(o_ref.dtype)
            else:
                m_ref[r, :] = m
                l_ref[r, :] = den
                acc_ref[r, :] = acc

    def jobs_of(r, c0, n, d, first):
        pitch = A_BAND * d
        kv = [None if first else load_kv(r + (c0 - 1) * pitch, d)]
        kv += [load_kv(r + (c0 + j) * pitch, d) for j in range(n)]
        return [(r + (c0 + j) * pitch, kv[j + 1], kv[j]) for j in range(n)]

    def group(r, c0, n, d, mode, first):
        process(jobs_of(r, c0, n, d, first), d, mode)

    def run_pass(d, mode):
        n_chunks = seq // (A_BAND * d)
        n = min(n_chunks, A_CHUNKS_PER_STEP)
        if n_chunks == n:
            per_step = A_CHUNKS_PER_STEP // n

            def step(i, carry):
                jobs = []
                for j in range(per_step):
                    jobs += jobs_of(i * per_step + j, 0, n, d, True)
                process(jobs, d, mode)
                return carry

            lax.fori_loop(0, d // per_step, step, 0)
            return

        def residue(r, carry):
            group(r, 0, n, d, mode, True)

            def step(i, c):
                group(r, i * n, n, d, mode, False)
                return c

            return lax.fori_loop(1, n_chunks // n, step, carry)

        if d == 1:
            residue(0, 0)
        else:
            lax.fori_loop(0, d, residue, 0)

    modes = ("init",) + ("merge",) * (len(A_DILATIONS) - 2) + ("final",)
    for d, mode in zip(A_DILATIONS, modes):
        run_pass(d, mode)


def _attn(q, k, v):
    batch, seq, _ = q.shape
    assert seq % (A_BAND * max(A_DILATIONS)) == 0
    spec = pl.BlockSpec((None, seq, A_HEAD_DIM), lambda b, h: (b, 0, h))
    return pl.pallas_call(
        functools.partial(_attn_kernel, seq=seq),
        grid=(batch, A_HEADS),
        in_specs=[spec, spec, spec],
        out_specs=spec,
        out_shape=jax.ShapeDtypeStruct((batch, seq, MIX_HALF), BF16),
        scratch_shapes=[pltpu.VMEM((seq, A_HEAD_DIM), F32)] * 3,
        compiler_params=_params("parallel", "parallel"),
        name="dilated_attn",
    )(q, k, v)


def _gelu_tanh(x):
    c1 = -2.0 * 0.7978845608028654 * LOG2_E
    return x / (1.0 + jnp.exp2(x * (c1 + (c1 * 0.044715) * (x * x))))


def _rglru_out_kernel(x_ref, oa_ref, xr_ref, gate_ref, cw_ref, cb_ref, wg_ref, gab_ref, gxb_ref,
                      lam_ref, wout_ref, o_ref, xpad_ref, carry_ref):
    halo = SUBLANES
    group = (SUBLANES, RG_BLOCK_DIM)

    @pl.when(pl.program_id(1) == 0)
    def _():
        xpad_ref[0:halo, :] = jnp.zeros((halo, RG_WIDTH), F32)
        carry_ref[...] = jnp.zeros((SUBLANES, RG_WIDTH), F32)

    xpad_ref[halo:halo + RG_TILE, :] = xr_ref[...]
    w_out = wout_ref[...].astype(BF16)
    wg = [wg_ref[g].astype(BF16) for g in range(RG_BLOCKS)]
    lam = -lam_ref[...]
    softplus = jnp.maximum(lam, 0.0) + jnp.log1p(jnp.exp(-jnp.abs(lam)))
    grouped = (RG_SUB // SUBLANES, SUBLANES, RG_BLOCK_DIM)
    row = lax.broadcasted_iota(jnp.int32, grouped, 1)
    carry =[carry_ref[:, g * RG_BLOCK_DIM:(g + 1) * RG_BLOCK_DIM] for g in range(RG_BLOCKS)]

    for r0 in range(0, RG_TILE, RG_SUB):
        rs = slice(r0, r0 + RG_SUB)
        xc = cb_ref[...] + cw_ref[CONV_WIDTH - 1:CONV_WIDTH, :] * xr_ref[rs, :]
        for j in range(1, CONV_WIDTH):
            xc = xc + (cw_ref[CONV_WIDTH - 1 - j:CONV_WIDTH - j, :]
                       * xpad_ref[pl.ds(halo - j + r0, RG_SUB), :])
        mixed = [oa_ref[rs, :]]
        for g in range(RG_BLOCKS):
            sl = slice(g * RG_BLOCK_DIM, (g + 1) * RG_BLOCK_DIM)
            xg = xc[:, sl]
            rg = _dot(xg.astype(BF16), wg[g])
            r = jax.nn.sigmoid(rg[:, :RG_BLOCK_DIM] + gab_ref[:, sl])
            i = jax.nn.sigmoid(rg[:, RG_BLOCK_DIM:] + gxb_ref[:, sl])
            log_a = (-RG_C) * r * softplus[:, sl]
            a = jnp.exp(log_a)
            th = jnp.tanh(log_a)
            u = lax.rsqrt((th - 1.0) / (2.0 * th)) * (i * xg)
            a = a.reshape(grouped)
            u = u.reshape(grouped)
            k = 1
            while k < SUBLANES:
                valid = row >= k
                a_prev = pltpu.roll(a, k, 1)
                u_prev = pltpu.roll(u, k, 1)
                u = jnp.where(valid, a * u_prev + u, u)
                a = jnp.where(valid, a * a_prev, a)
                k *= 2
            a = a.reshape(RG_SUB, RG_BLOCK_DIM)
            u = u.reshape(RG_SUB, RG_BLOCK_DIM)
            hs = []
            c = carry[g]
            for gi in range(0, RG_SUB, SUBLANES):
                h = u[gi:gi + SUBLANES, :] + a[gi:gi + SUBLANES, :] * c
                hs.append(h)
                c = jnp.broadcast_to(h[SUBLANES - 1:SUBLANES, :], group)
            carry[g] = c
            mixed.append((jnp.concatenate(hs, axis=0) * _gelu_tanh(gate_ref[rs, sl])).astype(BF16))
        o_ref[rs, :] = x_ref[rs, :] + _dot(jnp.concatenate(mixed, axis=-1), w_out)

    for g in range(RG_BLOCKS):
        carry_ref[:, g * RG_BLOCK_DIM:(g + 1) * RG_BLOCK_DIM] = carry[g]
    xpad_ref[0:halo, :] = xr_ref[RG_TILE - halo:RG_TILE, :]


def _rglru_out(x, o_a, xr, gate, e, conv_w, conv_b, w_gates, ga_b, gx_b, lam, w_out):
    batch, seq, _ = x.shape
    full = pl.BlockSpec((None, RG_TILE, D_MODEL), lambda b, t: (b, t, 0))
    half = pl.BlockSpec((None, RG_TILE, RG_WIDTH), lambda b, t: (b, t, 0))
    vec = pl.BlockSpec((None, 1, RG_WIDTH), lambda b, t: (e, 0, 0))
    return pl.pallas_call(
        _rglru_out_kernel,
        grid=(batch, seq // RG_TILE),
        in_specs=[
            full, half, half, half,
            pl.BlockSpec((None, CONV_WIDTH, RG_WIDTH), lambda b, t: (e, 0, 0)),
            vec,
            pl.BlockSpec((None, RG_BLOCKS, RG_BLOCK_DIM, 2 * RG_BLOCK_DIM), lambda b, t: (e, 0, 0, 0)),
            vec, vec, vec,
            _resident((None, D_MODEL, D_MODEL), lambda b, t: (e, 0, 0)),
        ],
        out_specs=full,
        out_shape=jax.ShapeDtypeStruct(x.shape, F32),
        scratch_shapes=[
            pltpu.VMEM((RG_TILE + SUBLANES, RG_WIDTH), F32),
            pltpu.VMEM((SUBLANES, RG_WIDTH), F32),
        ],
        compiler_params=_params("arbitrary", "arbitrary"),
        name="rglru_outproj",
    )(x, o_a, xr, gate, conv_w, conv_b, w_gates, ga_b, gx_b, lam, w_out)


def kernel(x, mem, mem_norm_g, mix_norm_g, xattn_norm_g, mlp_norm_g, ev_w_in, ev_q_norm_g, ev_k_norm_g, ev_conv_w, ev_conv_b, ev_gate_a_w, ev_gate_a_b, ev_gate_x_w, ev_gate_x_b, ev_lambda, ev_w_out, od_pool_w, od_scale, xa_w_q, xa_w_kv, xa_q_norm_g, xa_k_norm_g, xa_w_o, mlp_w1, mlp_w2):
    batch, seq, _ = x.shape
    rows = batch * seq

    def vec3(a):
        return a.reshape(a.shape[0], 1, a.shape[1])

    w_in, w_out = ev_w_in, ev_w_out
    w_gates = jnp.concatenate([ev_gate_a_w, ev_gate_x_w], axis=-1)
    pool_w = od_pool_w
    w_q, w_kv, w_o = xa_w_q, xa_w_kv, xa_w_o
    w1, w2 = mlp_w1, mlp_w2
    mix_g, xa_g, mlp_g = vec3(mix_norm_g), vec3(xattn_norm_g), vec3(mlp_norm_g)
    ev_qg, ev_kg = vec3(ev_q_norm_g), vec3(ev_k_norm_g)
    xa_qg = vec3(xa_q_norm_g)
    conv_b, ga_b, gx_b, lam = vec3(ev_conv_b), vec3(ev_gate_a_b), vec3(ev_gate_x_b), vec3(ev_lambda)
    od_sc = vec3(od_scale)

    kt, v_mem = _memkv(mem, mem_norm_g, w_kv, xa_k_norm_g)

    for l in range(DEPTH):
        if l % 2 == 0:
            e = l // 2
            q, k, v, xr, gate = _inproj(x.reshape(rows, D_MODEL), l, e, mix_g, w_in, ev_qg, ev_kg)
            half = (batch, seq, MIX_HALF)
            o_a = _attn(q.reshape(half), k.reshape(half), v.reshape(half))
            x = _rglru_out(x, o_a, xr.reshape(half), gate.reshape(half), e, ev_conv_w, conv_b,
                           w_gates, ga_b, gx_b, lam, w_out)
            pool = None
        else:
            pool = (mix_g, pool_w, od_sc)
        x = _xattn(x, l, xa_g, w_q, xa_qg, kt, v_mem, w_o, pool)
        x = _mlp(x.reshape(rows, D_MODEL), l, mlp_g, w1, w2).reshape(batch, seq, D_MODEL)
    return x
```

```python
import functools

import jax
import jax.numpy as jnp
from jax import lax
from jax.experimental import pallas as pl
from jax.experimental.pallas import tpu as pltpu

D_MODEL = 1024
DEPTH = 4
N_MEM = 256
MIX_HALF = D_MODEL // 2
A_HEADS = 4
A_HEAD_DIM = MIX_HALF // A_HEADS
A_BAND = 128
A_DILATIONS = (16, 4, 1)
A_CHUNKS_PER_STEP = 8
RG_WIDTH = MIX_HALF
RG_BLOCKS = 4
RG_BLOCK_DIM = RG_WIDTH // RG_BLOCKS
RG_C = 8.0
CONV_WIDTH = 4
POOL_WINDOWS = (2, 4, 8, 16)
POOL_GROUP_DIM = D_MODEL // len(POOL_WINDOWS)
POOL_HALO = 16
XA_HEADS = 4
XA_HEAD_DIM = D_MODEL // XA_HEADS
D_FF = 4 * D_MODEL
IN_COLS = 3 * MIX_HALF + 2 * RG_WIDTH
EPS = 1e-6
LOG2_E = 1.4426950408889634

SUBLANES = 8
VMEM_LIMIT = 56 * 1024 * 1024

ROW_TILE = 1024
XA_SUB = 512
MLP_TILE = 512
MLP_CHUNK = 1024
RG_TILE = 512
RG_SUB = 256

BF16 = jnp.bfloat16
F32 = jnp.float32


def _rms(x, g):
    ms = jnp.mean(x * x, axis=-1, keepdims=True)
    return x * lax.rsqrt(ms + EPS) * g


def _dot(a, b):
    return jnp.dot(a, b, preferred_element_type=F32)


def _dot_nt(a, b):
    return lax.dot_general(a, b, (((1,), (1,)), ((), ())), preferred_element_type=F32)


def _resident(shape, index_map):
    return pl.BlockSpec(shape, index_map, pipeline_mode=pl.Buffered(1))


def _params(*sem):
    return pltpu.CompilerParams(dimension_semantics=sem, vmem_limit_bytes=VMEM_LIMIT)


def _memkv_kernel(mem_ref, mg_ref, wkv_ref, kg_ref, kt_ref, v_ref):
    batch = mem_ref.shape[0]
    mem = mem_ref[...].reshape(batch * N_MEM, D_MODEL)
    mem_n = _rms(mem, mg_ref[...]).astype(BF16)
    kv = _dot(mem_n, wkv_ref[...].astype(BF16))
    scale = XA_HEAD_DIM ** -0.5 * LOG2_E
    for h in range(XA_HEADS):
        sl = slice(h * XA_HEAD_DIM, (h + 1) * XA_HEAD_DIM)
        kh = _rms(kv[:, sl], kg_ref[...]) * scale
        for b in range(batch):
            kt_ref[b, sl, :] = kh[b * N_MEM:(b + 1) * N_MEM, :].T.astype(BF16)
    v_ref[...] = kv[:, D_MODEL:].astype(BF16).reshape(batch, N_MEM, D_MODEL)


def _memkv(mem, mem_norm_g, w_kv, k_g):
    batch = mem.shape[0]
    return pl.pallas_call(
        _memkv_kernel,
        grid=(DEPTH,),
        in_specs=[
            pl.BlockSpec((batch, N_MEM, D_MODEL), lambda l: (0, 0, 0)),
            pl.BlockSpec((1, D_MODEL), lambda l: (0, 0)),
            pl.BlockSpec((None, D_MODEL, 2 * D_MODEL), lambda l: (l, 0, 0)),
            pl.BlockSpec((None, 1, XA_HEAD_DIM), lambda l: (l, 0, 0)),
        ],
        out_specs=[
            pl.BlockSpec((None, batch, D_MODEL, N_MEM), lambda l: (l, 0, 0, 0)),
            pl.BlockSpec((None, batch, N_MEM, D_MODEL), lambda l: (l, 0, 0, 0)),
        ],
        out_shape=[
            jax.ShapeDtypeStruct((DEPTH, batch, D_MODEL, N_MEM), BF16),
            jax.ShapeDtypeStruct((DEPTH, batch, N_MEM, D_MODEL), BF16),
        ],
        compiler_params=_params("arbitrary"),
        name="memkv",
    )(mem, mem_norm_g.reshape(1, D_MODEL), w_kv, k_g.reshape(DEPTH, 1, XA_HEAD_DIM))


def _pool_rows(x, r0, pos0, mg_ref, pw_ref, sc_ref, hpad_ref):
    rows = x.shape[0]
    h = _rms(x, mg_ref[...])
    hpad_ref[POOL_HALO + r0:POOL_HALO + r0 + rows, :] = h
    pos = pos0 + lax.broadcasted_iota(jnp.int32, (rows, 1), 0)
    s = hpad_ref[r0:r0 + POOL_HALO + rows, :]
    outs = []
    width = 1
    for gi, w in enumerate(POOL_WINDOWS):
        while width < w:
            s = s + pltpu.roll(s, width, 0)
            width *= 2
        cnt = jnp.minimum(pos + 1, w).astype(F32)
        d = s[POOL_HALO:, :POOL_GROUP_DIM] / cnt - h[:, gi * POOL_GROUP_DIM:(gi + 1) * POOL_GROUP_DIM]
        outs.append(_dot(d.astype(BF16), pw_ref[gi].astype(BF16)))
        if gi + 1 < len(POOL_WINDOWS):
            s = s[:, POOL_GROUP_DIM:]
    return x + jnp.concatenate(outs, axis=-1) * sc_ref[...]


def _xattn_kernel(*refs, pool):
    if pool:
        x_ref, mg_ref, pw_ref, sc_ref, g_ref, wq_ref, qg_ref, kt_ref, v_ref, wo_ref, o_ref, hpad_ref = refs
        t = pl.program_id(1)

        @pl.when(t == 0)
        def _():
            hpad_ref[0:POOL_HALO, :] = jnp.zeros((POOL_HALO, D_MODEL), F32)
    else:
        x_ref, g_ref, wq_ref, qg_ref, kt_ref, v_ref, wo_ref, o_ref = refs
    w_q = wq_ref[...].astype(BF16)
    w_o = wo_ref[...].astype(BF16)
    heads = [slice(hd * XA_HEAD_DIM, (hd + 1) * XA_HEAD_DIM) for hd in range(XA_HEADS)]
    for r0 in range(0, ROW_TILE, XA_SUB):
        rs = slice(r0, r0 + XA_SUB)
        x = x_ref[rs, :]
        if pool:
            x = _pool_rows(x, r0, t * ROW_TILE + r0, mg_ref, pw_ref, sc_ref, hpad_ref)
        h = _rms(x, g_ref[...]).astype(BF16)
        qs = [_dot(h, w_q[:, sl]) for sl in heads]
        qs = [_rms(q, qg_ref[...]).astype(BF16) for q in qs]
        ss = [_dot(q, kt_ref[sl, :]) for q, sl in zip(qs, heads)]
        ps = [jnp.exp2(s - jnp.max(s, axis=-1, keepdims=True)) for s in ss]
        dens = [jnp.sum(p, axis=-1, keepdims=True) for p in ps]
        os_ = [_dot(p.astype(BF16), v_ref[:, sl]) for p, sl in zip(ps, heads)]
        os_ = [(o / den).astype(BF16) for o, den in zip(os_, dens)]
        out = x
        for o, sl in zip(os_, heads):
            out = out + _dot(o, w_o[sl, :])
        o_ref[rs, :] = out
    if pool:
        hpad_ref[0:POOL_HALO, :] = hpad_ref[ROW_TILE:ROW_TILE + POOL_HALO, :]


def _xattn(x, layer, g, w_q, q_g, kt, v, w_o, pool=None):
    batch, seq, _ = x.shape
    tile = pl.BlockSpec((None, ROW_TILE, D_MODEL), lambda b, t: (b, t, 0))
    in_specs = [
        pl.BlockSpec((None, 1, D_MODEL), lambda b, t: (layer, 0, 0)),
        _resident((None, D_MODEL, D_MODEL), lambda b, t: (layer, 0, 0)),
        pl.BlockSpec((None, 1, XA_HEAD_DIM), lambda b, t: (layer, 0, 0)),
        pl.BlockSpec((None, None, D_MODEL, N_MEM), lambda b, t: (layer, b, 0, 0)),
        pl.BlockSpec((None, None, N_MEM, D_MODEL), lambda b, t: (layer, b, 0, 0)),
        _resident((None, D_MODEL, D_MODEL), lambda b, t: (layer, 0, 0)),
    ]
    args = (g, w_q, q_g, kt, v, w_o)
    scratch = []
    if pool is not None:
        o = layer // 2
        in_specs = [
            pl.BlockSpec((None, 1, D_MODEL), lambda b, t: (layer, 0, 0)),
            pl.BlockSpec((None, len(POOL_WINDOWS), POOL_GROUP_DIM, POOL_GROUP_DIM),
                         lambda b, t: (o, 0, 0, 0)),
            pl.BlockSpec((None, 1, D_MODEL), lambda b, t: (o, 0, 0)),
        ] + in_specs
        args = tuple(pool) + args
        scratch = [pltpu.VMEM((POOL_HALO + ROW_TILE, D_MODEL), F32)]
    return pl.pallas_call(
        functools.partial(_xattn_kernel, pool=pool is not None),
        grid=(batch, seq // ROW_TILE),
        in_specs=[tile] + in_specs,
        out_specs=tile,
        out_shape=jax.ShapeDtypeStruct(x.shape, F32),
        scratch_shapes=scratch,
        compiler_params=_params("parallel", "arbitrary" if pool is not None else "parallel"),
        name="xattn" if pool is None else "pool_xattn",
    )(x, *args)


def _mlp_kernel(x_ref, g_ref, w1_ref, w2_ref, o_ref):
    x = x_ref[...]
    h = _rms(x, g_ref[...]).astype(BF16)
    acc = x
    for c in range(D_FF // MLP_CHUNK):
        sl = slice(c * MLP_CHUNK, (c + 1) * MLP_CHUNK)
        a = jnp.maximum(_dot(h, w1_ref[:, sl].astype(BF16)), 0.0)
        acc = acc + _dot((a * a).astype(BF16), w2_ref[sl, :].astype(BF16))
    o_ref[...] = acc


def _mlp(x2, layer, g, w1, w2):
    rows = x2.shape[0]
    return pl.pallas_call(
        _mlp_kernel,
        grid=(rows // MLP_TILE,),
        in_specs=[
            pl.BlockSpec((MLP_TILE, D_MODEL), lambda i: (i, 0)),
            pl.BlockSpec((None, 1, D_MODEL), lambda i: (layer, 0, 0)),
            _resident((None, D_MODEL, D_FF), lambda i: (layer, 0, 0)),
            _resident((None, D_FF, D_MODEL), lambda i: (layer, 0, 0)),
        ],
        out_specs=pl.BlockSpec((MLP_TILE, D_MODEL), lambda i: (i, 0)),
        out_shape=jax.ShapeDtypeStruct(x2.shape, F32),
        compiler_params=_params("parallel"),
        name="mlp",
    )(x2, g, w1, w2)


def _inproj_kernel(x_ref, g_ref, w_ref, qg_ref, kg_ref, q_ref, k_ref, v_ref, xr_ref, gate_ref):
    h = _rms(x_ref[...], g_ref[...]).astype(BF16)
    scale = A_HEAD_DIM ** -0.5 * LOG2_E
    for j, (out_ref, gain_ref, mult) in enumerate(
            ((q_ref, qg_ref, scale), (k_ref, kg_ref, 1.0), (v_ref, None, None),
             (xr_ref, None, None), (gate_ref, None, None))):
        p = _dot(h, w_ref[:, j * MIX_HALF:(j + 1) * MIX_HALF].astype(BF16))
        if j >= 3:
            out_ref[...] = p
            continue
        for hd in range(A_HEADS):
            ph = p[:, hd * A_HEAD_DIM:(hd + 1) * A_HEAD_DIM]
            out_ref[hd] = ph if gain_ref is None else _rms(ph, gain_ref[...]) * mult


def _inproj(x2, layer, e, g, w_in, q_g, k_g):
    rows = x2.shape[0]
    half = jax.ShapeDtypeStruct((rows, MIX_HALF), F32)
    half_spec = pl.BlockSpec((ROW_TILE, MIX_HALF), lambda i: (i, 0))
    heads = jax.ShapeDtypeStruct((A_HEADS, rows, A_HEAD_DIM), F32)
    heads_spec = pl.BlockSpec((A_HEADS, ROW_TILE, A_HEAD_DIM), lambda i: (0, i, 0))
    gain_spec = pl.BlockSpec((None, 1, A_HEAD_DIM), lambda i: (e, 0, 0))
    return pl.pallas_call(
        _inproj_kernel,
        grid=(rows // ROW_TILE,),
        in_specs=[
            pl.BlockSpec((ROW_TILE, D_MODEL), lambda i: (i, 0)),
            pl.BlockSpec((None, 1, D_MODEL), lambda i: (layer, 0, 0)),
            _resident((None, D_MODEL, IN_COLS), lambda i: (e, 0, 0)),
            gain_spec, gain_spec,
        ],
        out_specs=[heads_spec] * 3 + [half_spec] * 2,
        out_shape=[heads] * 3 + [half] * 2,
        compiler_params=_params("parallel"),
        name="inproj",
    )(x2, g, w_in, q_g, k_g)


def _attn_kernel(q_ref, k_ref, v_ref, o_ref, m_ref, l_ref, acc_ref, *, seq):
    jq = lax.broadcasted_iota(jnp.int32, (A_BAND, A_BAND), 0)
    jk = lax.broadcasted_iota(jnp.int32, (A_BAND, A_BAND), 1)
    own_mask = jk <= jq
    jk2 = lax.broadcasted_iota(jnp.int32, (A_BAND, 2 * A_BAND), 1)
    jq2 = lax.broadcasted_iota(jnp.int32, (A_BAND, 2 * A_BAND), 0)
    both_mask = (jk2 >= jq2) & (jk2 <= jq2 + A_BAND)
    neg = -jnp.inf
    lanes = (A_BAND, A_HEAD_DIM)
    ones = jnp.ones(lanes, BF16)

    def rows(start, d):
        if d == 1:
            return pl.ds(pl.multiple_of(start, A_BAND), A_BAND)
        return pl.ds(start, A_BAND, stride=d)

    def load_kv(start, d):
        r = rows(start, d)
        return (k_ref[r, :].astype(BF16),
                jnp.concatenate([v_ref[r, :].astype(BF16), ones], axis=1))

    def process(jobs, d, mode):
        rs = [rows(start, d) for start, _, _ in jobs]
        qs = [q_ref[r, :].astype(BF16) for r in rs]
        scores = []
        for q, (_, own, prev) in zip(qs, jobs):
            if prev is None:
                s = jnp.where(own_mask, _dot_nt(q, own[0]), neg)
            else:
                s = _dot_nt(q, jnp.concatenate([prev[0], own[0]], axis=0))
                s = jnp.where(both_mask, s, neg)
            scores.append(s)
        ms = [jnp.broadcast_to(jnp.max(s, axis=-1, keepdims=True), lanes) for s in scores]
        if mode != "init":
            m_olds = [m_ref[r, :] for r in rs]
            ms = [jnp.maximum(m_old, m) for m_old, m in zip(m_olds, ms)]
        pvs = []
        for s, m, (_, own, prev) in zip(scores, ms, jobs):
            if prev is None:
                v = own[1]
            else:
                v = jnp.concatenate([prev[1], own[1]], axis=0)
                m = jnp.concatenate([m, m], axis=1)
            pvs.append(_dot(jnp.exp2(s - m).astype(BF16), v))
        for i, (r, m, pv) in enumerate(zip(rs, ms, pvs)):
            acc, den = pv[:, :A_HEAD_DIM], pv[:, A_HEAD_DIM:]
            if mode != "init":
                a_old = jnp.exp2(m_olds[i] - m)
                den = a_old * l_ref[r, :] + den
                acc = a_old * acc_ref[r, :] + acc
            if mode == "final":
                o_ref[r, :] = (acc / den).astype(o_ref.dtype)
            else:
                m_ref[r, :] = m
                l_ref[r, :] = den
                acc_ref[r, :] = acc

    def jobs_of(r, c0, n, d, first):
        pitch = A_BAND * d
        kv = [None if first else load_kv(r + (c0 - 1) * pitch, d)]
        kv += [load_kv(r + (c0 + j) * pitch, d) for j in range(n)]
        return [(r + (c0 + j) * pitch, kv[j + 1], kv[j]) for j in range(n)]

    def group(r, c0, n, d, mode, first):
        process(jobs_of(r, c0, n, d, first), d, mode)

    def run_pass(d, mode):
        n_chunks = seq // (A_BAND * d)
        n = min(n_chunks, A_CHUNKS_PER_STEP)
        if n_chunks == n:
            per_step = A_CHUNKS_PER_STEP // n

            def step(i, carry):
                jobs = []
                for j in range(per_step):
                    jobs += jobs_of(i * per_step + j, 0, n, d, True)
                process(jobs, d, mode)
                return carry

            lax.fori_loop(0, d // per_step, step, 0)
            return

        def residue(r, carry):
            group(r, 0, n, d, mode, True)

            def step(i, c):
                group(r, i * n, n, d, mode, False)
                return c

            return lax.fori_loop(1, n_chunks // n, step, carry)

        if d == 1:
            residue(0, 0)
        else:
            lax.fori_loop(0, d, residue, 0)

    modes = ("init",) + ("merge",) * (len(A_DILATIONS) - 2) + ("final",)
    for d, mode in zip(A_DILATIONS, modes):
        run_pass(d, mode)


def _attn(q, k, v):
    _, batch, seq, _ = q.shape
    assert seq % (A_BAND * max(A_DILATIONS)) == 0
    spec = pl.BlockSpec((None, None, seq, A_HEAD_DIM), lambda b, h: (h, b, 0, 0))
    return pl.pallas_call(
        functools.partial(_attn_kernel, seq=seq),
        grid=(batch, A_HEADS),
        in_specs=[spec, spec, spec],
        out_specs=spec,
        out_shape=jax.ShapeDtypeStruct(q.shape, BF16),
        scratch_shapes=[pltpu.VMEM((seq, A_HEAD_DIM), F32)] * 3,
        compiler_params=_params("parallel", "parallel"),
        name="dilated_attn",
    )(q, k, v)


def _gelu_tanh(x):
    c1 = -2.0 * 0.7978845608028654 * LOG2_E
    return x / (1.0 + jnp.exp2(x * (c1 + (c1 * 0.044715) * (x * x))))


def _rglru_out_kernel(x_ref, oa_ref, xr_ref, gate_ref, cw_ref, cb_ref, wg_ref, gab_ref, gxb_ref,
                      lam_ref, wout_ref, o_ref, xpad_ref, carry_ref):
    halo = SUBLANES
    group = (SUBLANES, RG_BLOCK_DIM)

    @pl.when(pl.program_id(1) == 0)
    def _():
        xpad_ref[0:halo, :] = jnp.zeros((halo, RG_WIDTH), F32)
        carry_ref[...] = jnp.zeros((SUBLANES, RG_WIDTH), F32)

    xpad_ref[halo:halo + RG_TILE, :] = xr_ref[...]
    w_out = wout_ref[...].astype(BF16)
    wg = [wg_ref[g].astype(BF16) for g in range(RG_BLOCKS)]
    lam = -lam_ref[...]
    softplus = jnp.maximum(lam, 0.0) + jnp.log1p(jnp.exp(-jnp.abs(lam)))
    grouped = (RG_SUB // SUBLANES, SUBLANES, RG_BLOCK_DIM)
    row = lax.broadcasted_iota(jnp.int32, grouped, 1)
    carry =[carry_ref[:, g * RG_BLOCK_DIM:(g + 1) * RG_BLOCK_DIM] for g in range(RG_BLOCKS)]

    for r0 in range(0, RG_TILE, RG_SUB):
        rs = slice(r0, r0 + RG_SUB)
        xc = cb_ref[...] + cw_ref[CONV_WIDTH - 1:CONV_WIDTH, :] * xr_ref[rs, :]
        for j in range(1, CONV_WIDTH):
            xc = xc + (cw_ref[CONV_WIDTH - 1 - j:CONV_WIDTH - j, :]
                       * xpad_ref[pl.ds(halo - j + r0, RG_SUB), :])
        mixed = [oa_ref[hd, rs, :] for hd in range(A_HEADS)]
        for g in range(RG_BLOCKS):
            sl = slice(g * RG_BLOCK_DIM, (g + 1) * RG_BLOCK_DIM)
            xg = xc[:, sl]
            rg = _dot(xg.astype(BF16), wg[g])
            r = jax.nn.sigmoid(rg[:, :RG_BLOCK_DIM] + gab_ref[:, sl])
            i = jax.nn.sigmoid(rg[:, RG_BLOCK_DIM:] + gxb_ref[:, sl])
            log_a = (-RG_C) * r * softplus[:, sl]
            a = jnp.exp(log_a)
            th = jnp.tanh(log_a)
            u = lax.rsqrt((th - 1.0) / (2.0 * th)) * (i * xg)
            a = a.reshape(grouped)
            u = u.reshape(grouped)
            k = 1
            while k < SUBLANES:
                valid = row >= k
                a_prev = pltpu.roll(a, k, 1)
                u_prev = pltpu.roll(u, k, 1)
                u = jnp.where(valid, a * u_prev + u, u)
                a = jnp.where(valid, a * a_prev, a)
                k *= 2
            a = a.reshape(RG_SUB, RG_BLOCK_DIM)
            u = u.reshape(RG_SUB, RG_BLOCK_DIM)
            hs = []
            c = carry[g]
            for gi in range(0, RG_SUB, SUBLANES):
                h = u[gi:gi + SUBLANES, :] + a[gi:gi + SUBLANES, :] * c
                hs.append(h)
                c = jnp.broadcast_to(h[SUBLANES - 1:SUBLANES, :], group)
            carry[g] = c
            mixed.append((jnp.concatenate(hs, axis=0) * _gelu_tanh(gate_ref[rs, sl])).astype(BF16))
        o_ref[rs, :] = x_ref[rs, :] + _dot(jnp.concatenate(mixed, axis=-1), w_out)

    for g in range(RG_BLOCKS):
        carry_ref[:, g * RG_BLOCK_DIM:(g + 1) * RG_BLOCK_DIM] = carry[g]
    xpad_ref[0:halo, :] = xr_ref[RG_TILE - halo:RG_TILE, :]


def _rglru_out(x, o_a, xr, gate, e, conv_w, conv_b, w_gates, ga_b, gx_b, lam, w_out):
    batch, seq, _ = x.shape
    full = pl.BlockSpec((None, RG_TILE, D_MODEL), lambda b, t: (b, t, 0))
    half = pl.BlockSpec((None, RG_TILE, RG_WIDTH), lambda b, t: (b, t, 0))
    vec = pl.BlockSpec((None, 1, RG_WIDTH), lambda b, t: (e, 0, 0))
    return pl.pallas_call(
        _rglru_out_kernel,
        grid=(batch, seq // RG_TILE),
        in_specs=[
            full,
            pl.BlockSpec((A_HEADS, None, RG_TILE, A_HEAD_DIM), lambda b, t: (0, b, t, 0)),
            half, half,
            pl.BlockSpec((None, CONV_WIDTH, RG_WIDTH), lambda b, t: (e, 0, 0)),
            vec,
            pl.BlockSpec((None, RG_BLOCKS, RG_BLOCK_DIM, 2 * RG_BLOCK_DIM), lambda b, t: (e, 0, 0, 0)),
            vec, vec, vec,
            _resident((None, D_MODEL, D_MODEL), lambda b, t: (e, 0, 0)),
        ],
        out_specs=full,
        out_shape=jax.ShapeDtypeStruct(x.shape, F32),
        scratch_shapes=[
            pltpu.VMEM((RG_TILE + SUBLANES, RG_WIDTH), F32),
            pltpu.VMEM((SUBLANES, RG_WIDTH), F32),
        ],
        compiler_params=_params("arbitrary", "arbitrary"),
        name="rglru_outproj",
    )(x, o_a, xr, gate, conv_w, conv_b, w_gates, ga_b, gx_b, lam, w_out)


def kernel(x, mem, mem_norm_g, mix_norm_g, xattn_norm_g, mlp_norm_g, ev_w_in, ev_q_norm_g, ev_k_norm_g, ev_conv_w, ev_conv_b, ev_gate_a_w, ev_gate_a_b, ev_gate_x_w, ev_gate_x_b, ev_lambda, ev_w_out, od_pool_w, od_scale, xa_w_q, xa_w_kv, xa_q_norm_g, xa_k_norm_g, xa_w_o, mlp_w1, mlp_w2):
    batch, seq, _ = x.shape
    rows = batch * seq

    def vec3(a):
        return a.reshape(a.shape[0], 1, a.shape[1])

    w_in, w_out = ev_w_in, ev_w_out
    w_gates = jnp.concatenate([ev_gate_a_w, ev_gate_x_w], axis=-1)
    pool_w = od_pool_w
    w_q, w_kv, w_o = xa_w_q, xa_w_kv, xa_w_o
    w1, w2 = mlp_w1, mlp_w2
    mix_g, xa_g, mlp_g = vec3(mix_norm_g), vec3(xattn_norm_g), vec3(mlp_norm_g)
    ev_qg, ev_kg = vec3(ev_q_norm_g), vec3(ev_k_norm_g)
    xa_qg = vec3(xa_q_norm_g)
    conv_b, ga_b, gx_b, lam = vec3(ev_conv_b), vec3(ev_gate_a_b), vec3(ev_gate_x_b), vec3(ev_lambda)
    od_sc = vec3(od_scale)

    kt, v_mem = _memkv(mem, mem_norm_g, w_kv, xa_k_norm_g)

    for l in range(DEPTH):
        if l % 2 == 0:
            e = l // 2
            q, k, v, xr, gate = _inproj(x.reshape(rows, D_MODEL), l, e, mix_g, w_in, ev_qg, ev_kg)
            half = (batch, seq, MIX_HALF)
            heads = (A_HEADS, batch, seq, A_HEAD_DIM)
            o_a = _attn(q.reshape(heads), k.reshape(heads), v.reshape(heads))
            x = _rglru_out(x, o_a, xr.reshape(half), gate.reshape(half), e, ev_conv_w, conv_b,
                           w_gates, ga_b, gx_b, lam, w_out)
            pool = None
        else:
            pool = (mix_g, pool_w, od_sc)
        x = _xattn(x, l, xa_g, w_q, xa_qg, kt, v_mem, w_o, pool)
        x = _mlp(x.reshape(rows, D_MODEL), l, mlp_g, w1, w2).reshape(batch, seq, D_MODEL)
    return x
```

```python
import functools

import jax
import jax.numpy as jnp
from jax import lax
from jax.experimental import pallas as pl
from jax.experimental.pallas import tpu as pltpu

D_MODEL = 1024
DEPTH = 4
N_MEM = 256
MIX_HALF = D_MODEL // 2
A_HEADS = 4
A_HEAD_DIM = MIX_HALF // A_HEADS
A_BAND = 128
A_DILATIONS = (16, 4, 1)
A_CHUNKS_PER_STEP = 8
RG_WIDTH = MIX_HALF
RG_BLOCKS = 4
RG_BLOCK_DIM = RG_WIDTH // RG_BLOCKS
RG_C = 8.0
CONV_WIDTH = 4
POOL_WINDOWS = (2, 4, 8, 16)
POOL_GROUP_DIM = D_MODEL // len(POOL_WINDOWS)
POOL_HALO = 16
XA_HEADS = 4
XA_HEAD_DIM = D_MODEL // XA_HEADS
D_FF = 4 * D_MODEL
IN_COLS = 3 * MIX_HALF + 2 * RG_WIDTH
EPS = 1e-6
LOG2_E = 1.4426950408889634

SUBLANES = 8
VMEM_LIMIT = 56 * 1024 * 1024

ROW_TILE = 1024
XA_SUB = 512
MLP_TILE = 512
MLP_CHUNK = 1024
RG_TILE = 512
RG_SUB = 256

BF16 = jnp.bfloat16
F32 = jnp.float32


def _rms(x, g):
    ms = jnp.mean(x * x, axis=-1, keepdims=True)
    return x * lax.rsqrt(ms + EPS) * g


def _dot(a, b):
    return jnp.dot(a, b, preferred_element_type=F32)


def _dot_nt(a, b):
    return lax.dot_general(a, b, (((1,), (1,)), ((), ())), preferred_element_type=F32)


def _resident(shape, index_map):
    return pl.BlockSpec(shape, index_map, pipeline_mode=pl.Buffered(1))


def _params(*sem):
    return pltpu.CompilerParams(dimension_semantics=sem, vmem_limit_bytes=VMEM_LIMIT)


def _memkv_kernel(mem_ref, mg_ref, wkv_ref, kg_ref, kt_ref, v_ref):
    batch = mem_ref.shape[0]
    mem = mem_ref[...].reshape(batch * N_MEM, D_MODEL)
    mem_n = _rms(mem, mg_ref[...]).astype(BF16)
    kv = _dot(mem_n, wkv_ref[...].astype(BF16))
    scale = XA_HEAD_DIM ** -0.5 * LOG2_E
    for h in range(XA_HEADS):
        sl = slice(h * XA_HEAD_DIM, (h + 1) * XA_HEAD_DIM)
        kh = _rms(kv[:, sl], kg_ref[...]) * scale
        for b in range(batch):
            kt_ref[b, sl, :] = kh[b * N_MEM:(b + 1) * N_MEM, :].T.astype(BF16)
    v_ref[...] = kv[:, D_MODEL:].astype(BF16).reshape(batch, N_MEM, D_MODEL)


def _memkv(mem, mem_norm_g, w_kv, k_g):
    batch = mem.shape[0]
    return pl.pallas_call(
        _memkv_kernel,
        grid=(DEPTH,),
        in_specs=[
            pl.BlockSpec((batch, N_MEM, D_MODEL), lambda l: (0, 0, 0)),
            pl.BlockSpec((1, D_MODEL), lambda l: (0, 0)),
            pl.BlockSpec((None, D_MODEL, 2 * D_MODEL), lambda l: (l, 0, 0)),
            pl.BlockSpec((None, 1, XA_HEAD_DIM), lambda l: (l, 0, 0)),
        ],
        out_specs=[
            pl.BlockSpec((None, batch, D_MODEL, N_MEM), lambda l: (l, 0, 0, 0)),
            pl.BlockSpec((None, batch, N_MEM, D_MODEL), lambda l: (l, 0, 0, 0)),
        ],
        out_shape=[
            jax.ShapeDtypeStruct((DEPTH, batch, D_MODEL, N_MEM), BF16),
            jax.ShapeDtypeStruct((DEPTH, batch, N_MEM, D_MODEL), BF16),
        ],
        compiler_params=_params("arbitrary"),
        name="memkv",
    )(mem, mem_norm_g.reshape(1, D_MODEL), w_kv, k_g.reshape(DEPTH, 1, XA_HEAD_DIM))


def _pool_rows(x, r0, pos0, mg_ref, pw_ref, sc_ref, hpad_ref):
    rows = x.shape[0]
    h = _rms(x, mg_ref[...])
    hpad_ref[POOL_HALO + r0:POOL_HALO + r0 + rows, :] = h
    pos = pos0 + lax.broadcasted_iota(jnp.int32, (rows, 1), 0)
    s = hpad_ref[r0:r0 + POOL_HALO + rows, :]
    outs = []
    width = 1
    for gi, w in enumerate(POOL_WINDOWS):
        while width < w:
            s = s + pltpu.roll(s, width, 0)
            width *= 2
        cnt = jnp.minimum(pos + 1, w).astype(F32)
        d = s[POOL_HALO:, :POOL_GROUP_DIM] / cnt - h[:, gi * POOL_GROUP_DIM:(gi + 1) * POOL_GROUP_DIM]
        outs.append(_dot(d.astype(BF16), pw_ref[gi].astype(BF16)))
        if gi + 1 < len(POOL_WINDOWS):
            s = s[:, POOL_GROUP_DIM:]
    return x + jnp.concatenate(outs, axis=-1) * sc_ref[...]


def _xattn_kernel(*refs, pool):
    if pool:
        x_ref, mg_ref, pw_ref, sc_ref, g_ref, wq_ref, qg_ref, kt_ref, v_ref, wo_ref, o_ref, hpad_ref = refs
        t = pl.program_id(1)

        @pl.when(t == 0)
        def _():
            hpad_ref[0:POOL_HALO, :] = jnp.zeros((POOL_HALO, D_MODEL), F32)
    else:
        x_ref, g_ref, wq_ref, qg_ref, kt_ref, v_ref, wo_ref, o_ref = refs
    w_q = wq_ref[...].astype(BF16)
    w_o = wo_ref[...].astype(BF16)
    heads = [slice(hd * XA_HEAD_DIM, (hd + 1) * XA_HEAD_DIM) for hd in range(XA_HEADS)]
    for r0 in range(0, ROW_TILE, XA_SUB):
        rs = slice(r0, r0 + XA_SUB)
        x = x_ref[rs, :]
        if pool:
            x = _pool_rows(x, r0, t * ROW_TILE + r0, mg_ref, pw_ref, sc_ref, hpad_ref)
        h = _rms(x, g_ref[...]).astype(BF16)
        qs = [_dot(h, w_q[:, sl]) for sl in heads]
        qs = [_rms(q, qg_ref[...]).astype(BF16) for q in qs]
        ss = [_dot(q, kt_ref[sl, :]) for q, sl in zip(qs, heads)]
        ps = [jnp.exp2(s - jnp.max(s, axis=-1, keepdims=True)) for s in ss]
        dens = [jnp.sum(p, axis=-1, keepdims=True) for p in ps]
        os_ = [_dot(p.astype(BF16), v_ref[:, sl]) for p, sl in zip(ps, heads)]
        os_ = [(o / den).astype(BF16) for o, den in zip(os_, dens)]
        out = x
        for o, sl in zip(os_, heads):
            out = out + _dot(o, w_o[sl, :])
        o_ref[rs, :] = out
    if pool:
        hpad_ref[0:POOL_HALO, :] = hpad_ref[ROW_TILE:ROW_TILE + POOL_HALO, :]


def _xattn(x, layer, g, w_q, q_g, kt, v, w_o, pool=None):
    batch, seq, _ = x.shape
    tile = pl.BlockSpec((None, ROW_TILE, D_MODEL), lambda b, t: (b, t, 0))
    in_specs = [
        pl.BlockSpec((None, 1, D_MODEL), lambda b, t: (layer, 0, 0)),
        _resident((None, D_MODEL, D_MODEL), lambda b, t: (layer, 0, 0)),
        pl.BlockSpec((None, 1, XA_HEAD_DIM), lambda b, t: (layer, 0, 0)),
        pl.BlockSpec((None, None, D_MODEL, N_MEM), lambda b, t: (layer, b, 0, 0)),
        pl.BlockSpec((None, None, N_MEM, D_MODEL), lambda b, t: (layer, b, 0, 0)),
        _resident((None, D_MODEL, D_MODEL), lambda b, t: (layer, 0, 0)),
    ]
    args = (g, w_q, q_g, kt, v, w_o)
    scratch = []
    if pool is not None:
        o = layer // 2
        in_specs = [
            pl.BlockSpec((None, 1, D_MODEL), lambda b, t: (layer, 0, 0)),
            pl.BlockSpec((None, len(POOL_WINDOWS), POOL_GROUP_DIM, POOL_GROUP_DIM),
                         lambda b, t: (o, 0, 0, 0)),
            pl.BlockSpec((None, 1, D_MODEL), lambda b, t: (o, 0, 0)),
        ] + in_specs
        args = tuple(pool) + args
        scratch = [pltpu.VMEM((POOL_HALO + ROW_TILE, D_MODEL), F32)]
    return pl.pallas_call(
        functools.partial(_xattn_kernel, pool=pool is not None),
        grid=(batch, seq // ROW_TILE),
        in_specs=[tile] + in_specs,
        out_specs=tile,
        out_shape=jax.ShapeDtypeStruct(x.shape, F32),
        scratch_shapes=scratch,
        compiler_params=_params("parallel", "arbitrary" if pool is not None else "parallel"),
        name="xattn" if pool is None else "pool_xattn",
    )(x, *args)


def _mlp_kernel(x_ref, g_ref, w1_hbm, w2_hbm, o_ref, w1_ref, w2_ref, sems, *, layer):
    chunks = [slice(c * MLP_CHUNK, (c + 1) * MLP_CHUNK) for c in range(D_FF // MLP_CHUNK)]

    def copies(c):
        sl = chunks[c]
        return (pltpu.make_async_copy(w1_hbm.at[layer, :, sl], w1_ref.at[:, sl], sems.at[0, c]),
                pltpu.make_async_copy(w2_hbm.at[layer, sl, :], w2_ref.at[sl, :], sems.at[1, c]))

    def body(first):
        if first:
            for c in range(len(chunks)):
                for cp in copies(c):
                    cp.start()
        x = x_ref[...]
        h = _rms(x, g_ref[...]).astype(BF16)
        acc = x
        for c, sl in enumerate(chunks):
            if first:
                for cp in copies(c):
                    cp.wait()
            a = jnp.maximum(_dot(h, w1_ref[:, sl].astype(BF16)), 0.0)
            acc = acc + _dot((a * a).astype(BF16), w2_ref[sl, :].astype(BF16))
        o_ref[...] = acc

    first_step = pl.program_id(0) == 0
    pl.when(first_step)(functools.partial(body, True))
    pl.when(jnp.logical_not(first_step))(functools.partial(body, False))


def _mlp(x2, layer, g, w1, w2):
    rows = x2.shape[0]
    return pl.pallas_call(
        functools.partial(_mlp_kernel, layer=layer),
        grid=(rows // MLP_TILE,),
        in_specs=[
            pl.BlockSpec((MLP_TILE, D_MODEL), lambda i: (i, 0)),
            pl.BlockSpec((None, 1, D_MODEL), lambda i: (layer, 0, 0)),
            pl.BlockSpec(memory_space=pl.ANY),
            pl.BlockSpec(memory_space=pl.ANY),
        ],
        out_specs=pl.BlockSpec((MLP_TILE, D_MODEL), lambda i: (i, 0)),
        out_shape=jax.ShapeDtypeStruct(x2.shape, F32),
        scratch_shapes=[
            pltpu.VMEM((D_MODEL, D_FF), F32),
            pltpu.VMEM((D_FF, D_MODEL), F32),
            pltpu.SemaphoreType.DMA((2, D_FF // MLP_CHUNK)),
        ],
        compiler_params=_params("arbitrary"),
        name="mlp",
    )(x2, g, w1, w2)


def _inproj_kernel(x_ref, g_ref, w_ref, qg_ref, kg_ref, q_ref, k_ref, v_ref, xr_ref, gate_ref):
    h = _rms(x_ref[...], g_ref[...]).astype(BF16)
    scale = A_HEAD_DIM ** -0.5 * LOG2_E
    for j, (out_ref, gain_ref, mult) in enumerate(
            ((q_ref, qg_ref, scale), (k_ref, kg_ref, 1.0), (v_ref, None, None),
             (xr_ref, None, None), (gate_ref, None, None))):
        p = _dot(h, w_ref[:, j * MIX_HALF:(j + 1) * MIX_HALF].astype(BF16))
        if j >= 3:
            out_ref[...] = p
            continue
        for hd in range(A_HEADS):
            ph = p[:, hd * A_HEAD_DIM:(hd + 1) * A_HEAD_DIM]
            out_ref[hd] = ph if gain_ref is None else _rms(ph, gain_ref[...]) * mult


def _inproj(x2, layer, e, g, w_in, q_g, k_g):
    rows = x2.shape[0]
    half = jax.ShapeDtypeStruct((rows, MIX_HALF), F32)
    half_spec = pl.BlockSpec((ROW_TILE, MIX_HALF), lambda i: (i, 0))
    heads = jax.ShapeDtypeStruct((A_HEADS, rows, A_HEAD_DIM), F32)
    heads_spec = pl.BlockSpec((A_HEADS, ROW_TILE, A_HEAD_DIM), lambda i: (0, i, 0))
    gain_spec = pl.BlockSpec((None, 1, A_HEAD_DIM), lambda i: (e, 0, 0))
    return pl.pallas_call(
        _inproj_kernel,
        grid=(rows // ROW_TILE,),
        in_specs=[
            pl.BlockSpec((ROW_TILE, D_MODEL), lambda i: (i, 0)),
            pl.BlockSpec((None, 1, D_MODEL), lambda i: (layer, 0, 0)),
            _resident((None, D_MODEL, IN_COLS), lambda i: (e, 0, 0)),
            gain_spec, gain_spec,
        ],
        out_specs=[heads_spec] * 3 + [half_spec] * 2,
        out_shape=[heads] * 3 + [half] * 2,
        compiler_params=_params("parallel"),
        name="inproj",
    )(x2, g, w_in, q_g, k_g)


def _attn_kernel(q_ref, k_ref, v_ref, o_ref, m_ref, l_ref, acc_ref, *, seq):
    jq = lax.broadcasted_iota(jnp.int32, (A_BAND, A_BAND), 0)
    jk = lax.broadcasted_iota(jnp.int32, (A_BAND, A_BAND), 1)
    own_mask = jk <= jq
    jk2 = lax.broadcasted_iota(jnp.int32, (A_BAND, 2 * A_BAND), 1)
    jq2 = lax.broadcasted_iota(jnp.int32, (A_BAND, 2 * A_BAND), 0)
    both_mask = (jk2 >= jq2) & (jk2 <= jq2 + A_BAND)
    neg = -jnp.inf
    lanes = (A_BAND, A_HEAD_DIM)
    ones = jnp.ones(lanes, BF16)

    def rows(start, d):
        if d == 1:
            return pl.ds(pl.multiple_of(start, A_BAND), A_BAND)
        return pl.ds(start, A_BAND, stride=d)

    def load_kv(start, d):
        r = rows(start, d)
        return (k_ref[r, :].astype(BF16),
                jnp.concatenate([v_ref[r, :].astype(BF16), ones], axis=1))

    def process(jobs, d, mode):
        rs = [rows(start, d) for start, _, _ in jobs]
        qs = [q_ref[r, :].astype(BF16) for r in rs]
        scores = []
        for q, (_, own, prev) in zip(qs, jobs):
            if prev is None:
                s = jnp.where(own_mask, _dot_nt(q, own[0]), neg)
            else:
                s = _dot_nt(q, jnp.concatenate([prev[0], own[0]], axis=0))
                s = jnp.where(both_mask, s, neg)
            scores.append(s)
        ms = [jnp.broadcast_to(jnp.max(s, axis=-1, keepdims=True), lanes) for s in scores]
        if mode != "init":
            m_olds = [m_ref[r, :] for r in rs]
            ms = [jnp.maximum(m_old, m) for m_old, m in zip(m_olds, ms)]
        pvs = []
        for s, m, (_, own, prev) in zip(scores, ms, jobs):
            if prev is None:
                v = own[1]
            else:
                v = jnp.concatenate([prev[1], own[1]], axis=0)
                m = jnp.concatenate([m, m], axis=1)
            pvs.append(_dot(jnp.exp2(s - m).astype(BF16), v))
        for i, (r, m, pv) in enumerate(zip(rs, ms, pvs)):
            acc, den = pv[:, :A_HEAD_DIM], pv[:, A_HEAD_DIM:]
            if mode != "init":
                a_old = jnp.exp2(m_olds[i] - m)
                den = a_old * l_ref[r, :] + den
                acc = a_old * acc_ref[r, :] + acc
            if mode == "final":
                o_ref[r, :] = (acc / den).astype(o_ref.dtype)
            else:
                m_ref[r, :] = m
                l_ref[r, :] = den
                acc_ref[r, :] = acc

    def jobs_of(r, c0, n, d, first):
        pitch = A_BAND * d
        kv = [None if first else load_kv(r + (c0 - 1) * pitch, d)]
        kv += [load_kv(r + (c0 + j) * pitch, d) for j in range(n)]
        return [(r + (c0 + j) * pitch, kv[j + 1], kv[j]) for j in range(n)]

    def group(r, c0, n, d, mode, first):
        process(jobs_of(r, c0, n, d, first), d, mode)

    def run_pass(d, mode):
        n_chunks = seq // (A_BAND * d)
        n = min(n_chunks, A_CHUNKS_PER_STEP)
        if n_chunks == n:
            per_step = A_CHUNKS_PER_STEP // n

            def step(i, carry):
                jobs = []
                for j in range(per_step):
                    jobs += jobs_of(i * per_step + j, 0, n, d, True)
                process(jobs, d, mode)
                return carry

            lax.fori_loop(0, d // per_step, step, 0)
            return

        def residue(r, carry):
            group(r, 0, n, d, mode, True)

            def step(i, c):
                group(r, i * n, n, d, mode, False)
                return c

            return lax.fori_loop(1, n_chunks // n, step, carry)

        if d == 1:
            residue(0, 0)
        else:
            lax.fori_loop(0, d, residue, 0)

    modes = ("init",) + ("merge",) * (len(A_DILATIONS) - 2) + ("final",)
    for d, mode in zip(A_DILATIONS, modes):
        run_pass(d, mode)


def _attn(q, k, v):
    _, batch, seq, _ = q.shape
    assert seq % (A_BAND * max(A_DILATIONS)) == 0
    spec = pl.BlockSpec((None, None, seq, A_HEAD_DIM), lambda b, h: (h, b, 0, 0))
    return pl.pallas_call(
        functools.partial(_attn_kernel, seq=seq),
        grid=(batch, A_HEADS),
        in_specs=[spec, spec, spec],
        out_specs=spec,
        out_shape=jax.ShapeDtypeStruct(q.shape, BF16),
        scratch_shapes=[pltpu.VMEM((seq, A_HEAD_DIM), F32)] * 3,
        compiler_params=_params("parallel", "parallel"),
        name="dilated_attn",
    )(q, k, v)


def _gelu_tanh(x):
    c1 = -2.0 * 0.7978845608028654 * LOG2_E
    return x / (1.0 + jnp.exp2(x * (c1 + (c1 * 0.044715) * (x * x))))


def _rglru_out_kernel(x_ref, oa_ref, xr_ref, gate_ref, cw_ref, cb_ref, wg_ref, gab_ref, gxb_ref,
                      lam_ref, wout_ref, o_ref, xpad_ref, carry_ref):
    halo = SUBLANES
    group = (SUBLANES, RG_BLOCK_DIM)

    @pl.when(pl.program_id(1) == 0)
    def _():
        xpad_ref[0:halo, :] = jnp.zeros((halo, RG_WIDTH), F32)
        carry_ref[...] = jnp.zeros((SUBLANES, RG_WIDTH), F32)

    xpad_ref[halo:halo + RG_TILE, :] = xr_ref[...]
    w_out = wout_ref[...].astype(BF16)
    wg = [wg_ref[g].astype(BF16) for g in range(RG_BLOCKS)]
    lam = -lam_ref[...]
    softplus = jnp.maximum(lam, 0.0) + jnp.log1p(jnp.exp(-jnp.abs(lam)))
    grouped = (RG_SUB // SUBLANES, SUBLANES, RG_BLOCK_DIM)
    row = lax.broadcasted_iota(jnp.int32, grouped, 1)
    carry = [carry_ref[:, g * RG_BLOCK_DIM:(g + 1) * RG_BLOCK_DIM] for g in range(RG_BLOCKS)]

    for r0 in range(0, RG_TILE, RG_SUB):
        rs = slice(r0, r0 + RG_SUB)
        xc = cb_ref[...] + cw_ref[CONV_WIDTH - 1:CONV_WIDTH, :] * xr_ref[rs, :]
        for j in range(1, CONV_WIDTH):
            xc = xc + (cw_ref[CONV_WIDTH - 1 - j:CONV_WIDTH - j, :]
                       * xpad_ref[pl.ds(halo - j + r0, RG_SUB), :])
        mixed = [oa_ref[hd, rs, :] for hd in range(A_HEADS)]
        for g in range(RG_BLOCKS):
            sl = slice(g * RG_BLOCK_DIM, (g + 1) * RG_BLOCK_DIM)
            xg = xc[:, sl]
            rg = _dot(xg.astype(BF16), wg[g])
            r = jax.nn.sigmoid(rg[:, :RG_BLOCK_DIM] + gab_ref[:, sl])
            i = jax.nn.sigmoid(rg[:, RG_BLOCK_DIM:] + gxb_ref[:, sl])
            log_a = (-RG_C) * r * softplus[:, sl]
            a = jnp.exp(log_a)
            th = jnp.tanh(log_a)
            u = lax.rsqrt((th - 1.0) / (2.0 * th)) * (i * xg)
            a = a.reshape(grouped)
            u = u.reshape(grouped)
            k = 1
            while k < SUBLANES:
                valid = row >= k
                a_prev = pltpu.roll(a, k, 1)
                u_prev = pltpu.roll(u, k, 1)
                u = jnp.where(valid, a * u_prev + u, u)
                a = jnp.where(valid, a * a_prev, a)
                k *= 2
            a = a.reshape(RG_SUB, RG_BLOCK_DIM)
            u = u.reshape(RG_SUB, RG_BLOCK_DIM)
            hs = []
            c = carry[g]
            for gi in range(0, RG_SUB, SUBLANES):
                h = u[gi:gi + SUBLANES, :] + a[gi:gi + SUBLANES, :] * c
                hs.append(h)
                c = jnp.broadcast_to(h[SUBLANES - 1:SUBLANES, :], group)
            carry[g] = c
            mixed.append((jnp.concatenate(hs, axis=0) * _gelu_tanh(gate_ref[rs, sl])).astype(BF16))
        o_ref[rs, :] = x_ref[rs, :] + _dot(jnp.concatenate(mixed, axis=-1), w_out)

    for g in range(RG_BLOCKS):
        carry_ref[:, g * RG_BLOCK_DIM:(g + 1) * RG_BLOCK_DIM] = carry[g]
    xpad_ref[0:halo, :] = xr_ref[RG_TILE - halo:RG_TILE, :]


def _rglru_out(x, o_a, xr, gate, e, conv_w, conv_b, w_gates, ga_b, gx_b, lam, w_out):
    batch, seq, _ = x.shape
    full = pl.BlockSpec((None, RG_TILE, D_MODEL), lambda b, t: (b, t, 0))
    half = pl.BlockSpec((None, RG_TILE, RG_WIDTH), lambda b, t: (b, t, 0))
    vec = pl.BlockSpec((None, 1, RG_WIDTH), lambda b, t: (e, 0, 0))
    return pl.pallas_call(
        _rglru_out_kernel,
        grid=(batch, seq // RG_TILE),
        in_specs=[
            full,
            pl.BlockSpec((A_HEADS, None, RG_TILE, A_HEAD_DIM), lambda b, t: (0, b, t, 0)),
            half, half,
            pl.BlockSpec((None, CONV_WIDTH, RG_WIDTH), lambda b, t: (e, 0, 0)),
            vec,
            pl.BlockSpec((None, RG_BLOCKS, RG_BLOCK_DIM, 2 * RG_BLOCK_DIM), lambda b, t: (e, 0, 0, 0)),
            vec, vec, vec,
            _resident((None, D_MODEL, D_MODEL), lambda b, t: (e, 0, 0)),
        ],
        out_specs=full,
        out_shape=jax.ShapeDtypeStruct(x.shape, F32),
        scratch_shapes=[
            pltpu.VMEM((RG_TILE + SUBLANES, RG_WIDTH), F32),
            pltpu.VMEM((SUBLANES, RG_WIDTH), F32),
        ],
        compiler_params=_params("arbitrary", "arbitrary"),
        name="rglru_outproj",
    )(x, o_a, xr, gate, conv_w, conv_b, w_gates, ga_b, gx_b, lam, w_out)


def kernel(x, mem, mem_norm_g, mix_norm_g, xattn_norm_g, mlp_norm_g, ev_w_in, ev_q_norm_g, ev_k_norm_g, ev_conv_w, ev_conv_b, ev_gate_a_w, ev_gate_a_b, ev_gate_x_w, ev_gate_x_b, ev_lambda, ev_w_out, od_pool_w, od_scale, xa_w_q, xa_w_kv, xa_q_norm_g, xa_k_norm_g, xa_w_o, mlp_w1, mlp_w2):
    batch, seq, _ = x.shape
    rows = batch * seq

    def vec3(a):
        return a.reshape(a.shape[0], 1, a.shape[1])

    w_in, w_out = ev_w_in, ev_w_out
    w_gates = jnp.concatenate([ev_gate_a_w, ev_gate_x_w], axis=-1)
    pool_w = od_pool_w
    w_q, w_kv, w_o = xa_w_q, xa_w_kv, xa_w_o
    w1, w2 = mlp_w1, mlp_w2
    mix_g, xa_g, mlp_g = vec3(mix_norm_g), vec3(xattn_norm_g), vec3(mlp_norm_g)
    ev_qg, ev_kg = vec3(ev_q_norm_g), vec3(ev_k_norm_g)
    xa_qg = vec3(xa_q_norm_g)
    conv_b, ga_b, gx_b, lam = vec3(ev_conv_b), vec3(ev_gate_a_b), vec3(ev_gate_x_b), vec3(ev_lambda)
    od_sc = vec3(od_scale)

    kt, v_mem = _memkv(mem, mem_norm_g, w_kv, xa_k_norm_g)

    for l in range(DEPTH):
        if l % 2 == 0:
            e = l // 2
            q, k, v, xr, gate = _inproj(x.reshape(rows, D_MODEL), l, e, mix_g, w_in, ev_qg, ev_kg)
            half = (batch, seq, MIX_HALF)
            heads = (A_HEADS, batch, seq, A_HEAD_DIM)
            o_a = _attn(q.reshape(heads), k.reshape(heads), v.reshape(heads))
            x = _rglru_out(x, o_a, xr.reshape(half), gate.reshape(half), e, ev_conv_w, conv_b,
                           w_gates, ga_b, gx_b, lam, w_out)
            pool = None
        else:
            pool = (mix_g, pool_w, od_sc)
        x = _xattn(x, l, xa_g, w_q, xa_qg, kt, v_mem, w_o, pool)
        x = _mlp(x.reshape(rows, D_MODEL), l, mlp_g, w1, w2).reshape(batch, seq, D_MODEL)
    return x
```

```python
import functools

import jax
import jax.numpy as jnp
from jax import lax
from jax.experimental import pallas as pl
from jax.experimental.pallas import tpu as pltpu

D_MODEL = 1024
DEPTH = 4
N_MEM = 256
MIX_HALF = D_MODEL // 2
A_HEADS = 4
A_HEAD_DIM = MIX_HALF // A_HEADS
A_BAND = 128
A_DILATIONS = (16, 4, 1)
A_CHUNKS_PER_STEP = 8
RG_WIDTH = MIX_HALF
RG_BLOCKS = 4
RG_BLOCK_DIM = RG_WIDTH // RG_BLOCKS
RG_C = 8.0
CONV_WIDTH = 4
POOL_WINDOWS = (2, 4, 8, 16)
POOL_GROUP_DIM = D_MODEL // len(POOL_WINDOWS)
POOL_HALO = 16
XA_HEADS = 4
XA_HEAD_DIM = D_MODEL // XA_HEADS
D_FF = 4 * D_MODEL
IN_COLS = 3 * MIX_HALF + 2 * RG_WIDTH
EPS = 1e-6
LOG2_E = 1.4426950408889634

SUBLANES = 8
VMEM_LIMIT = 56 * 1024 * 1024

ROW_TILE = 1024
XA_SUB = 512
MLP_TILE = 512
MLP_CHUNK = 1024
RG_TILE = 512
RG_SUB = 256

BF16 = jnp.bfloat16
F32 = jnp.float32


def _rms(x, g):
    ms = jnp.mean(x * x, axis=-1, keepdims=True)
    return x * lax.rsqrt(ms + EPS) * g


def _dot(a, b):
    return jnp.dot(a, b, preferred_element_type=F32)


def _dot_nt(a, b):
    return lax.dot_general(a, b, (((1,), (1,)), ((), ())), preferred_element_type=F32)


def _resident(shape, index_map):
    return pl.BlockSpec(shape, index_map, pipeline_mode=pl.Buffered(1))


def _params(*sem):
    return pltpu.CompilerParams(dimension_semantics=sem, vmem_limit_bytes=VMEM_LIMIT)


def _memkv_kernel(mem_ref, mg_ref, wkv_ref, kg_ref, kt_ref, v_ref):
    batch = mem_ref.shape[0]
    mem = mem_ref[...].reshape(batch * N_MEM, D_MODEL)
    mem_n = _rms(mem, mg_ref[...]).astype(BF16)
    kv = _dot(mem_n, wkv_ref[...].astype(BF16))
    scale = XA_HEAD_DIM ** -0.5 * LOG2_E
    for h in range(XA_HEADS):
        sl = slice(h * XA_HEAD_DIM, (h + 1) * XA_HEAD_DIM)
        kh = _rms(kv[:, sl], kg_ref[...]) * scale
        for b in range(batch):
            kt_ref[b, sl, :] = kh[b * N_MEM:(b + 1) * N_MEM, :].T.astype(BF16)
    v_ref[...] = kv[:, D_MODEL:].astype(BF16).reshape(batch, N_MEM, D_MODEL)


def _memkv(mem, mem_norm_g, w_kv, k_g):
    batch = mem.shape[0]
    return pl.pallas_call(
        _memkv_kernel,
        grid=(DEPTH,),
        in_specs=[
            pl.BlockSpec((batch, N_MEM, D_MODEL), lambda l: (0, 0, 0)),
            pl.BlockSpec((1, D_MODEL), lambda l: (0, 0)),
            pl.BlockSpec((None, D_MODEL, 2 * D_MODEL), lambda l: (l, 0, 0)),
            pl.BlockSpec((None, 1, XA_HEAD_DIM), lambda l: (l, 0, 0)),
        ],
        out_specs=[
            pl.BlockSpec((None, batch, D_MODEL, N_MEM), lambda l: (l, 0, 0, 0)),
            pl.BlockSpec((None, batch, N_MEM, D_MODEL), lambda l: (l, 0, 0, 0)),
        ],
        out_shape=[
            jax.ShapeDtypeStruct((DEPTH, batch, D_MODEL, N_MEM), BF16),
            jax.ShapeDtypeStruct((DEPTH, batch, N_MEM, D_MODEL), BF16),
        ],
        compiler_params=_params("arbitrary"),
        name="memkv",
    )(mem, mem_norm_g.reshape(1, D_MODEL), w_kv, k_g.reshape(DEPTH, 1, XA_HEAD_DIM))


def _pool_rows(x, r0, pos0, mg_ref, pw_ref, sc_ref, hpad_ref):
    rows = x.shape[0]
    h = _rms(x, mg_ref[...])
    hpad_ref[POOL_HALO + r0:POOL_HALO + r0 + rows, :] = h
    pos = pos0 + lax.broadcasted_iota(jnp.int32, (rows, 1), 0)
    s = hpad_ref[r0:r0 + POOL_HALO + rows, :]
    outs = []
    width = 1
    for gi, w in enumerate(POOL_WINDOWS):
        while width < w:
            s = s + pltpu.roll(s, width, 0)
            width *= 2
        cnt = jnp.minimum(pos + 1, w).astype(F32)
        d = s[POOL_HALO:, :POOL_GROUP_DIM] / cnt - h[:, gi * POOL_GROUP_DIM:(gi + 1) * POOL_GROUP_DIM]
        outs.append(_dot(d.astype(BF16), pw_ref[gi].astype(BF16)))
        if gi + 1 < len(POOL_WINDOWS):
            s = s[:, POOL_GROUP_DIM:]
    return x + jnp.concatenate(outs, axis=-1) * sc_ref[...]


def _xattn_kernel(*refs, pool):
    if pool:
        x_ref, mg_ref, pw_ref, sc_ref, g_ref, wq_ref, qg_ref, kt_ref, v_ref, wo_ref, o_ref, hpad_ref = refs
        t = pl.program_id(1)

        @pl.when(t == 0)
        def _():
            hpad_ref[0:POOL_HALO, :] = jnp.zeros((POOL_HALO, D_MODEL), F32)
    else:
        x_ref, g_ref, wq_ref, qg_ref, kt_ref, v_ref, wo_ref, o_ref = refs
    w_q = wq_ref[...].astype(BF16)
    w_o = wo_ref[...].astype(BF16)
    heads = [slice(hd * XA_HEAD_DIM, (hd + 1) * XA_HEAD_DIM) for hd in range(XA_HEADS)]
    for r0 in range(0, ROW_TILE, XA_SUB):
        rs = slice(r0, r0 + XA_SUB)
        x = x_ref[rs, :]
        if pool:
            x = _pool_rows(x, r0, t * ROW_TILE + r0, mg_ref, pw_ref, sc_ref, hpad_ref)
        h = _rms(x, g_ref[...]).astype(BF16)
        qs = [_dot(h, w_q[:, sl]) for sl in heads]
        qs = [_rms(q, qg_ref[...]).astype(BF16) for q in qs]
        ss = [_dot(q, kt_ref[sl, :]) for q, sl in zip(qs, heads)]
        ps = [jnp.exp2(s - jnp.max(s, axis=-1, keepdims=True)) for s in ss]
        dens = [jnp.sum(p, axis=-1, keepdims=True) for p in ps]
        os_ = [_dot(p.astype(BF16), v_ref[:, sl]) for p, sl in zip(ps, heads)]
        os_ = [(o / den).astype(BF16) for o, den in zip(os_, dens)]
        out = x
        for o, sl in zip(os_, heads):
            out = out + _dot(o, w_o[sl, :])
        o_ref[rs, :] = out
    if pool:
        hpad_ref[0:POOL_HALO, :] = hpad_ref[ROW_TILE:ROW_TILE + POOL_HALO, :]


def _xattn(x, layer, g, w_q, q_g, kt, v, w_o, pool=None):
    batch, seq, _ = x.shape
    tile = pl.BlockSpec((None, ROW_TILE, D_MODEL), lambda b, t: (b, t, 0))
    in_specs = [
        pl.BlockSpec((None, 1, D_MODEL), lambda b, t: (layer, 0, 0)),
        _resident((None, D_MODEL, D_MODEL), lambda b, t: (layer, 0, 0)),
        pl.BlockSpec((None, 1, XA_HEAD_DIM), lambda b, t: (layer, 0, 0)),
        pl.BlockSpec((None, None, D_MODEL, N_MEM), lambda b, t: (layer, b, 0, 0)),
        pl.BlockSpec((None, None, N_MEM, D_MODEL), lambda b, t: (layer, b, 0, 0)),
        _resident((None, D_MODEL, D_MODEL), lambda b, t: (layer, 0, 0)),
    ]
    args = (g, w_q, q_g, kt, v, w_o)
    scratch = []
    if pool is not None:
        o = layer // 2
        in_specs = [
            pl.BlockSpec((None, 1, D_MODEL), lambda b, t: (layer, 0, 0)),
            pl.BlockSpec((None, len(POOL_WINDOWS), POOL_GROUP_DIM, POOL_GROUP_DIM),
                         lambda b, t: (o, 0, 0, 0)),
            pl.BlockSpec((None, 1, D_MODEL), lambda b, t: (o, 0, 0)),
        ] + in_specs
        args = tuple(pool) + args
        scratch = [pltpu.VMEM((POOL_HALO + ROW_TILE, D_MODEL), F32)]
    return pl.pallas_call(
        functools.partial(_xattn_kernel, pool=pool is not None),
        grid=(batch, seq // ROW_TILE),
        in_specs=[tile] + in_specs,
        out_specs=tile,
        out_shape=jax.ShapeDtypeStruct(x.shape, F32),
        scratch_shapes=scratch,
        compiler_params=_params("parallel", "arbitrary" if pool is not None else "parallel"),
        name="xattn" if pool is None else "pool_xattn",
    )(x, *args)


def _mlp_kernel(x_ref, g_ref, w1_hbm, w2_hbm, o_ref, w1_ref, w2_ref, sems, *, layer):
    chunks = [slice(c * MLP_CHUNK, (c + 1) * MLP_CHUNK) for c in range(D_FF // MLP_CHUNK)]

    def copies(c):
        sl = chunks[c]
        return (pltpu.make_async_copy(w1_hbm.at[layer, :, sl], w1_ref.at[:, sl], sems.at[0, c]),
                pltpu.make_async_copy(w2_hbm.at[layer, sl, :], w2_ref.at[sl, :], sems.at[1, c]))

    def body(first):
        if first:
            for c in range(len(chunks)):
                for cp in copies(c):
                    cp.start()
        x = x_ref[...]
        h = _rms(x, g_ref[...]).astype(BF16)
        acc = x
        for c, sl in enumerate(chunks):
            if first:
                for cp in copies(c):
                    cp.wait()
            a = jnp.maximum(_dot(h, w1_ref[:, sl].astype(BF16)), 0.0)
            acc = acc + _dot((a * a).astype(BF16), w2_ref[sl, :].astype(BF16))
        o_ref[...] = acc

    first_step = pl.program_id(0) == 0
    pl.when(first_step)(functools.partial(body, True))
    pl.when(jnp.logical_not(first_step))(functools.partial(body, False))


def _mlp(x2, layer, g, w1, w2):
    rows = x2.shape[0]
    return pl.pallas_call(
        functools.partial(_mlp_kernel, layer=layer),
        grid=(rows // MLP_TILE,),
        in_specs=[
            pl.BlockSpec((MLP_TILE, D_MODEL), lambda i: (i, 0)),
            pl.BlockSpec((None, 1, D_MODEL), lambda i: (layer, 0, 0)),
            pl.BlockSpec(memory_space=pl.ANY),
            pl.BlockSpec(memory_space=pl.ANY),
        ],
        out_specs=pl.BlockSpec((MLP_TILE, D_MODEL), lambda i: (i, 0)),
        out_shape=jax.ShapeDtypeStruct(x2.shape, F32),
        scratch_shapes=[
            pltpu.VMEM((D_MODEL, D_FF), F32),
            pltpu.VMEM((D_FF, D_MODEL), F32),
            pltpu.SemaphoreType.DMA((2, D_FF // MLP_CHUNK)),
        ],
        compiler_params=_params("arbitrary"),
        name="mlp",
    )(x2, g, w1, w2)


def _inproj_kernel(x_ref, g_ref, w_ref, qg_ref, kg_ref, *refs):
    n_d = len(A_DILATIONS)
    xr_ref, gate_ref, stage_ref = refs[3 * n_d:]
    h = _rms(x_ref[...], g_ref[...]).astype(BF16)
    scale = A_HEAD_DIM ** -0.5 * LOG2_E
    for j, (gain_ref, mult) in enumerate(((qg_ref, scale), (kg_ref, 1.0), (None, None))):
        p = _dot(h, w_ref[:, j * MIX_HALF:(j + 1) * MIX_HALF].astype(BF16))
        outs = dict(zip(A_DILATIONS, refs[j * n_d:(j + 1) * n_d]))
        for hd in range(A_HEADS):
            ph = p[:, hd * A_HEAD_DIM:(hd + 1) * A_HEAD_DIM]
            if gain_ref is not None:
                ph = _rms(ph, gain_ref[...]) * mult
            outs[1][hd] = ph.astype(BF16)
            stage = stage_ref.at[hd]
            stage[...] = ph
            staged_d = 1
            for d in sorted(d for d in outs if d > 1):
                step = d // staged_d
                rows_d = ROW_TILE // d
                gathered = []
                for r in range(d):
                    base = (r % staged_d) * (ROW_TILE // staged_d) + r // staged_d
                    gathered.append(stage[pl.ds(base, rows_d, stride=step), :])
                for r, val in enumerate(gathered):
                    outs[d][hd, r] = val.astype(BF16)
                    stage[r * rows_d:(r + 1) * rows_d, :] = val
                staged_d = d
    for j, out_ref in ((3, xr_ref), (4, gate_ref)):
        out_ref[...] = _dot(h, w_ref[:, j * MIX_HALF:(j + 1) * MIX_HALF].astype(BF16))


def _inproj(x, layer, e, g, w_in, q_g, k_g):
    batch, seq, _ = x.shape
    half = jax.ShapeDtypeStruct((batch, seq, MIX_HALF), F32)
    half_spec = pl.BlockSpec((None, ROW_TILE, MIX_HALF), lambda b, t: (b, t, 0))
    gain_spec = pl.BlockSpec((None, 1, A_HEAD_DIM), lambda b, t: (e, 0, 0))

    def by_residue_spec(d):
        if d == 1:
            return pl.BlockSpec((A_HEADS, None, ROW_TILE, A_HEAD_DIM), lambda b, t: (0, b, t, 0))
        return pl.BlockSpec((A_HEADS, None, d, ROW_TILE // d, A_HEAD_DIM), lambda b, t: (0, b, 0, t, 0))

    by_residue = [jax.ShapeDtypeStruct(_by_residue_shape(batch, seq, d), BF16) for d in A_DILATIONS]
    outs = pl.pallas_call(
        _inproj_kernel,
        grid=(batch, seq // ROW_TILE),
        in_specs=[
            pl.BlockSpec((None, ROW_TILE, D_MODEL), lambda b, t: (b, t, 0)),
            pl.BlockSpec((None, 1, D_MODEL), lambda b, t: (layer, 0, 0)),
            _resident((None, D_MODEL, IN_COLS), lambda b, t: (e, 0, 0)),
            gain_spec, gain_spec,
        ],
        out_specs=[by_residue_spec(d) for d in A_DILATIONS] * 3 + [half_spec] * 2,
        out_shape=by_residue * 3 + [half] * 2,
        scratch_shapes=[pltpu.VMEM((A_HEADS, ROW_TILE, A_HEAD_DIM), F32)],
        compiler_params=_params("parallel", "parallel"),
        name="inproj",
    )(x, g, w_in, q_g, k_g)
    return outs[:-2], outs[-2], outs[-1]


def _attn_kernel(*refs, seq):
    n_d = len(A_DILATIONS)
    q_refs, k_refs, v_refs = (dict(zip(A_DILATIONS, refs[i * n_d:(i + 1) * n_d])) for i in range(3))
    o_ref, m_ref, l_ref, acc_ref = refs[3 * n_d:]
    jq = lax.broadcasted_iota(jnp.int32, (A_BAND, A_BAND), 0)
    jk = lax.broadcasted_iota(jnp.int32, (A_BAND, A_BAND), 1)
    own_mask = jk <= jq
    jk2 = lax.broadcasted_iota(jnp.int32, (A_BAND, 2 * A_BAND), 1)
    jq2 = lax.broadcasted_iota(jnp.int32, (A_BAND, 2 * A_BAND), 0)
    both_mask = (jk2 >= jq2) & (jk2 <= jq2 + A_BAND)
    neg = -jnp.inf
    lanes = (A_BAND, A_HEAD_DIM)
    ones = jnp.ones(lanes, BF16)

    def rows(start, d):
        if d == 1:
            return pl.ds(pl.multiple_of(start, A_BAND), A_BAND)
        return pl.ds(start, A_BAND, stride=d)

    def chunk_of(ref, r, c, d):
        at = pl.ds(pl.multiple_of(c * A_BAND, A_BAND), A_BAND)
        return ref[at, :] if d == 1 else ref[r, at, :]

    def load_kv(r, c, d):
        return (chunk_of(k_refs[d], r, c, d),
                jnp.concatenate([chunk_of(v_refs[d], r, c, d), ones], axis=1))

    def process(jobs, d, mode):
        rs = [rows(r + c * (A_BAND * d), d) for r, c, _, _ in jobs]
        qs = [chunk_of(q_refs[d], r, c, d) for r, c, _, _ in jobs]
        jobs = [(None, own, prev) for _, _, own, prev in jobs]
        scores = []
        for q, (_, own, prev) in zip(qs, jobs):
            if prev is None:
                s = jnp.where(own_mask, _dot_nt(q, own[0]), neg)
            else:
                s = _dot_nt(q, jnp.concatenate([prev[0], own[0]], axis=0))
                s = jnp.where(both_mask, s, neg)
            scores.append(s)
        ms = [jnp.broadcast_to(jnp.max(s, axis=-1, keepdims=True), lanes) for s in scores]
        if mode != "init":
            m_olds = [m_ref[r, :] for r in rs]
            ms = [jnp.maximum(m_old, m) for m_old, m in zip(m_olds, ms)]
        pvs = []
        for s, m, (_, own, prev) in zip(scores, ms, jobs):
            if prev is None:
                v = own[1]
            else:
                v = jnp.concatenate([prev[1], own[1]], axis=0)
                m = jnp.concatenate([m, m], axis=1)
            pvs.append(_dot(jnp.exp2(s - m).astype(BF16), v))
        for i, (r, m, pv) in enumerate(zip(rs, ms, pvs)):
            acc, den = pv[:, :A_HEAD_DIM], pv[:, A_HEAD_DIM:]
            if mode != "init":
                a_old = jnp.exp2(m_olds[i] - m)
                den = a_old * l_ref[r, :] + den
                acc = a_old * acc_ref[r, :] + acc
            if mode == "final":
                o_ref[r, :] = (acc / den).astype(o_ref.dtype)
            else:
                m_ref[r, :] = m
                l_ref[r, :] = den
                acc_ref[r, :] = acc

    def jobs_of(r, c0, n, d, first):
        kv = [None if first else load_kv(r, c0 - 1, d)]
        kv += [load_kv(r, c0 + j, d) for j in range(n)]
        return [(r, c0 + j, kv[j + 1], kv[j]) for j in range(n)]

    def group(r, c0, n, d, mode, first):
        process(jobs_of(r, c0, n, d, first), d, mode)

    def run_pass(d, mode):
        n_chunks = seq // (A_BAND * d)
        n = min(n_chunks, A_CHUNKS_PER_STEP)
        if n_chunks == n:
            per_step = A_CHUNKS_PER_STEP // n

            def step(i, carry):
                jobs = []
                for j in range(per_step):
                    jobs += jobs_of(i * per_step + j, 0, n, d, True)
                process(jobs, d, mode)
                return carry

            lax.fori_loop(0, d // per_step, step, 0)
            return

        def residue(r, carry):
            group(r, 0, n, d, mode, True)

            def step(i, c):
                group(r, i * n, n, d, mode, False)
                return c

            return lax.fori_loop(1, n_chunks // n, step, carry)

        if d == 1:
            residue(0, 0)
        else:
            lax.fori_loop(0, d, residue, 0)

    modes = ("init",) + ("merge",) * (len(A_DILATIONS) - 2) + ("final",)
    for d, mode in zip(A_DILATIONS, modes):
        run_pass(d, mode)


def _by_residue_shape(batch, seq, d):
    if d == 1:
        return (A_HEADS, batch, seq, A_HEAD_DIM)
    return (A_HEADS, batch, d, seq // d, A_HEAD_DIM)


def _attn(qkv, batch, seq):
    assert seq % (A_BAND * max(A_DILATIONS)) == 0

    def spec(d):
        shape = _by_residue_shape(batch, seq, d)[2:]
        return pl.BlockSpec((None, None) + shape, lambda b, h: (h, b) + (0,) * len(shape))

    return pl.pallas_call(
        functools.partial(_attn_kernel, seq=seq),
        grid=(batch, A_HEADS),
        in_specs=[spec(d) for d in A_DILATIONS] * 3,
        out_specs=spec(1),
        out_shape=jax.ShapeDtypeStruct(_by_residue_shape(batch, seq, 1), BF16),
        scratch_shapes=[pltpu.VMEM((seq, A_HEAD_DIM), F32)] * 3,
        compiler_params=_params("parallel", "parallel"),
        name="dilated_attn",
    )(*qkv)


def _gelu_tanh(x):
    c1 = -2.0 * 0.7978845608028654 * LOG2_E
    return x / (1.0 + jnp.exp2(x * (c1 + (c1 * 0.044715) * (x * x))))


def _rglru_out_kernel(x_ref, oa_ref, xr_ref, gate_ref, cw_ref, cb_ref, wg_ref, gab_ref, gxb_ref,
                      lam_ref, wout_ref, o_ref, xpad_ref, carry_ref):
    halo = SUBLANES
    group = (SUBLANES, RG_BLOCK_DIM)

    @pl.when(pl.program_id(1) == 0)
    def _():
        xpad_ref[0:halo, :] = jnp.zeros((halo, RG_WIDTH), F32)
        carry_ref[...] = jnp.zeros((SUBLANES, RG_WIDTH), F32)

    xpad_ref[halo:halo + RG_TILE, :] = xr_ref[...]
    w_out = wout_ref[...].astype(BF16)
    wg = [wg_ref[g].astype(BF16) for g in range(RG_BLOCKS)]
    lam = -lam_ref[...]
    softplus = jnp.maximum(lam, 0.0) + jnp.log1p(jnp.exp(-jnp.abs(lam)))
    grouped = (RG_SUB // SUBLANES, SUBLANES, RG_BLOCK_DIM)
    row = lax.broadcasted_iota(jnp.int32, grouped, 1)
    carry = [carry_ref[:, g * RG_BLOCK_DIM:(g + 1) * RG_BLOCK_DIM] for g in range(RG_BLOCKS)]

    for r0 in range(0, RG_TILE, RG_SUB):
        rs = slice(r0, r0 + RG_SUB)
        xc = cb_ref[...] + cw_ref[CONV_WIDTH - 1:CONV_WIDTH, :] * xr_ref[rs, :]
        for j in range(1, CONV_WIDTH):
            xc = xc + (cw_ref[CONV_WIDTH - 1 - j:CONV_WIDTH - j, :]
                       * xpad_ref[pl.ds(halo - j + r0, RG_SUB), :])
        mixed = [oa_ref[hd, rs, :] for hd in range(A_HEADS)]
        for g in range(RG_BLOCKS):
            sl = slice(g * RG_BLOCK_DIM, (g + 1) * RG_BLOCK_DIM)
            xg = xc[:, sl]
            rg = _dot(xg.astype(BF16), wg[g])
            r = jax.nn.sigmoid(rg[:, :RG_BLOCK_DIM] + gab_ref[:, sl])
            i = jax.nn.sigmoid(rg[:, RG_BLOCK_DIM:] + gxb_ref[:, sl])
            log_a = (-RG_C) * r * softplus[:, sl]
            a = jnp.exp(log_a)
            th = jnp.tanh(log_a)
            u = lax.rsqrt((th - 1.0) / (2.0 * th)) * (i * xg)
            a = a.reshape(grouped)
            u = u.reshape(grouped)
            k = 1
            while k < SUBLANES:
                valid = row >= k
                a_prev = pltpu.roll(a, k, 1)
                u_prev = pltpu.roll(u, k, 1)
                u = jnp.where(valid, a * u_prev + u, u)
                a = jnp.where(valid, a * a_prev, a)
                k *= 2
            a = a.reshape(RG_SUB, RG_BLOCK_DIM)
            u = u.reshape(RG_SUB, RG_BLOCK_DIM)
            hs = []
            c = carry[g]
            for gi in range(0, RG_SUB, SUBLANES):
                h = u[gi:gi + SUBLANES, :] + a[gi:gi + SUBLANES, :] * c
                hs.append(h)
                c = jnp.broadcast_to(h[SUBLANES - 1:SUBLANES, :], group)
            carry[g] = c
            mixed.append((jnp.concatenate(hs, axis=0) * _gelu_tanh(gate_ref[rs, sl])).astype(BF16))
        o_ref[rs, :] = x_ref[rs, :] + _dot(jnp.concatenate(mixed, axis=-1), w_out)

    for g in range(RG_BLOCKS):
        carry_ref[:, g * RG_BLOCK_DIM:(g + 1) * RG_BLOCK_DIM] = carry[g]
    xpad_ref[0:halo, :] = xr_ref[RG_TILE - halo:RG_TILE, :]


def _rglru_out(x, o_a, xr, gate, e, conv_w, conv_b, w_gates, ga_b, gx_b, lam, w_out):
    batch, seq, _ = x.shape
    full = pl.BlockSpec((None, RG_TILE, D_MODEL), lambda b, t: (b, t, 0))
    half = pl.BlockSpec((None, RG_TILE, RG_WIDTH), lambda b, t: (b, t, 0))
    vec = pl.BlockSpec((None, 1, RG_WIDTH), lambda b, t: (e, 0, 0))
    return pl.pallas_call(
        _rglru_out_kernel,
        grid=(batch, seq // RG_TILE),
        in_specs=[
            full,
            pl.BlockSpec((A_HEADS, None, RG_TILE, A_HEAD_DIM), lambda b, t: (0, b, t, 0)),
            half, half,
            pl.BlockSpec((None, CONV_WIDTH, RG_WIDTH), lambda b, t: (e, 0, 0)),
            vec,
            pl.BlockSpec((None, RG_BLOCKS, RG_BLOCK_DIM, 2 * RG_BLOCK_DIM), lambda b, t: (e, 0, 0, 0)),
            vec, vec, vec,
            _resident((None, D_MODEL, D_MODEL), lambda b, t: (e, 0, 0)),
        ],
        out_specs=full,
        out_shape=jax.ShapeDtypeStruct(x.shape, F32),
        scratch_shapes=[
            pltpu.VMEM((RG_TILE + SUBLANES, RG_WIDTH), F32),
            pltpu.VMEM((SUBLANES, RG_WIDTH), F32),
        ],
        compiler_params=_params("arbitrary", "arbitrary"),
        name="rglru_outproj",
    )(x, o_a, xr, gate, conv_w, conv_b, w_gates, ga_b, gx_b, lam, w_out)


def kernel(x, mem, mem_norm_g, mix_norm_g, xattn_norm_g, mlp_norm_g, ev_w_in, ev_q_norm_g, ev_k_norm_g, ev_conv_w, ev_conv_b, ev_gate_a_w, ev_gate_a_b, ev_gate_x_w, ev_gate_x_b, ev_lambda, ev_w_out, od_pool_w, od_scale, xa_w_q, xa_w_kv, xa_q_norm_g, xa_k_norm_g, xa_w_o, mlp_w1, mlp_w2):
    batch, seq, _ = x.shape
    rows = batch * seq

    def vec3(a):
        return a.reshape(a.shape[0], 1, a.shape[1])

    w_in, w_out = ev_w_in, ev_w_out
    w_gates = jnp.concatenate([ev_gate_a_w, ev_gate_x_w], axis=-1)
    pool_w = od_pool_w
    w_q, w_kv, w_o = xa_w_q, xa_w_kv, xa_w_o
    w1, w2 = mlp_w1, mlp_w2
    mix_g, xa_g, mlp_g = vec3(mix_norm_g), vec3(xattn_norm_g), vec3(mlp_norm_g)
    ev_qg, ev_kg = vec3(ev_q_norm_g), vec3(ev_k_norm_g)
    xa_qg = vec3(xa_q_norm_g)
    conv_b, ga_b, gx_b, lam = vec3(ev_conv_b), vec3(ev_gate_a_b), vec3(ev_gate_x_b), vec3(ev_lambda)
    od_sc = vec3(od_scale)

    kt, v_mem = _memkv(mem, mem_norm_g, w_kv, xa_k_norm_g)

    for l in range(DEPTH):
        if l % 2 == 0:
            e = l // 2
            qkv, xr, gate = _inproj(x, l, e, mix_g, w_in, ev_qg, ev_kg)
            o_a = _attn(qkv, batch, seq)
            x = _rglru_out(x, o_a, xr, gate, e, ev_conv_w, conv_b, w_gates, ga_b, gx_b, lam, w_out)
            pool = None
        else:
            pool = (mix_g, pool_w, od_sc)
        x = _xattn(x, l, xa_g, w_q, xa_qg, kt, v_mem, w_o, pool)
        x = _mlp(x.reshape(rows, D_MODEL), l, mlp_g, w1, w2).reshape(batch, seq, D_MODEL)
    return x
```

```python
import functools

import jax
import jax.numpy as jnp
from jax import lax
from jax.experimental import pallas as pl
from jax.experimental.pallas import tpu as pltpu

D_MODEL = 1024
DEPTH = 4
N_MEM = 256
MIX_HALF = D_MODEL // 2
A_HEADS = 4
A_HEAD_DIM = MIX_HALF // A_HEADS
A_BAND = 128
A_DILATIONS = (16, 4, 1)
A_CHUNKS_PER_STEP = 8
RG_WIDTH = MIX_HALF
RG_BLOCKS = 4
RG_BLOCK_DIM = RG_WIDTH // RG_BLOCKS
RG_C = 8.0
CONV_WIDTH = 4
POOL_WINDOWS = (2, 4, 8, 16)
POOL_GROUP_DIM = D_MODEL // len(POOL_WINDOWS)
POOL_HALO = 16
XA_HEADS = 4
XA_HEAD_DIM = D_MODEL // XA_HEADS
D_FF = 4 * D_MODEL
IN_COLS = 3 * MIX_HALF + 2 * RG_WIDTH
EPS = 1e-6
LOG2_E = 1.4426950408889634

SUBLANES = 8
VMEM_LIMIT = 56 * 1024 * 1024

ROW_TILE = 1024
XA_SUB = 512
MLP_TILE = 512
MLP_CHUNK = 1024
RG_TILE = 512
RG_SUB = 256

BF16 = jnp.bfloat16
F32 = jnp.float32


def _rms(x, g):
    ms = jnp.mean(x * x, axis=-1, keepdims=True)
    return x * lax.rsqrt(ms + EPS) * g


def _dot(a, b):
    return jnp.dot(a, b, preferred_element_type=F32)


def _dot_nt(a, b):
    return lax.dot_general(a, b, (((1,), (1,)), ((), ())), preferred_element_type=F32)


def _resident(shape, index_map):
    return pl.BlockSpec(shape, index_map, pipeline_mode=pl.Buffered(1))


def _params(*sem):
    return pltpu.CompilerParams(dimension_semantics=sem, vmem_limit_bytes=VMEM_LIMIT)


def _memkv_kernel(mem_ref, mg_ref, wkv_ref, kg_ref, wo_ref, kt_ref, vw_ref):
    mem_n = _rms(mem_ref[...], mg_ref[...]).astype(BF16)
    kv = _dot(mem_n, wkv_ref[...].astype(BF16))
    scale = XA_HEAD_DIM ** -0.5 * LOG2_E
    for h in range(XA_HEADS):
        sl = slice(h * XA_HEAD_DIM, (h + 1) * XA_HEAD_DIM)
        kh = _rms(kv[:, sl], kg_ref[...]) * scale
        kt_ref[sl, :] = kh.T.astype(BF16)
        vh = kv[:, D_MODEL + h * XA_HEAD_DIM:D_MODEL + (h + 1) * XA_HEAD_DIM].astype(BF16)
        vw_ref[h] = _dot(vh, wo_ref[sl, :].astype(BF16)).astype(BF16)


def _memkv(mem, mem_norm_g, w_kv, k_g, w_o):
    batch = mem.shape[0]
    return pl.pallas_call(
        _memkv_kernel,
        grid=(DEPTH, batch),
        in_specs=[
            pl.BlockSpec((None, N_MEM, D_MODEL), lambda l, b: (b, 0, 0)),
            pl.BlockSpec((1, D_MODEL), lambda l, b: (0, 0)),
            pl.BlockSpec((None, D_MODEL, 2 * D_MODEL), lambda l, b: (l, 0, 0)),
            pl.BlockSpec((None, 1, XA_HEAD_DIM), lambda l, b: (l, 0, 0)),
            pl.BlockSpec((None, D_MODEL, D_MODEL), lambda l, b: (l, 0, 0)),
        ],
        out_specs=[
            pl.BlockSpec((None, None, D_MODEL, N_MEM), lambda l, b: (l, b, 0, 0)),
            pl.BlockSpec((None, None, XA_HEADS, N_MEM, D_MODEL), lambda l, b: (l, b, 0, 0, 0)),
        ],
        out_shape=[
            jax.ShapeDtypeStruct((DEPTH, batch, D_MODEL, N_MEM), BF16),
            jax.ShapeDtypeStruct((DEPTH, batch, XA_HEADS, N_MEM, D_MODEL), BF16),
        ],
        compiler_params=_params("arbitrary", "arbitrary"),
        name="memkv",
    )(mem, mem_norm_g.reshape(1, D_MODEL), w_kv, k_g.reshape(DEPTH, 1, XA_HEAD_DIM), w_o)


def _pool_rows(x, r0, pos0, mg_ref, pw_ref, sc_ref, hpad_ref):
    rows = x.shape[0]
    h = _rms(x, mg_ref[...])
    hpad_ref[POOL_HALO + r0:POOL_HALO + r0 + rows, :] = h
    pos = pos0 + lax.broadcasted_iota(jnp.int32, (rows, 1), 0)
    s = hpad_ref[r0:r0 + POOL_HALO + rows, :]
    outs = []
    width = 1
    for gi, w in enumerate(POOL_WINDOWS):
        while width < w:
            s = s + pltpu.roll(s, width, 0)
            width *= 2
        cnt = jnp.minimum(pos + 1, w).astype(F32)
        d = s[POOL_HALO:, :POOL_GROUP_DIM] / cnt - h[:, gi * POOL_GROUP_DIM:(gi + 1) * POOL_GROUP_DIM]
        outs.append(_dot(d.astype(BF16), pw_ref[gi].astype(BF16)))
        if gi + 1 < len(POOL_WINDOWS):
            s = s[:, POOL_GROUP_DIM:]
    return x + jnp.concatenate(outs, axis=-1) * sc_ref[...]


def _xattn_kernel(*refs, pool):
    if pool:
        x_ref, mg_ref, pw_ref, sc_ref, g_ref, wq_ref, qg_ref, kt_ref, vw_ref, o_ref, hpad_ref = refs
        t = pl.program_id(1)

        @pl.when(t == 0)
        def _():
            hpad_ref[0:POOL_HALO, :] = jnp.zeros((POOL_HALO, D_MODEL), F32)
    else:
        x_ref, g_ref, wq_ref, qg_ref, kt_ref, vw_ref, o_ref = refs
    w_q = wq_ref[...].astype(BF16)
    heads = [slice(hd * XA_HEAD_DIM, (hd + 1) * XA_HEAD_DIM) for hd in range(XA_HEADS)]
    for r0 in range(0, ROW_TILE, XA_SUB):
        rs = slice(r0, r0 + XA_SUB)
        x = x_ref[rs, :]
        if pool:
            x = _pool_rows(x, r0, t * ROW_TILE + r0, mg_ref, pw_ref, sc_ref, hpad_ref)
        h = _rms(x, g_ref[...]).astype(BF16)
        qs = [_dot(h, w_q[:, sl]) for sl in heads]
        qs = [_rms(q, qg_ref[...]).astype(BF16) for q in qs]
        ss = [_dot(q, kt_ref[sl, :]) for q, sl in zip(qs, heads)]
        ps = [jnp.exp2(s - jnp.max(s, axis=-1, keepdims=True)) for s in ss]
        ps = [(p / jnp.sum(p, axis=-1, keepdims=True)).astype(BF16) for p in ps]
        out = x
        for hd, p in enumerate(ps):
            out = out + _dot(p, vw_ref[hd])
        o_ref[rs, :] = out
    if pool:
        hpad_ref[0:POOL_HALO, :] = hpad_ref[ROW_TILE:ROW_TILE + POOL_HALO, :]


def _xattn(x, layer, g, w_q, q_g, kt, vw, pool=None):
    batch, seq, _ = x.shape
    tile = pl.BlockSpec((None, ROW_TILE, D_MODEL), lambda b, t: (b, t, 0))
    in_specs = [
        pl.BlockSpec((None, 1, D_MODEL), lambda b, t: (layer, 0, 0)),
        _resident((None, D_MODEL, D_MODEL), lambda b, t: (layer, 0, 0)),
        pl.BlockSpec((None, 1, XA_HEAD_DIM), lambda b, t: (layer, 0, 0)),
        pl.BlockSpec((None, None, D_MODEL, N_MEM), lambda b, t: (layer, b, 0, 0)),
        pl.BlockSpec((None, None, XA_HEADS, N_MEM, D_MODEL), lambda b, t: (layer, b, 0, 0, 0)),
    ]
    args = (g, w_q, q_g, kt, vw)
    scratch = []
    if pool is not None:
        o = layer // 2
        in_specs = [
            pl.BlockSpec((None, 1, D_MODEL), lambda b, t: (layer, 0, 0)),
            pl.BlockSpec((None, len(POOL_WINDOWS), POOL_GROUP_DIM, POOL_GROUP_DIM),
                         lambda b, t: (o, 0, 0, 0)),
            pl.BlockSpec((None, 1, D_MODEL), lambda b, t: (o, 0, 0)),
        ] + in_specs
        args = tuple(pool) + args
        scratch = [pltpu.VMEM((POOL_HALO + ROW_TILE, D_MODEL), F32)]
    return pl.pallas_call(
        functools.partial(_xattn_kernel, pool=pool is not None),
        grid=(batch, seq // ROW_TILE),
        in_specs=[tile] + in_specs,
        out_specs=tile,
        out_shape=jax.ShapeDtypeStruct(x.shape, F32),
        scratch_shapes=scratch,
        compiler_params=_params("parallel", "arbitrary" if pool is not None else "parallel"),
        name="xattn" if pool is None else "pool_xattn",
    )(x, *args)


def _mlp_kernel(x_ref, g_ref, w1_hbm, w2_hbm, o_ref, w1_ref, w2_ref, sems, *, layer):
    chunks = [slice(c * MLP_CHUNK, (c + 1) * MLP_CHUNK) for c in range(D_FF // MLP_CHUNK)]

    def copies(c):
        sl = chunks[c]
        return (pltpu.make_async_copy(w1_hbm.at[layer, :, sl], w1_ref.at[:, sl], sems.at[0, c]),
                pltpu.make_async_copy(w2_hbm.at[layer, sl, :], w2_ref.at[sl, :], sems.at[1, c]))

    def body(first):
        if first:
            for c in range(len(chunks)):
                for cp in copies(c):
                    cp.start()
        x = x_ref[...]
        h = _rms(x, g_ref[...]).astype(BF16)
        acc = x
        for c, sl in enumerate(chunks):
            if first:
                for cp in copies(c):
                    cp.wait()
            a = jnp.maximum(_dot(h, w1_ref[:, sl].astype(BF16)), 0.0)
            acc = acc + _dot((a * a).astype(BF16), w2_ref[sl, :].astype(BF16))
        o_ref[...] = acc

    first_step = pl.program_id(0) == 0
    pl.when(first_step)(functools.partial(body, True))
    pl.when(jnp.logical_not(first_step))(functools.partial(body, False))


def _mlp(x2, layer, g, w1, w2):
    rows = x2.shape[0]
    return pl.pallas_call(
        functools.partial(_mlp_kernel, layer=layer),
        grid=(rows // MLP_TILE,),
        in_specs=[
            pl.BlockSpec((MLP_TILE, D_MODEL), lambda i: (i, 0)),
            pl.BlockSpec((None, 1, D_MODEL), lambda i: (layer, 0, 0)),
            pl.BlockSpec(memory_space=pl.ANY),
            pl.BlockSpec(memory_space=pl.ANY),
        ],
        out_specs=pl.BlockSpec((MLP_TILE, D_MODEL), lambda i: (i, 0)),
        out_shape=jax.ShapeDtypeStruct(x2.shape, F32),
        scratch_shapes=[
            pltpu.VMEM((D_MODEL, D_FF), F32),
            pltpu.VMEM((D_FF, D_MODEL), F32),
            pltpu.SemaphoreType.DMA((2, D_FF // MLP_CHUNK)),
        ],
        compiler_params=_params("arbitrary"),
        name="mlp",
    )(x2, g, w1, w2)


def _inproj_kernel(x_ref, g_ref, w_ref, qg_ref, kg_ref, q_ref, k_ref, v_ref, xr_ref, gate_ref):
    h = _rms(x_ref[...], g_ref[...]).astype(BF16)
    scale = A_HEAD_DIM ** -0.5 * LOG2_E
    for j, (out_ref, gain_ref, mult) in enumerate(
            ((q_ref, qg_ref, scale), (k_ref, kg_ref, 1.0), (v_ref, None, None),
             (xr_ref, None, None), (gate_ref, None, None))):
        p = _dot(h, w_ref[:, j * MIX_HALF:(j + 1) * MIX_HALF].astype(BF16))
        if j >= 3:
            out_ref[...] = p
            continue
        for hd in range(A_HEADS):
            ph = p[:, hd * A_HEAD_DIM:(hd + 1) * A_HEAD_DIM]
            out_ref[hd] = ph if gain_ref is None else _rms(ph, gain_ref[...]) * mult


def _inproj(x2, layer, e, g, w_in, q_g, k_g):
    rows = x2.shape[0]
    half = jax.ShapeDtypeStruct((rows, MIX_HALF), F32)
    half_spec = pl.BlockSpec((ROW_TILE, MIX_HALF), lambda i: (i, 0))
    heads = jax.ShapeDtypeStruct((A_HEADS, rows, A_HEAD_DIM), F32)
    heads_spec = pl.BlockSpec((A_HEADS, ROW_TILE, A_HEAD_DIM), lambda i: (0, i, 0))
    gain_spec = pl.BlockSpec((None, 1, A_HEAD_DIM), lambda i: (e, 0, 0))
    return pl.pallas_call(
        _inproj_kernel,
        grid=(rows // ROW_TILE,),
        in_specs=[
            pl.BlockSpec((ROW_TILE, D_MODEL), lambda i: (i, 0)),
            pl.BlockSpec((None, 1, D_MODEL), lambda i: (layer, 0, 0)),
            _resident((None, D_MODEL, IN_COLS), lambda i: (e, 0, 0)),
            gain_spec, gain_spec,
        ],
        out_specs=[heads_spec] * 3 + [half_spec] * 2,
        out_shape=[heads] * 3 + [half] * 2,
        compiler_params=_params("parallel"),
        name="inproj",
    )(x2, g, w_in, q_g, k_g)


def _attn_kernel(q_ref, k_ref, v_ref, o_ref, m_ref, l_ref, acc_ref, *, seq):
    jq = lax.broadcasted_iota(jnp.int32, (A_BAND, A_BAND), 0)
    jk = lax.broadcasted_iota(jnp.int32, (A_BAND, A_BAND), 1)
    own_mask = jk <= jq
    jk2 = lax.broadcasted_iota(jnp.int32, (A_BAND, 2 * A_BAND), 1)
    jq2 = lax.broadcasted_iota(jnp.int32, (A_BAND, 2 * A_BAND), 0)
    both_mask = (jk2 >= jq2) & (jk2 <= jq2 + A_BAND)
    neg = -jnp.inf
    lanes = (A_BAND, A_HEAD_DIM)
    ones = jnp.ones(lanes, BF16)

    def rows(start, d):
        if d == 1:
            return pl.ds(pl.multiple_of(start, A_BAND), A_BAND)
        return pl.ds(start, A_BAND, stride=d)

    def load_kv(start, d):
        r = rows(start, d)
        return (k_ref[r, :].astype(BF16),
                jnp.concatenate([v_ref[r, :].astype(BF16), ones], axis=1))

    def process(jobs, d, mode):
        rs = [rows(start, d) for start, _, _ in jobs]
        qs = [q_ref[r, :].astype(BF16) for r in rs]
        scores = []
        for q, (_, own, prev) in zip(qs, jobs):
            if prev is None:
                s = jnp.where(own_mask, _dot_nt(q, own[0]), neg)
            else:
                s = _dot_nt(q, jnp.concatenate([prev[0], own[0]], axis=0))
                s = jnp.where(both_mask, s, neg)
            scores.append(s)
        ms = [jnp.broadcast_to(jnp.max(s, axis=-1, keepdims=True), lanes) for s in scores]
        if mode != "init":
            m_olds = [m_ref[r, :] for r in rs]
            ms = [jnp.maximum(m_old, m) for m_old, m in zip(m_olds, ms)]
        pvs = []
        for s, m, (_, own, prev) in zip(scores, ms, jobs):
            if prev is None:
                v = own[1]
            else:
                v = jnp.concatenate([prev[1], own[1]], axis=0)
                m = jnp.concatenate([m, m], axis=1)
            pvs.append(_dot(jnp.exp2(s - m).astype(BF16), v))
        for i, (r, m, pv) in enumerate(zip(rs, ms, pvs)):
            acc, den = pv[:, :A_HEAD_DIM], pv[:, A_HEAD_DIM:]
            if mode != "init":
                a_old = jnp.exp2(m_olds[i] - m)
                den = a_old * l_ref[r, :] + den
                acc = a_old * acc_ref[r, :] + acc
            if mode == "final":
                o_ref[r, :] = (acc / den).astype(o_ref.dtype)
            else:
                m_ref[r, :] = m
                l_ref[r, :] = den
                acc_ref[r, :] = acc

    def jobs_of(r, c0, n, d, first):
        pitch = A_BAND * d
        kv = [None if first else load_kv(r + (c0 - 1) * pitch, d)]
        kv += [load_kv(r + (c0 + j) * pitch, d) for j in range(n)]
        return [(r + (c0 + j) * pitch, kv[j + 1], kv[j]) for j in range(n)]

    def group(r, c0, n, d, mode, first):
        process(jobs_of(r, c0, n, d, first), d, mode)

    def run_pass(d, mode):
        n_chunks = seq // (A_BAND * d)
        n = min(n_chunks, A_CHUNKS_PER_STEP)
        if n_chunks == n:
            per_step = A_CHUNKS_PER_STEP // n

            def step(i, carry):
                jobs = []
                for j in range(per_step):
                    jobs += jobs_of(i * per_step + j, 0, n, d, True)
                process(jobs, d, mode)
                return carry

            lax.fori_loop(0, d // per_step, step, 0)
            return

        def residue(r, carry):
            group(r, 0, n, d, mode, True)

            def step(i, c):
                group(r, i * n, n, d, mode, False)
                return c

            return lax.fori_loop(1, n_chunks // n, step, carry)

        if d == 1:
            residue(0, 0)
        else:
            lax.fori_loop(0, d, residue, 0)

    modes = ("init",) + ("merge",) * (len(A_DILATIONS) - 2) + ("final",)
    for d, mode in zip(A_DILATIONS, modes):
        run_pass(d, mode)


def _attn(q, k, v):
    _, batch, seq, _ = q.shape
    assert seq % (A_BAND * max(A_DILATIONS)) == 0
    spec = pl.BlockSpec((None, None, seq, A_HEAD_DIM), lambda b, h: (h, b, 0, 0))
    return pl.pallas_call(
        functools.partial(_attn_kernel, seq=seq),
        grid=(batch, A_HEADS),
        in_specs=[spec, spec, spec],
        out_specs=spec,
        out_shape=jax.ShapeDtypeStruct(q.shape, BF16),
        scratch_shapes=[pltpu.VMEM((seq, A_HEAD_DIM), F32)] * 3,
        compiler_params=_params("parallel", "parallel"),
        name="dilated_attn",
    )(q, k, v)


def _gelu_tanh(x):
    c1 = -2.0 * 0.7978845608028654 * LOG2_E
    return x / (1.0 + jnp.exp2(x * (c1 + (c1 * 0.044715) * (x * x))))


def _rglru_out_kernel(x_ref, oa_ref, xr_ref, gate_ref, cw_ref, cb_ref, wg_ref, gab_ref, gxb_ref,
                      lam_ref, wout_ref, o_ref, xpad_ref, carry_ref):
    halo = SUBLANES
    group = (SUBLANES, RG_BLOCK_DIM)

    @pl.when(pl.program_id(1) == 0)
    def _():
        xpad_ref[0:halo, :] = jnp.zeros((halo, RG_WIDTH), F32)
        carry_ref[...] = jnp.zeros((SUBLANES, RG_WIDTH), F32)

    xpad_ref[halo:halo + RG_TILE, :] = xr_ref[...]
    w_out = wout_ref[...].astype(BF16)
    wg = [wg_ref[g].astype(BF16) for g in range(RG_BLOCKS)]
    lam = -lam_ref[...]
    softplus = jnp.maximum(lam, 0.0) + jnp.log1p(jnp.exp(-jnp.abs(lam)))
    grouped = (RG_SUB // SUBLANES, SUBLANES, RG_BLOCK_DIM)
    row = lax.broadcasted_iota(jnp.int32, grouped, 1)
    carry = [carry_ref[:, g * RG_BLOCK_DIM:(g + 1) * RG_BLOCK_DIM] for g in range(RG_BLOCKS)]

    for r0 in range(0, RG_TILE, RG_SUB):
        rs = slice(r0, r0 + RG_SUB)
        xc = cb_ref[...] + cw_ref[CONV_WIDTH - 1:CONV_WIDTH, :] * xr_ref[rs, :]
        for j in range(1, CONV_WIDTH):
            xc = xc + (cw_ref[CONV_WIDTH - 1 - j:CONV_WIDTH - j, :]
                       * xpad_ref[pl.ds(halo - j + r0, RG_SUB), :])
        mixed = [oa_ref[hd, rs, :] for hd in range(A_HEADS)]
        for g in range(RG_BLOCKS):
            sl = slice(g * RG_BLOCK_DIM, (g + 1) * RG_BLOCK_DIM)
            xg = xc[:, sl]
            rg = _dot(xg.astype(BF16), wg[g])
            r = jax.nn.sigmoid(rg[:, :RG_BLOCK_DIM] + gab_ref[:, sl])
            i = jax.nn.sigmoid(rg[:, RG_BLOCK_DIM:] + gxb_ref[:, sl])
            log_a = (-RG_C) * r * softplus[:, sl]
            a = jnp.exp(log_a)
            th = jnp.tanh(log_a)
            u = lax.rsqrt((th - 1.0) / (2.0 * th)) * (i * xg)
            a = a.reshape(grouped)
            u = u.reshape(grouped)
            k = 1
            while k < SUBLANES:
                valid = row >= k
                a_prev = pltpu.roll(a, k, 1)
                u_prev = pltpu.roll(u, k, 1)
                u = jnp.where(valid, a * u_prev + u, u)
                a = jnp.where(valid, a * a_prev, a)
                k *= 2
            a = a.reshape(RG_SUB, RG_BLOCK_DIM)
            u = u.reshape(RG_SUB, RG_BLOCK_DIM)
            hs = []
            c = carry[g]
            for gi in range(0, RG_SUB, SUBLANES):
                h = u[gi:gi + SUBLANES, :] + a[gi:gi + SUBLANES, :] * c
                hs.append(h)
                c = jnp.broadcast_to(h[SUBLANES - 1:SUBLANES, :], group)
            carry[g] = c
            mixed.append((jnp.concatenate(hs, axis=0) * _gelu_tanh(gate_ref[rs, sl])).astype(BF16))
        o_ref[rs, :] = x_ref[rs, :] + _dot(jnp.concatenate(mixed, axis=-1), w_out)

    for g in range(RG_BLOCKS):
        carry_ref[:, g * RG_BLOCK_DIM:(g + 1) * RG_BLOCK_DIM] = carry[g]
    xpad_ref[0:halo, :] = xr_ref[RG_TILE - halo:RG_TILE, :]


def _rglru_out(x, o_a, xr, gate, e, conv_w, conv_b, w_gates, ga_b, gx_b, lam, w_out):
    batch, seq, _ = x.shape
    full = pl.BlockSpec((None, RG_TILE, D_MODEL), lambda b, t: (b, t, 0))
    half = pl.BlockSpec((None, RG_TILE, RG_WIDTH), lambda b, t: (b, t, 0))
    vec = pl.BlockSpec((None, 1, RG_WIDTH), lambda b, t: (e, 0, 0))
    return pl.pallas_call(
        _rglru_out_kernel,
        grid=(batch, seq // RG_TILE),
        in_specs=[
            full,
            pl.BlockSpec((A_HEADS, None, RG_TILE, A_HEAD_DIM), lambda b, t: (0, b, t, 0)),
            half, half,
            pl.BlockSpec((None, CONV_WIDTH, RG_WIDTH), lambda b, t: (e, 0, 0)),
            vec,
            pl.BlockSpec((None, RG_BLOCKS, RG_BLOCK_DIM, 2 * RG_BLOCK_DIM), lambda b, t: (e, 0, 0, 0)),
            vec, vec, vec,
            _resident((None, D_MODEL, D_MODEL), lambda b, t: (e, 0, 0)),
        ],
        out_specs=full,
        out_shape=jax.ShapeDtypeStruct(x.shape, F32),
        scratch_shapes=[
            pltpu.VMEM((RG_TILE + SUBLANES, RG_WIDTH), F32),
            pltpu.VMEM((SUBLANES, RG_WIDTH), F32),
        ],
        compiler_params=_params("arbitrary", "arbitrary"),
        name="rglru_outproj",
    )(x, o_a, xr, gate, conv_w, conv_b, w_gates, ga_b, gx_b, lam, w_out)


def kernel(x, mem, mem_norm_g, mix_norm_g, xattn_norm_g, mlp_norm_g, ev_w_in, ev_q_norm_g, ev_k_norm_g, ev_conv_w, ev_conv_b, ev_gate_a_w, ev_gate_a_b, ev_gate_x_w, ev_gate_x_b, ev_lambda, ev_w_out, od_pool_w, od_scale, xa_w_q, xa_w_kv, xa_q_norm_g, xa_k_norm_g, xa_w_o, mlp_w1, mlp_w2):
    batch, seq, _ = x.shape
    rows = batch * seq

    def vec3(a):
        return a.reshape(a.shape[0], 1, a.shape[1])

    w_in, w_out = ev_w_in, ev_w_out
    w_gates = jnp.concatenate([ev_gate_a_w, ev_gate_x_w], axis=-1)
    pool_w = od_pool_w
    w_q, w_kv, w_o = xa_w_q, xa_w_kv, xa_w_o
    w1, w2 = mlp_w1, mlp_w2
    mix_g, xa_g, mlp_g = vec3(mix_norm_g), vec3(xattn_norm_g), vec3(mlp_norm_g)
    ev_qg, ev_kg = vec3(ev_q_norm_g), vec3(ev_k_norm_g)
    xa_qg = vec3(xa_q_norm_g)
    conv_b, ga_b, gx_b, lam = vec3(ev_conv_b), vec3(ev_gate_a_b), vec3(ev_gate_x_b), vec3(ev_lambda)
    od_sc = vec3(od_scale)

    kt, vw_mem = _memkv(mem, mem_norm_g, w_kv, xa_k_norm_g, w_o)

    for l in range(DEPTH):
        if l % 2 == 0:
            e = l // 2
            q, k, v, xr, gate = _inproj(x.reshape(rows, D_MODEL), l, e, mix_g, w_in, ev_qg, ev_kg)
            half = (batch, seq, MIX_HALF)
            heads = (A_HEADS, batch, seq, A_HEAD_DIM)
            o_a = _attn(q.reshape(heads), k.reshape(heads), v.reshape(heads))
            x = _rglru_out(x, o_a, xr.reshape(half), gate.reshape(half), e, ev_conv_w, conv_b,
                           w_gates, ga_b, gx_b, lam, w_out)
            pool = None
        else:
            pool = (mix_g, pool_w, od_sc)
        x = _xattn(x, l, xa_g, w_q, xa_qg, kt, vw_mem, pool)
        x = _mlp(x.reshape(rows, D_MODEL), l, mlp_g, w1, w2).reshape(batch, seq, D_MODEL)
    return x
```

```python
import functools

import jax
import jax.numpy as jnp
from jax import lax
from jax.experimental import pallas as pl
from jax.experimental.pallas import tpu as pltpu

D_MODEL = 1024
DEPTH = 4
N_MEM = 256
MIX_HALF = D_MODEL // 2
A_HEADS = 4
A_HEAD_DIM = MIX_HALF // A_HEADS
A_BAND = 128
A_DILATIONS = (16, 4, 1)
A_CHUNKS_PER_STEP = 8
RG_WIDTH = MIX_HALF
RG_BLOCKS = 4
RG_BLOCK_DIM = RG_WIDTH // RG_BLOCKS
RG_C = 8.0
CONV_WIDTH = 4
POOL_WINDOWS = (2, 4, 8, 16)
POOL_GROUP_DIM = D_MODEL // len(POOL_WINDOWS)
POOL_HALO = 16
XA_HEADS = 4
XA_HEAD_DIM = D_MODEL // XA_HEADS
D_FF = 4 * D_MODEL
IN_COLS = 3 * MIX_HALF + 2 * RG_WIDTH
EPS = 1e-6
LOG2_E = 1.4426950408889634

SUBLANES = 8
VMEM_LIMIT = 56 * 1024 * 1024

ROW_TILE = 1024
XA_SUB = 512
MLP_TILE = 512
MLP_CHUNK = 1024
RG_TILE = 1024
RG_SUB = 256

BF16 = jnp.bfloat16
F32 = jnp.float32


def _rms(x, g):
    ms = jnp.mean(x * x, axis=-1, keepdims=True)
    return x * lax.rsqrt(ms + EPS) * g


def _dot(a, b):
    return jnp.dot(a, b, preferred_element_type=F32)


def _dot_nt(a, b):
    return lax.dot_general(a, b, (((1,), (1,)), ((), ())), preferred_element_type=F32)


def _resident(shape, index_map):
    return pl.BlockSpec(shape, index_map, pipeline_mode=pl.Buffered(1))


def _params(*sem):
    return pltpu.CompilerParams(dimension_semantics=sem, vmem_limit_bytes=VMEM_LIMIT)


def _memkv_kernel(mem_ref, mg_ref, wkv_ref, kg_ref, kt_ref, v_ref):
    batch = mem_ref.shape[0]
    mem = mem_ref[...].reshape(batch * N_MEM, D_MODEL)
    mem_n = _rms(mem, mg_ref[...]).astype(BF16)
    kv = _dot(mem_n, wkv_ref[...].astype(BF16))
    scale = XA_HEAD_DIM ** -0.5 * LOG2_E
    for h in range(XA_HEADS):
        sl = slice(h * XA_HEAD_DIM, (h + 1) * XA_HEAD_DIM)
        kh = _rms(kv[:, sl], kg_ref[...]) * scale
        for b in range(batch):
            kt_ref[b, sl, :] = kh[b * N_MEM:(b + 1) * N_MEM, :].T.astype(BF16)
    v_ref[...] = kv[:, D_MODEL:].astype(BF16).reshape(batch, N_MEM, D_MODEL)


def _memkv(mem, mem_norm_g, w_kv, k_g):
    batch = mem.shape[0]
    return pl.pallas_call(
        _memkv_kernel,
        grid=(DEPTH,),
        in_specs=[
            pl.BlockSpec((batch, N_MEM, D_MODEL), lambda l: (0, 0, 0)),
            pl.BlockSpec((1, D_MODEL), lambda l: (0, 0)),
            pl.BlockSpec((None, D_MODEL, 2 * D_MODEL), lambda l: (l, 0, 0)),
            pl.BlockSpec((None, 1, XA_HEAD_DIM), lambda l: (l, 0, 0)),
        ],
        out_specs=[
            pl.BlockSpec((None, batch, D_MODEL, N_MEM), lambda l: (l, 0, 0, 0)),
            pl.BlockSpec((None, batch, N_MEM, D_MODEL), lambda l: (l, 0, 0, 0)),
        ],
        out_shape=[
            jax.ShapeDtypeStruct((DEPTH, batch, D_MODEL, N_MEM), BF16),
            jax.ShapeDtypeStruct((DEPTH, batch, N_MEM, D_MODEL), BF16),
        ],
        compiler_params=_params("arbitrary"),
        name="memkv",
    )(mem, mem_norm_g.reshape(1, D_MODEL), w_kv, k_g.reshape(DEPTH, 1, XA_HEAD_DIM))


def _pool_rows(x, r0, pos0, mg_ref, pw_ref, sc_ref, hpad_ref):
    rows = x.shape[0]
    h = _rms(x, mg_ref[...])
    hpad_ref[POOL_HALO + r0:POOL_HALO + r0 + rows, :] = h
    pos = pos0 + lax.broadcasted_iota(jnp.int32, (rows, 1), 0)
    s = hpad_ref[r0:r0 + POOL_HALO + rows, :]
    outs = []
    width = 1
    for gi, w in enumerate(POOL_WINDOWS):
        while width < w:
            s = s + pltpu.roll(s, width, 0)
            width *= 2
        cnt = jnp.minimum(pos + 1, w).astype(F32)
        d = s[POOL_HALO:, :POOL_GROUP_DIM] / cnt - h[:, gi * POOL_GROUP_DIM:(gi + 1) * POOL_GROUP_DIM]
        outs.append(_dot(d.astype(BF16), pw_ref[gi].astype(BF16)))
        if gi + 1 < len(POOL_WINDOWS):
            s = s[:, POOL_GROUP_DIM:]
    return x + jnp.concatenate(outs, axis=-1) * sc_ref[...]


def _xattn_kernel(*refs, pool):
    if pool:
        x_ref, mg_ref, pw_ref, sc_ref, g_ref, wq_ref, qg_ref, kt_ref, v_ref, wo_ref, o_ref, hpad_ref = refs
        t = pl.program_id(1)

        @pl.when(t == 0)
        def _():
            hpad_ref[0:POOL_HALO, :] = jnp.zeros((POOL_HALO, D_MODEL), F32)
    else:
        x_ref, g_ref, wq_ref, qg_ref, kt_ref, v_ref, wo_ref, o_ref = refs
    w_q = wq_ref[...].astype(BF16)
    w_o = wo_ref[...].astype(BF16)
    heads = [slice(hd * XA_HEAD_DIM, (hd + 1) * XA_HEAD_DIM) for hd in range(XA_HEADS)]
    for r0 in range(0, ROW_TILE, XA_SUB):
        rs = slice(r0, r0 + XA_SUB)
        x = x_ref[rs, :]
        if pool:
            x = _pool_rows(x, r0, t * ROW_TILE + r0, mg_ref, pw_ref, sc_ref, hpad_ref)
        h = _rms(x, g_ref[...]).astype(BF16)
        qs = [_dot(h, w_q[:, sl]) for sl in heads]
        qs = [_rms(q, qg_ref[...]).astype(BF16) for q in qs]
        ss = [_dot(q, kt_ref[sl, :]) for q, sl in zip(qs, heads)]
        ps = [jnp.exp2(s - jnp.max(s, axis=-1, keepdims=True)) for s in ss]
        dens = [jnp.sum(p, axis=-1, keepdims=True) for p in ps]
        os_ = [_dot(p.astype(BF16), v_ref[:, sl]) for p, sl in zip(ps, heads)]
        os_ = [(o / den).astype(BF16) for o, den in zip(os_, dens)]
        out = x
        for o, sl in zip(os_, heads):
            out = out + _dot(o, w_o[sl, :])
        o_ref[rs, :] = out
    if pool:
        hpad_ref[0:POOL_HALO, :] = hpad_ref[ROW_TILE:ROW_TILE + POOL_HALO, :]


def _xattn(x, layer, g, w_q, q_g, kt, v, w_o, pool=None):
    batch, seq, _ = x.shape
    tile = pl.BlockSpec((None, ROW_TILE, D_MODEL), lambda b, t: (b, t, 0))
    in_specs = [
        pl.BlockSpec((None, 1, D_MODEL), lambda b, t: (layer, 0, 0)),
        _resident((None, D_MODEL, D_MODEL), lambda b, t: (layer, 0, 0)),
        pl.BlockSpec((None, 1, XA_HEAD_DIM), lambda b, t: (layer, 0, 0)),
        pl.BlockSpec((None, None, D_MODEL, N_MEM), lambda b, t: (layer, b, 0, 0)),
        pl.BlockSpec((None, None, N_MEM, D_MODEL), lambda b, t: (layer, b, 0, 0)),
        _resident((None, D_MODEL, D_MODEL), lambda b, t: (layer, 0, 0)),
    ]
    args = (g, w_q, q_g, kt, v, w_o)
    scratch = []
    if pool is not None:
        o = layer // 2
        in_specs = [
            pl.BlockSpec((None, 1, D_MODEL), lambda b, t: (layer, 0, 0)),
            pl.BlockSpec((None, len(POOL_WINDOWS), POOL_GROUP_DIM, POOL_GROUP_DIM),
                         lambda b, t: (o, 0, 0, 0)),
            pl.BlockSpec((None, 1, D_MODEL), lambda b, t: (o, 0, 0)),
        ] + in_specs
        args = tuple(pool) + args
        scratch = [pltpu.VMEM((POOL_HALO + ROW_TILE, D_MODEL), F32)]
    return pl.pallas_call(
        functools.partial(_xattn_kernel, pool=pool is not None),
        grid=(batch, seq // ROW_TILE),
        in_specs=[tile] + in_specs,
        out_specs=tile,
        out_shape=jax.ShapeDtypeStruct(x.shape, F32),
        scratch_shapes=scratch,
        compiler_params=_params("parallel", "arbitrary" if pool is not None else "parallel"),
        name="xattn" if pool is None else "pool_xattn",
    )(x, *args)


def _mlp_kernel(x_ref, g_ref, w1_hbm, w2_hbm, o_ref, w1_ref, w2_ref, sems, *, layer):
    chunks = [slice(c * MLP_CHUNK, (c + 1) * MLP_CHUNK) for c in range(D_FF // MLP_CHUNK)]

    def copies(c):
        sl = chunks[c]
        return (pltpu.make_async_copy(w1_hbm.at[layer, :, sl], w1_ref.at[:, sl], sems.at[0, c]),
                pltpu.make_async_copy(w2_hbm.at[layer, sl, :], w2_ref.at[sl, :], sems.at[1, c]))

    def body(first):
        if first:
            for c in range(len(chunks)):
                for cp in copies(c):
                    cp.start()
        x = x_ref[...]
        h = _rms(x, g_ref[...]).astype(BF16)
        acc = x
        for c, sl in enumerate(chunks):
            if first:
                for cp in copies(c):
                    cp.wait()
            a = jnp.maximum(_dot(h, w1_ref[:, sl].astype(BF16)), 0.0)
            acc = acc + _dot((a * a).astype(BF16), w2_ref[sl, :].astype(BF16))
        o_ref[...] = acc

    first_step = pl.program_id(0) == 0
    pl.when(first_step)(functools.partial(body, True))
    pl.when(jnp.logical_not(first_step))(functools.partial(body, False))


def _mlp(x2, layer, g, w1, w2):
    rows = x2.shape[0]
    return pl.pallas_call(
        functools.partial(_mlp_kernel, layer=layer),
        grid=(rows // MLP_TILE,),
        in_specs=[
            pl.BlockSpec((MLP_TILE, D_MODEL), lambda i: (i, 0)),
            pl.BlockSpec((None, 1, D_MODEL), lambda i: (layer, 0, 0)),
            pl.BlockSpec(memory_space=pl.ANY),
            pl.BlockSpec(memory_space=pl.ANY),
        ],
        out_specs=pl.BlockSpec((MLP_TILE, D_MODEL), lambda i: (i, 0)),
        out_shape=jax.ShapeDtypeStruct(x2.shape, F32),
        scratch_shapes=[
            pltpu.VMEM((D_MODEL, D_FF), F32),
            pltpu.VMEM((D_FF, D_MODEL), F32),
            pltpu.SemaphoreType.DMA((2, D_FF // MLP_CHUNK)),
        ],
        compiler_params=_params("arbitrary"),
        name="mlp",
    )(x2, g, w1, w2)


def _inproj_kernel(x_ref, g_ref, w_ref, qg_ref, kg_ref, q_ref, k_ref, v_ref, xr_ref, gate_ref):
    h = _rms(x_ref[...], g_ref[...]).astype(BF16)
    scale = A_HEAD_DIM ** -0.5 * LOG2_E
    for j, (out_ref, gain_ref, mult) in enumerate(
            ((q_ref, qg_ref, scale), (k_ref, kg_ref, 1.0), (v_ref, None, None),
             (xr_ref, None, None), (gate_ref, None, None))):
        p = _dot(h, w_ref[:, j * MIX_HALF:(j + 1) * MIX_HALF].astype(BF16))
        if j >= 3:
            out_ref[...] = p
            continue
        for hd in range(A_HEADS):
            ph = p[:, hd * A_HEAD_DIM:(hd + 1) * A_HEAD_DIM]
            out_ref[hd] = ph if gain_ref is None else _rms(ph, gain_ref[...]) * mult


def _inproj(x2, layer, e, g, w_in, q_g, k_g):
    rows = x2.shape[0]
    half = jax.ShapeDtypeStruct((rows, MIX_HALF), F32)
    half_spec = pl.BlockSpec((ROW_TILE, MIX_HALF), lambda i: (i, 0))
    heads = jax.ShapeDtypeStruct((A_HEADS, rows, A_HEAD_DIM), F32)
    heads_spec = pl.BlockSpec((A_HEADS, ROW_TILE, A_HEAD_DIM), lambda i: (0, i, 0))
    gain_spec = pl.BlockSpec((None, 1, A_HEAD_DIM), lambda i: (e, 0, 0))
    return pl.pallas_call(
        _inproj_kernel,
        grid=(rows // ROW_TILE,),
        in_specs=[
            pl.BlockSpec((ROW_TILE, D_MODEL), lambda i: (i, 0)),
            pl.BlockSpec((None, 1, D_MODEL), lambda i: (layer, 0, 0)),
            _resident((None, D_MODEL, IN_COLS), lambda i: (e, 0, 0)),
            gain_spec, gain_spec,
        ],
        out_specs=[heads_spec] * 3 + [half_spec] * 2,
        out_shape=[heads] * 3 + [half] * 2,
        compiler_params=_params("parallel"),
        name="inproj",
    )(x2, g, w_in, q_g, k_g)


def _attn_kernel(q_ref, k_ref, v_ref, o_ref, m_ref, l_ref, acc_ref, *, seq):
    jq = lax.broadcasted_iota(jnp.int32, (A_BAND, A_BAND), 0)
    jk = lax.broadcasted_iota(jnp.int32, (A_BAND, A_BAND), 1)
    own_mask = jk <= jq
    jk2 = lax.broadcasted_iota(jnp.int32, (A_BAND, 2 * A_BAND), 1)
    jq2 = lax.broadcasted_iota(jnp.int32, (A_BAND, 2 * A_BAND), 0)
    both_mask = (jk2 >= jq2) & (jk2 <= jq2 + A_BAND)
    neg = -jnp.inf
    lanes = (A_BAND, A_HEAD_DIM)
    ones = jnp.ones(lanes, BF16)

    def rows(start, d):
        if d == 1:
            return pl.ds(pl.multiple_of(start, A_BAND), A_BAND)
        return pl.ds(start, A_BAND, stride=d)

    def load_kv(start, d):
        r = rows(start, d)
        return (k_ref[r, :].astype(BF16),
                jnp.concatenate([v_ref[r, :].astype(BF16), ones], axis=1))

    def process(jobs, d, mode):
        rs = [rows(start, d) for start, _, _ in jobs]
        qs = [q_ref[r, :].astype(BF16) for r in rs]
        scores = []
        for q, (_, own, prev) in zip(qs, jobs):
            if prev is None:
                s = jnp.where(own_mask, _dot_nt(q, own[0]), neg)
            else:
                s = _dot_nt(q, jnp.concatenate([prev[0], own[0]], axis=0))
                s = jnp.where(both_mask, s, neg)
            scores.append(s)
        ms = [jnp.broadcast_to(jnp.max(s, axis=-1, keepdims=True), lanes) for s in scores]
        if mode != "init":
            m_olds = [m_ref[r, :] for r in rs]
            ms = [jnp.maximum(m_old, m) for m_old, m in zip(m_olds, ms)]
        pvs = []
        for s, m, (_, own, prev) in zip(scores, ms, jobs):
            if prev is None:
                v = own[1]
            else:
                v = jnp.concatenate([prev[1], own[1]], axis=0)
                m = jnp.concatenate([m, m], axis=1)
            pvs.append(_dot(jnp.exp2(s - m).astype(BF16), v))
        for i, (r, m, pv) in enumerate(zip(rs, ms, pvs)):
            acc, den = pv[:, :A_HEAD_DIM], pv[:, A_HEAD_DIM:]
            if mode != "init":
                a_old = jnp.exp2(m_olds[i] - m)
                den = a_old * l_ref[r, :] + den
                acc = a_old * acc_ref[r, :] + acc
            if mode == "final":
                o_ref[r, :] = (acc / den).astype(o_ref.dtype)
            else:
                m_ref[r, :] = m
                l_ref[r, :] = den
                acc_ref[r, :] = acc

    def jobs_of(r, c0, n, d, first):
        pitch = A_BAND * d
        kv = [None if first else load_kv(r + (c0 - 1) * pitch, d)]
        kv += [load_kv(r + (c0 + j) * pitch, d) for j in range(n)]
        return [(r + (c0 + j) * pitch, kv[j + 1], kv[j]) for j in range(n)]

    def group(r, c0, n, d, mode, first):
        process(jobs_of(r, c0, n, d, first), d, mode)

    def run_pass(d, mode):
        n_chunks = seq // (A_BAND * d)
        n = min(n_chunks, A_CHUNKS_PER_STEP)
        if n_chunks == n:
            per_step = A_CHUNKS_PER_STEP // n

            def step(i, carry):
                jobs = []
                for j in range(per_step):
                    jobs += jobs_of(i * per_step + j, 0, n, d, True)
                process(jobs, d, mode)
                return carry

            lax.fori_loop(0, d // per_step, step, 0)
            return

        def residue(r, carry):
            group(r, 0, n, d, mode, True)

            def step(i, c):
                group(r, i * n, n, d, mode, False)
                return c

            return lax.fori_loop(1, n_chunks // n, step, carry)

        if d == 1:
            residue(0, 0)
        else:
            lax.fori_loop(0, d, residue, 0)

    modes = ("init",) + ("merge",) * (len(A_DILATIONS) - 2) + ("final",)
    for d, mode in zip(A_DILATIONS, modes):
        run_pass(d, mode)


def _attn(q, k, v):
    _, batch, seq, _ = q.shape
    assert seq % (A_BAND * max(A_DILATIONS)) == 0
    spec = pl.BlockSpec((None, None, seq, A_HEAD_DIM), lambda b, h: (h, b, 0, 0))
    return pl.pallas_call(
        functools.partial(_attn_kernel, seq=seq),
        grid=(batch, A_HEADS),
        in_specs=[spec, spec, spec],
        out_specs=spec,
        out_shape=jax.ShapeDtypeStruct(q.shape, BF16),
        scratch_shapes=[pltpu.VMEM((seq, A_HEAD_DIM), F32)] * 3,
        compiler_params=_params("parallel", "parallel"),
        name="dilated_attn",
    )(q, k, v)


def _gelu_tanh(x):
    c1 = -2.0 * 0.7978845608028654 * LOG2_E
    return x / (1.0 + jnp.exp2(x * (c1 + (c1 * 0.044715) * (x * x))))


def _rglru_out_kernel(x_ref, oa_ref, xr_ref, gate_ref, cw_ref, cb_ref, wg_ref, gab_ref, gxb_ref,
                      lam_ref, wout_ref, o_ref, xpad_ref, carry_ref):
    halo = SUBLANES
    group = (SUBLANES, RG_BLOCK_DIM)

    @pl.when(pl.program_id(1) == 0)
    def _():
        xpad_ref[0:halo, :] = jnp.zeros((halo, RG_WIDTH), F32)
        carry_ref[...] = jnp.zeros((SUBLANES, RG_WIDTH), F32)

    xpad_ref[halo:halo + RG_TILE, :] = xr_ref[...]
    w_out = wout_ref[...].astype(BF16)
    wg = [wg_ref[g].astype(BF16) for g in range(RG_BLOCKS)]
    lam = -lam_ref[...]
    softplus = jnp.maximum(lam, 0.0) + jnp.log1p(jnp.exp(-jnp.abs(lam)))
    grouped = (RG_SUB // SUBLANES, SUBLANES, RG_BLOCK_DIM)
    row = lax.broadcasted_iota(jnp.int32, grouped, 1)
    carry = [carry_ref[:, g * RG_BLOCK_DIM:(g + 1) * RG_BLOCK_DIM] for g in range(RG_BLOCKS)]

    for r0 in range(0, RG_TILE, RG_SUB):
        rs = slice(r0, r0 + RG_SUB)
        xc = cb_ref[...] + cw_ref[CONV_WIDTH - 1:CONV_WIDTH, :] * xr_ref[rs, :]
        for j in range(1, CONV_WIDTH):
            xc = xc + (cw_ref[CONV_WIDTH - 1 - j:CONV_WIDTH - j, :]
                       * xpad_ref[pl.ds(halo - j + r0, RG_SUB), :])
        mixed = [oa_ref[hd, rs, :] for hd in range(A_HEADS)]
        for g in range(RG_BLOCKS):
            sl = slice(g * RG_BLOCK_DIM, (g + 1) * RG_BLOCK_DIM)
            xg = xc[:, sl]
            rg = _dot(xg.astype(BF16), wg[g])
            r = jax.nn.sigmoid(rg[:, :RG_BLOCK_DIM] + gab_ref[:, sl])
            i = jax.nn.sigmoid(rg[:, RG_BLOCK_DIM:] + gxb_ref[:, sl])
            log_a = (-RG_C) * r * softplus[:, sl]
            a = jnp.exp(log_a)
            th = jnp.tanh(log_a)
            u = lax.rsqrt((th - 1.0) / (2.0 * th)) * (i * xg)
            a = a.reshape(grouped)
            u = u.reshape(grouped)
            k = 1
            while k < SUBLANES:
                valid = row >= k
                a_prev = pltpu.roll(a, k, 1)
                u_prev = pltpu.roll(u, k, 1)
                u = jnp.where(valid, a * u_prev + u, u)
                a = jnp.where(valid, a * a_prev, a)
                k *= 2
            a = a.reshape(RG_SUB, RG_BLOCK_DIM)
            u = u.reshape(RG_SUB, RG_BLOCK_DIM)
            hs = []
            c = carry[g]
            for gi in range(0, RG_SUB, SUBLANES):
                h = u[gi:gi + SUBLANES, :] + a[gi:gi + SUBLANES, :] * c
                hs.append(h)
                c = jnp.broadcast_to(h[SUBLANES - 1:SUBLANES, :], group)
            carry[g] = c
            mixed.append((jnp.concatenate(hs, axis=0) * _gelu_tanh(gate_ref[rs, sl])).astype(BF16))
        o_ref[rs, :] = x_ref[rs, :] + _dot(jnp.concatenate(mixed, axis=-1), w_out)

    for g in range(RG_BLOCKS):
        carry_ref[:, g * RG_BLOCK_DIM:(g + 1) * RG_BLOCK_DIM] = carry[g]
    xpad_ref[0:halo, :] = xr_ref[RG_TILE - halo:RG_TILE, :]


def _rglru_out(x, o_a, xr, gate, e, conv_w, conv_b, w_gates, ga_b, gx_b, lam, w_out):
    batch, seq, _ = x.shape
    full = pl.BlockSpec((None, RG_TILE, D_MODEL), lambda b, t: (b, t, 0))
    half = pl.BlockSpec((None, RG_TILE, RG_WIDTH), lambda b, t: (b, t, 0))
    vec = pl.BlockSpec((None, 1, RG_WIDTH), lambda b, t: (e, 0, 0))
    return pl.pallas_call(
        _rglru_out_kernel,
        grid=(batch, seq // RG_TILE),
        in_specs=[
            full,
            pl.BlockSpec((A_HEADS, None, RG_TILE, A_HEAD_DIM), lambda b, t: (0, b, t, 0)),
            half, half,
            pl.BlockSpec((None, CONV_WIDTH, RG_WIDTH), lambda b, t: (e, 0, 0)),
            vec,
            pl.BlockSpec((None, RG_BLOCKS, RG_BLOCK_DIM, 2 * RG_BLOCK_DIM), lambda b, t: (e, 0, 0, 0)),
            vec, vec, vec,
            _resident((None, D_MODEL, D_MODEL), lambda b, t: (e, 0, 0)),
        ],
        out_specs=full,
        out_shape=jax.ShapeDtypeStruct(x.shape, F32),
        scratch_shapes=[
            pltpu.VMEM((RG_TILE + SUBLANES, RG_WIDTH), F32),
            pltpu.VMEM((SUBLANES, RG_WIDTH), F32),
        ],
        compiler_params=_params("arbitrary", "arbitrary"),
        name="rglru_outproj",
    )(x, o_a, xr, gate, conv_w, conv_b, w_gates, ga_b, gx_b, lam, w_out)


def kernel(x, mem, mem_norm_g, mix_norm_g, xattn_norm_g, mlp_norm_g, ev_w_in, ev_q_norm_g, ev_k_norm_g, ev_conv_w, ev_conv_b, ev_gate_a_w, ev_gate_a_b, ev_gate_x_w, ev_gate_x_b, ev_lambda, ev_w_out, od_pool_w, od_scale, xa_w_q, xa_w_kv, xa_q_norm_g, xa_k_norm_g, xa_w_o, mlp_w1, mlp_w2):
    batch, seq, _ = x.shape
    rows = batch * seq

    def vec3(a):
        return a.reshape(a.shape[0], 1, a.shape[1])

    w_in, w_out = ev_w_in, ev_w_out
    w_gates = jnp.concatenate([ev_gate_a_w, ev_gate_x_w], axis=-1)
    pool_w = od_pool_w
    w_q, w_kv, w_o = xa_w_q, xa_w_kv, xa_w_o
    w1, w2 = mlp_w1, mlp_w2
    mix_g, xa_g, mlp_g = vec3(mix_norm_g), vec3(xattn_norm_g), vec3(mlp_norm_g)
    ev_qg, ev_kg = vec3(ev_q_norm_g), vec3(ev_k_norm_g)
    xa_qg = vec3(xa_q_norm_g)
    conv_b, ga_b, gx_b, lam = vec3(ev_conv_b), vec3(ev_gate_a_b), vec3(ev_gate_x_b), vec3(ev_lambda)
    od_sc = vec3(od_scale)

    kt, v_mem = _memkv(mem, mem_norm_g, w_kv, xa_k_norm_g)

    for l in range(DEPTH):
        if l % 2 == 0:
            e = l // 2
            q, k, v, xr, gate = _inproj(x.reshape(rows, D_MODEL), l, e, mix_g, w_in, ev_qg, ev_kg)
            half = (batch, seq, MIX_HALF)
            heads = (A_HEADS, batch, seq, A_HEAD_DIM)
            o_a = _attn(q.reshape(heads), k.reshape(heads), v.reshape(heads))
            x = _rglru_out(x, o_a, xr.reshape(half), gate.reshape(half), e, ev_conv_w, conv_b,
                           w_gates, ga_b, gx_b, lam, w_out)
            pool = None
        else:
            pool = (mix_g, pool_w, od_sc)
        x = _xattn(x, l, xa_g, w_q, xa_qg, kt, v_mem, w_o, pool)
        x = _mlp(x.reshape(rows, D_MODEL), l, mlp_g, w1, w2).reshape(batch, seq, D_MODEL)
    return x
```

```python
import functools

import jax
import jax.numpy as jnp
from jax import lax
from jax.experimental import pallas as pl
from jax.experimental.pallas import tpu as pltpu

D_MODEL = 1024
DEPTH = 4
N_MEM = 256
MIX_HALF = D_MODEL // 2
A_HEADS = 4
A_HEAD_DIM = MIX_HALF // A_HEADS
A_BAND = 128
A_DILATIONS = (16, 4, 1)
A_CHUNKS_PER_STEP = 8
RG_WIDTH = MIX_HALF
RG_BLOCKS = 4
RG_BLOCK_DIM = RG_WIDTH // RG_BLOCKS
RG_C = 8.0
CONV_WIDTH = 4
POOL_WINDOWS = (2, 4, 8, 16)
POOL_GROUP_DIM = D_MODEL // len(POOL_WINDOWS)
POOL_HALO = 16
XA_HEADS = 4
XA_HEAD_DIM = D_MODEL // XA_HEADS
D_FF = 4 * D_MODEL
IN_COLS = 3 * MIX_HALF + 2 * RG_WIDTH
EPS = 1e-6
LOG2_E = 1.4426950408889634

SUBLANES = 8
VMEM_LIMIT = 56 * 1024 * 1024

ROW_TILE = 1024
XA_SUB = 512
MLP_TILE = 1024
MLP_CHUNK = 1024
RG_TILE = 1024
RG_SUB = 256

BF16 = jnp.bfloat16
F32 = jnp.float32


def _rms(x, g):
    ms = jnp.mean(x * x, axis=-1, keepdims=True)
    return x * lax.rsqrt(ms + EPS) * g


def _dot(a, b):
    return jnp.dot(a, b, preferred_element_type=F32)


def _dot_nt(a, b):
    return lax.dot_general(a, b, (((1,), (1,)), ((), ())), preferred_element_type=F32)


def _resident(shape, index_map):
    return pl.BlockSpec(shape, index_map, pipeline_mode=pl.Buffered(1))


def _params(*sem):
    return pltpu.CompilerParams(dimension_semantics=sem, vmem_limit_bytes=VMEM_LIMIT)


def _memkv_kernel(mem_ref, mg_ref, wkv_ref, kg_ref, kt_ref, v_ref):
    batch = mem_ref.shape[0]
    mem = mem_ref[...].reshape(batch * N_MEM, D_MODEL)
    mem_n = _rms(mem, mg_ref[...]).astype(BF16)
    kv = _dot(mem_n, wkv_ref[...].astype(BF16))
    scale = XA_HEAD_DIM ** -0.5 * LOG2_E
    for h in range(XA_HEADS):
        sl = slice(h * XA_HEAD_DIM, (h + 1) * XA_HEAD_DIM)
        kh = _rms(kv[:, sl], kg_ref[...]) * scale
        for b in range(batch):
            kt_ref[b, sl, :] = kh[b * N_MEM:(b + 1) * N_MEM, :].T.astype(BF16)
    v_ref[...] = kv[:, D_MODEL:].astype(BF16).reshape(batch, N_MEM, D_MODEL)


def _memkv(mem, mem_norm_g, w_kv, k_g):
    batch = mem.shape[0]
    return pl.pallas_call(
        _memkv_kernel,
        grid=(DEPTH,),
        in_specs=[
            pl.BlockSpec((batch, N_MEM, D_MODEL), lambda l: (0, 0, 0)),
            pl.BlockSpec((1, D_MODEL), lambda l: (0, 0)),
            pl.BlockSpec((None, D_MODEL, 2 * D_MODEL), lambda l: (l, 0, 0)),
            pl.BlockSpec((None, 1, XA_HEAD_DIM), lambda l: (l, 0, 0)),
        ],
        out_specs=[
            pl.BlockSpec((None, batch, D_MODEL, N_MEM), lambda l: (l, 0, 0, 0)),
            pl.BlockSpec((None, batch, N_MEM, D_MODEL), lambda l: (l, 0, 0, 0)),
        ],
        out_shape=[
            jax.ShapeDtypeStruct((DEPTH, batch, D_MODEL, N_MEM), BF16),
            jax.ShapeDtypeStruct((DEPTH, batch, N_MEM, D_MODEL), BF16),
        ],
        compiler_params=_params("arbitrary"),
        name="memkv",
    )(mem, mem_norm_g.reshape(1, D_MODEL), w_kv, k_g.reshape(DEPTH, 1, XA_HEAD_DIM))


def _pool_rows(x, r0, pos0, mg_ref, pw_ref, sc_ref, hpad_ref):
    rows = x.shape[0]
    h = _rms(x, mg_ref[...])
    hpad_ref[POOL_HALO + r0:POOL_HALO + r0 + rows, :] = h
    pos = pos0 + lax.broadcasted_iota(jnp.int32, (rows, 1), 0)
    s = hpad_ref[r0:r0 + POOL_HALO + rows, :]
    outs = []
    width = 1
    for gi, w in enumerate(POOL_WINDOWS):
        while width < w:
            s = s + pltpu.roll(s, width, 0)
            width *= 2
        cnt = jnp.minimum(pos + 1, w).astype(F32)
        d = s[POOL_HALO:, :POOL_GROUP_DIM] / cnt - h[:, gi * POOL_GROUP_DIM:(gi + 1) * POOL_GROUP_DIM]
        outs.append(_dot(d.astype(BF16), pw_ref[gi].astype(BF16)))
        if gi + 1 < len(POOL_WINDOWS):
            s = s[:, POOL_GROUP_DIM:]
    return x + jnp.concatenate(outs, axis=-1) * sc_ref[...]


def _xattn_kernel(*refs, pool):
    if pool:
        (x_ref, mg_ref, pw_ref, sc_ref, g_ref, wq_ref, qg_ref, kt_ref, v_ref, wo_ref, w1p_ref, w2p_ref,
         o_ref, w1b_ref, w2b_ref, hpad_ref) = refs
        t = pl.program_id(1)

        @pl.when(t == 0)
        def _():
            hpad_ref[0:POOL_HALO, :] = jnp.zeros((POOL_HALO, D_MODEL), F32)
    else:
        (x_ref, g_ref, wq_ref, qg_ref, kt_ref, v_ref, wo_ref, w1p_ref, w2p_ref,
         o_ref, w1b_ref, w2b_ref) = refs
    w1b_ref[...] = w1p_ref[...].astype(BF16)
    w2b_ref[...] = w2p_ref[...].astype(BF16)
    w_q = wq_ref[...].astype(BF16)
    w_o = wo_ref[...].astype(BF16)
    heads = [slice(hd * XA_HEAD_DIM, (hd + 1) * XA_HEAD_DIM) for hd in range(XA_HEADS)]
    for r0 in range(0, ROW_TILE, XA_SUB):
        rs = slice(r0, r0 + XA_SUB)
        x = x_ref[rs, :]
        if pool:
            x = _pool_rows(x, r0, t * ROW_TILE + r0, mg_ref, pw_ref, sc_ref, hpad_ref)
        h = _rms(x, g_ref[...]).astype(BF16)
        qs = [_dot(h, w_q[:, sl]) for sl in heads]
        qs = [_rms(q, qg_ref[...]).astype(BF16) for q in qs]
        ss = [_dot(q, kt_ref[sl, :]) for q, sl in zip(qs, heads)]
        ps = [jnp.exp2(s - jnp.max(s, axis=-1, keepdims=True)) for s in ss]
        dens = [jnp.sum(p, axis=-1, keepdims=True) for p in ps]
        os_ = [_dot(p.astype(BF16), v_ref[:, sl]) for p, sl in zip(ps, heads)]
        os_ = [(o / den).astype(BF16) for o, den in zip(os_, dens)]
        out = x
        for o, sl in zip(os_, heads):
            out = out + _dot(o, w_o[sl, :])
        o_ref[rs, :] = out
    if pool:
        hpad_ref[0:POOL_HALO, :] = hpad_ref[ROW_TILE:ROW_TILE + POOL_HALO, :]


def _xattn(x, layer, g, w_q, q_g, kt, v, w_o, w1, w2, pool=None):
    batch, seq, _ = x.shape
    n_t = seq // ROW_TILE
    steps = batch * n_t
    tile = pl.BlockSpec((None, ROW_TILE, D_MODEL), lambda b, t: (b, t, 0))
    in_specs = [
        pl.BlockSpec((None, 1, D_MODEL), lambda b, t: (layer, 0, 0)),
        _resident((None, D_MODEL, D_MODEL), lambda b, t: (layer, 0, 0)),
        pl.BlockSpec((None, 1, XA_HEAD_DIM), lambda b, t: (layer, 0, 0)),
        pl.BlockSpec((None, None, D_MODEL, N_MEM), lambda b, t: (layer, b, 0, 0)),
        pl.BlockSpec((None, None, N_MEM, D_MODEL), lambda b, t: (layer, b, 0, 0)),
        _resident((None, D_MODEL, D_MODEL), lambda b, t: (layer, 0, 0)),
        pl.BlockSpec((None, D_MODEL // steps, D_FF), lambda b, t: (layer, b * n_t + t, 0)),
        pl.BlockSpec((None, D_FF // steps, D_MODEL), lambda b, t: (layer, b * n_t + t, 0)),
    ]
    args = (g, w_q, q_g, kt, v, w_o, w1, w2)
    scratch = []
    if pool is not None:
        o = layer // 2
        in_specs = [
            pl.BlockSpec((None, 1, D_MODEL), lambda b, t: (layer, 0, 0)),
            pl.BlockSpec((None, len(POOL_WINDOWS), POOL_GROUP_DIM, POOL_GROUP_DIM),
                         lambda b, t: (o, 0, 0, 0)),
            pl.BlockSpec((None, 1, D_MODEL), lambda b, t: (o, 0, 0)),
        ] + in_specs
        args = tuple(pool) + args
        scratch = [pltpu.VMEM((POOL_HALO + ROW_TILE, D_MODEL), F32)]
    return pl.pallas_call(
        functools.partial(_xattn_kernel, pool=pool is not None),
        grid=(batch, seq // ROW_TILE),
        in_specs=[tile] + in_specs,
        out_specs=[
            tile,
            pl.BlockSpec((D_MODEL // steps, D_FF), lambda b, t: (b * n_t + t, 0)),
            pl.BlockSpec((D_FF // steps, D_MODEL), lambda b, t: (b * n_t + t, 0)),
        ],
        out_shape=[
            jax.ShapeDtypeStruct(x.shape, F32),
            jax.ShapeDtypeStruct((D_MODEL, D_FF), BF16),
            jax.ShapeDtypeStruct((D_FF, D_MODEL), BF16),
        ],
        scratch_shapes=scratch,
        compiler_params=_params("parallel", "arbitrary" if pool is not None else "parallel"),
        name="xattn" if pool is None else "pool_xattn",
    )(x, *args)


def _mlp_kernel(x_ref, g_ref, w1_ref, w2_ref, o_ref):
    x = x_ref[...]
    h = _rms(x, g_ref[...]).astype(BF16)
    acc = x
    for c in range(D_FF // MLP_CHUNK):
        sl = slice(c * MLP_CHUNK, (c + 1) * MLP_CHUNK)
        a = jnp.maximum(_dot(h, w1_ref[:, sl]), 0.0)
        acc = acc + _dot((a * a).astype(BF16), w2_ref[sl, :])
    o_ref[...] = acc


def _mlp(x2, layer, g, w1b, w2b):
    rows = x2.shape[0]
    return pl.pallas_call(
        _mlp_kernel,
        grid=(rows // MLP_TILE,),
        in_specs=[
            pl.BlockSpec((MLP_TILE, D_MODEL), lambda i: (i, 0)),
            pl.BlockSpec((None, 1, D_MODEL), lambda i: (layer, 0, 0)),
            _resident((D_MODEL, D_FF), lambda i: (0, 0)),
            _resident((D_FF, D_MODEL), lambda i: (0, 0)),
        ],
        out_specs=pl.BlockSpec((MLP_TILE, D_MODEL), lambda i: (i, 0)),
        out_shape=jax.ShapeDtypeStruct(x2.shape, F32),
        compiler_params=_params("parallel"),
        name="mlp",
    )(x2, g, w1b, w2b)


def _inproj_kernel(x_ref, g_ref, w_ref, qg_ref, kg_ref, q_ref, k_ref, v_ref, xr_ref, gate_ref):
    h = _rms(x_ref[...], g_ref[...]).astype(BF16)
    scale = A_HEAD_DIM ** -0.5 * LOG2_E
    for j, (out_ref, gain_ref, mult) in enumerate(
            ((q_ref, qg_ref, scale), (k_ref, kg_ref, 1.0), (v_ref, None, None),
             (xr_ref, None, None), (gate_ref, None, None))):
        p = _dot(h, w_ref[:, j * MIX_HALF:(j + 1) * MIX_HALF].astype(BF16))
        if j >= 3:
            out_ref[...] = p
            continue
        for hd in range(A_HEADS):
            ph = p[:, hd * A_HEAD_DIM:(hd + 1) * A_HEAD_DIM]
            out_ref[hd] = ph if gain_ref is None else _rms(ph, gain_ref[...]) * mult


def _inproj(x2, layer, e, g, w_in, q_g, k_g):
    rows = x2.shape[0]
    half = jax.ShapeDtypeStruct((rows, MIX_HALF), F32)
    half_spec = pl.BlockSpec((ROW_TILE, MIX_HALF), lambda i: (i, 0))
    heads = jax.ShapeDtypeStruct((A_HEADS, rows, A_HEAD_DIM), F32)
    heads_spec = pl.BlockSpec((A_HEADS, ROW_TILE, A_HEAD_DIM), lambda i: (0, i, 0))
    gain_spec = pl.BlockSpec((None, 1, A_HEAD_DIM), lambda i: (e, 0, 0))
    return pl.pallas_call(
        _inproj_kernel,
        grid=(rows // ROW_TILE,),
        in_specs=[
            pl.BlockSpec((ROW_TILE, D_MODEL), lambda i: (i, 0)),
            pl.BlockSpec((None, 1, D_MODEL), lambda i: (layer, 0, 0)),
            _resident((None, D_MODEL, IN_COLS), lambda i: (e, 0, 0)),
            gain_spec, gain_spec,
        ],
        out_specs=[heads_spec] * 3 + [half_spec] * 2,
        out_shape=[heads] * 3 + [half] * 2,
        compiler_params=_params("parallel"),
        name="inproj",
    )(x2, g, w_in, q_g, k_g)


def _attn_kernel(q_ref, k_ref, v_ref, o_ref, m_ref, l_ref, acc_ref, *, seq):
    jq = lax.broadcasted_iota(jnp.int32, (A_BAND, A_BAND), 0)
    jk = lax.broadcasted_iota(jnp.int32, (A_BAND, A_BAND), 1)
    own_mask = jk <= jq
    jk2 = lax.broadcasted_iota(jnp.int32, (A_BAND, 2 * A_BAND), 1)
    jq2 = lax.broadcasted_iota(jnp.int32, (A_BAND, 2 * A_BAND), 0)
    both_mask = (jk2 >= jq2) & (jk2 <= jq2 + A_BAND)
    neg = -jnp.inf
    lanes = (A_BAND, A_HEAD_DIM)
    ones = jnp.ones(lanes, BF16)

    def rows(start, d):
        if d == 1:
            return pl.ds(pl.multiple_of(start, A_BAND), A_BAND)
        return pl.ds(start, A_BAND, stride=d)

    def load_kv(start, d):
        r = rows(start, d)
        return (k_ref[r, :].astype(BF16),
                jnp.concatenate([v_ref[r, :].astype(BF16), ones], axis=1))

    def process(jobs, d, mode):
        rs = [rows(start, d) for start, _, _ in jobs]
        qs = [q_ref[r, :].astype(BF16) for r in rs]
        scores = []
        for q, (_, own, prev) in zip(qs, jobs):
            if prev is None:
                s = jnp.where(own_mask, _dot_nt(q, own[0]), neg)
            else:
                s = _dot_nt(q, jnp.concatenate([prev[0], own[0]], axis=0))
                s = jnp.where(both_mask, s, neg)
            scores.append(s)
        ms = [jnp.broadcast_to(jnp.max(s, axis=-1, keepdims=True), lanes) for s in scores]
        if mode != "init":
            m_olds = [m_ref[r, :] for r in rs]
            ms = [jnp.maximum(m_old, m) for m_old, m in zip(m_olds, ms)]
        pvs = []
        for s, m, (_, own, prev) in zip(scores, ms, jobs):
            if prev is None:
                v = own[1]
            else:
                v = jnp.concatenate([prev[1], own[1]], axis=0)
                m = jnp.concatenate([m, m], axis=1)
            pvs.append(_dot(jnp.exp2(s - m).astype(BF16), v))
        for i, (r, m, pv) in enumerate(zip(rs, ms, pvs)):
            acc, den = pv[:, :A_HEAD_DIM], pv[:, A_HEAD_DIM:]
            if mode != "init":
                a_old = jnp.exp2(m_olds[i] - m)
                den = a_old * l_ref[r, :] + den
                acc = a_old * acc_ref[r, :] + acc
            if mode == "final":
                o_ref[r, :] = (acc / den).astype(o_ref.dtype)
            else:
                m_ref[r, :] = m
                l_ref[r, :] = den
                acc_ref[r, :] = acc

    def jobs_of(r, c0, n, d, first):
        pitch = A_BAND * d
        kv = [None if first else load_kv(r + (c0 - 1) * pitch, d)]
        kv += [load_kv(r + (c0 + j) * pitch, d) for j in range(n)]
        return [(r + (c0 + j) * pitch, kv[j + 1], kv[j]) for j in range(n)]

    def group(r, c0, n, d, mode, first):
        process(jobs_of(r, c0, n, d, first), d, mode)

    def run_pass(d, mode):
        n_chunks = seq // (A_BAND * d)
        n = min(n_chunks, A_CHUNKS_PER_STEP)
        if n_chunks == n:
            per_step = A_CHUNKS_PER_STEP // n

            def step(i, carry):
                jobs = []
                for j in range(per_step):
                    jobs += jobs_of(i * per_step + j, 0, n, d, True)
                process(jobs, d, mode)
                return carry

            lax.fori_loop(0, d // per_step, step, 0)
            return

        def residue(r, carry):
            group(r, 0, n, d, mode, True)

            def step(i, c):
                group(r, i * n, n, d, mode, False)
                return c

            return lax.fori_loop(1, n_chunks // n, step, carry)

        if d == 1:
            residue(0, 0)
        else:
            lax.fori_loop(0, d, residue, 0)

    modes = ("init",) + ("merge",) * (len(A_DILATIONS) - 2) + ("final",)
    for d, mode in zip(A_DILATIONS, modes):
        run_pass(d, mode)


def _attn(q, k, v):
    _, batch, seq, _ = q.shape
    assert seq % (A_BAND * max(A_DILATIONS)) == 0
    spec = pl.BlockSpec((None, None, seq, A_HEAD_DIM), lambda b, h: (h, b, 0, 0))
    return pl.pallas_call(
        functools.partial(_attn_kernel, seq=seq),
        grid=(batch, A_HEADS),
        in_specs=[spec, spec, spec],
        out_specs=spec,
        out_shape=jax.ShapeDtypeStruct(q.shape, BF16),
        scratch_shapes=[pltpu.VMEM((seq, A_HEAD_DIM), F32)] * 3,
        compiler_params=_params("parallel", "parallel"),
        name="dilated_attn",
    )(q, k, v)


def _gelu_tanh(x):
    c1 = -2.0 * 0.7978845608028654 * LOG2_E
    return x / (1.0 + jnp.exp2(x * (c1 + (c1 * 0.044715) * (x * x))))


def _rglru_out_kernel(x_ref, oa_ref, xr_ref, gate_ref, cw_ref, cb_ref, wg_ref, gab_ref, gxb_ref,
                      lam_ref, wout_ref, o_ref, xpad_ref, carry_ref):
    halo = SUBLANES
    group = (SUBLANES, RG_BLOCK_DIM)

    @pl.when(pl.program_id(1) == 0)
    def _():
        xpad_ref[0:halo, :] = jnp.zeros((halo, RG_WIDTH), F32)
        carry_ref[...] = jnp.zeros((SUBLANES, RG_WIDTH), F32)

    xpad_ref[halo:halo + RG_TILE, :] = xr_ref[...]
    w_out = wout_ref[...].astype(BF16)
    wg = [wg_ref[g].astype(BF16) for g in range(RG_BLOCKS)]
    lam = -lam_ref[...]
    softplus = jnp.maximum(lam, 0.0) + jnp.log1p(jnp.exp(-jnp.abs(lam)))
    grouped = (RG_SUB // SUBLANES, SUBLANES, RG_BLOCK_DIM)
    row = lax.broadcasted_iota(jnp.int32, grouped, 1)
    carry = [carry_ref[:, g * RG_BLOCK_DIM:(g + 1) * RG_BLOCK_DIM] for g in range(RG_BLOCKS)]

    for r0 in range(0, RG_TILE, RG_SUB):
        rs = slice(r0, r0 + RG_SUB)
        xc = cb_ref[...] + cw_ref[CONV_WIDTH - 1:CONV_WIDTH, :] * xr_ref[rs, :]
        for j in range(1, CONV_WIDTH):
            xc = xc + (cw_ref[CONV_WIDTH - 1 - j:CONV_WIDTH - j, :]
                       * xpad_ref[pl.ds(halo - j + r0, RG_SUB), :])
        mixed = [oa_ref[hd, rs, :] for hd in range(A_HEADS)]
        for g in range(RG_BLOCKS):
            sl = slice(g * RG_BLOCK_DIM, (g + 1) * RG_BLOCK_DIM)
            xg = xc[:, sl]
            rg = _dot(xg.astype(BF16), wg[g])
            r = jax.nn.sigmoid(rg[:, :RG_BLOCK_DIM] + gab_ref[:, sl])
            i = jax.nn.sigmoid(rg[:, RG_BLOCK_DIM:] + gxb_ref[:, sl])
            log_a = (-RG_C) * r * softplus[:, sl]
            a = jnp.exp(log_a)
            th = jnp.tanh(log_a)
            u = lax.rsqrt((th - 1.0) / (2.0 * th)) * (i * xg)
            a = a.reshape(grouped)
            u = u.reshape(grouped)
            k = 1
            while k < SUBLANES:
                valid = row >= k
                a_prev = pltpu.roll(a, k, 1)
                u_prev = pltpu.roll(u, k, 1)
                u = jnp.where(valid, a * u_prev + u, u)
                a = jnp.where(valid, a * a_prev, a)
                k *= 2
            a = a.reshape(RG_SUB, RG_BLOCK_DIM)
            u = u.reshape(RG_SUB, RG_BLOCK_DIM)
            hs = []
            c = carry[g]
            for gi in range(0, RG_SUB, SUBLANES):
                h = u[gi:gi + SUBLANES, :] + a[gi:gi + SUBLANES, :] * c
                hs.append(h)
                c = jnp.broadcast_to(h[SUBLANES - 1:SUBLANES, :], group)
            carry[g] = c
            mixed.append((jnp.concatenate(hs, axis=0) * _gelu_tanh(gate_ref[rs, sl])).astype(BF16))
        o_ref[rs, :] = x_ref[rs, :] + _dot(jnp.concatenate(mixed, axis=-1), w_out)

    for g in range(RG_BLOCKS):
        carry_ref[:, g * RG_BLOCK_DIM:(g + 1) * RG_BLOCK_DIM] = carry[g]
    xpad_ref[0:halo, :] = xr_ref[RG_TILE - halo:RG_TILE, :]


def _rglru_out(x, o_a, xr, gate, e, conv_w, conv_b, w_gates, ga_b, gx_b, lam, w_out):
    batch, seq, _ = x.shape
    full = pl.BlockSpec((None, RG_TILE, D_MODEL), lambda b, t: (b, t, 0))
    half = pl.BlockSpec((None, RG_TILE, RG_WIDTH), lambda b, t: (b, t, 0))
    vec = pl.BlockSpec((None, 1, RG_WIDTH), lambda b, t: (e, 0, 0))
    return pl.pallas_call(
        _rglru_out_kernel,
        grid=(batch, seq // RG_TILE),
        in_specs=[
            full,
            pl.BlockSpec((A_HEADS, None, RG_TILE, A_HEAD_DIM), lambda b, t: (0, b, t, 0)),
            half, half,
            pl.BlockSpec((None, CONV_WIDTH, RG_WIDTH), lambda b, t: (e, 0, 0)),
            vec,
            pl.BlockSpec((None, RG_BLOCKS, RG_BLOCK_DIM, 2 * RG_BLOCK_DIM), lambda b, t: (e, 0, 0, 0)),
            vec, vec, vec,
            _resident((None, D_MODEL, D_MODEL), lambda b, t: (e, 0, 0)),
        ],
        out_specs=full,
        out_shape=jax.ShapeDtypeStruct(x.shape, F32),
        scratch_shapes=[
            pltpu.VMEM((RG_TILE + SUBLANES, RG_WIDTH), F32),
            pltpu.VMEM((SUBLANES, RG_WIDTH), F32),
        ],
        compiler_params=_params("arbitrary", "arbitrary"),
        name="rglru_outproj",
    )(x, o_a, xr, gate, conv_w, conv_b, w_gates, ga_b, gx_b, lam, w_out)


def kernel(x, mem, mem_norm_g, mix_norm_g, xattn_norm_g, mlp_norm_g, ev_w_in, ev_q_norm_g, ev_k_norm_g, ev_conv_w, ev_conv_b, ev_gate_a_w, ev_gate_a_b, ev_gate_x_w, ev_gate_x_b, ev_lambda, ev_w_out, od_pool_w, od_scale, xa_w_q, xa_w_kv, xa_q_norm_g, xa_k_norm_g, xa_w_o, mlp_w1, mlp_w2):
    batch, seq, _ = x.shape
    rows = batch * seq

    def vec3(a):
        return a.reshape(a.shape[0], 1, a.shape[1])

    w_in, w_out = ev_w_in, ev_w_out
    w_gates = jnp.concatenate([ev_gate_a_w, ev_gate_x_w], axis=-1)
    pool_w = od_pool_w
    w_q, w_kv, w_o = xa_w_q, xa_w_kv, xa_w_o
    w1, w2 = mlp_w1, mlp_w2
    mix_g, xa_g, mlp_g = vec3(mix_norm_g), vec3(xattn_norm_g), vec3(mlp_norm_g)
    ev_qg, ev_kg = vec3(ev_q_norm_g), vec3(ev_k_norm_g)
    xa_qg = vec3(xa_q_norm_g)
    conv_b, ga_b, gx_b, lam = vec3(ev_conv_b), vec3(ev_gate_a_b), vec3(ev_gate_x_b), vec3(ev_lambda)
    od_sc = vec3(od_scale)

    kt, v_mem = _memkv(mem, mem_norm_g, w_kv, xa_k_norm_g)

    for l in range(DEPTH):
        if l % 2 == 0:
            e = l // 2
            q, k, v, xr, gate = _inproj(x.reshape(rows, D_MODEL), l, e, mix_g, w_in, ev_qg, ev_kg)
            half = (batch, seq, MIX_HALF)
            heads = (A_HEADS, batch, seq, A_HEAD_DIM)
            o_a = _attn(q.reshape(heads), k.reshape(heads), v.reshape(heads))
            x = _rglru_out(x, o_a, xr.reshape(half), gate.reshape(half), e, ev_conv_w, conv_b,
                           w_gates, ga_b, gx_b, lam, w_out)
            pool = None
        else:
            pool = (mix_g, pool_w, od_sc)
        x, w1b, w2b = _xattn(x, l, xa_g, w_q, xa_qg, kt, v_mem, w_o, w1, w2, pool)
        x = _mlp(x.reshape(rows, D_MODEL), l, mlp_g, w1b, w2b).reshape(batch, seq, D_MODEL)
    return x
```

```python
import functools

import jax
import jax.numpy as jnp
from jax import lax
from jax.experimental import pallas as pl
from jax.experimental.pallas import tpu as pltpu

D_MODEL = 1024
DEPTH = 4
N_MEM = 256
MIX_HALF = D_MODEL // 2
A_HEADS = 4
A_HEAD_DIM = MIX_HALF // A_HEADS
A_BAND = 128
A_DILATIONS = (16, 4, 1)
A_CHUNKS_PER_STEP = 8
RG_WIDTH = MIX_HALF
RG_BLOCKS = 4
RG_BLOCK_DIM = RG_WIDTH // RG_BLOCKS
RG_C = 8.0
CONV_WIDTH = 4
POOL_WINDOWS = (2, 4, 8, 16)
POOL_GROUP_DIM = D_MODEL // len(POOL_WINDOWS)
POOL_HALO = 16
XA_HEADS = 4
XA_HEAD_DIM = D_MODEL // XA_HEADS
D_FF = 4 * D_MODEL
IN_COLS = 3 * MIX_HALF + 2 * RG_WIDTH
EPS = 1e-6
LOG2_E = 1.4426950408889634

SUBLANES = 8
VMEM_LIMIT = 56 * 1024 * 1024

ROW_TILE = 1024
XA_SUB = 512
MLP_TILE = 1024
MLP_CHUNK = 1024
RG_TILE = 1024
RG_SUB = 256

BF16 = jnp.bfloat16
F32 = jnp.float32


def _rms(x, g):
    ms = jnp.mean(x * x, axis=-1, keepdims=True)
    return x * lax.rsqrt(ms + EPS) * g


def _dot(a, b):
    return jnp.dot(a, b, preferred_element_type=F32)


def _dot_nt(a, b):
    return lax.dot_general(a, b, (((1,), (1,)), ((), ())), preferred_element_type=F32)


def _resident(shape, index_map):
    return pl.BlockSpec(shape, index_map, pipeline_mode=pl.Buffered(1))


def _params(*sem):
    return pltpu.CompilerParams(dimension_semantics=sem, vmem_limit_bytes=VMEM_LIMIT)


def _memkv_kernel(mem_ref, mg_ref, wkv_ref, kg_ref, kt_ref, v_ref):
    batch = mem_ref.shape[0]
    mem = mem_ref[...].reshape(batch * N_MEM, D_MODEL)
    mem_n = _rms(mem, mg_ref[...]).astype(BF16)
    kv = _dot(mem_n, wkv_ref[...].astype(BF16))
    scale = XA_HEAD_DIM ** -0.5 * LOG2_E
    for h in range(XA_HEADS):
        sl = slice(h * XA_HEAD_DIM, (h + 1) * XA_HEAD_DIM)
        kh = _rms(kv[:, sl], kg_ref[...]) * scale
        for b in range(batch):
            kt_ref[b, sl, :] = kh[b * N_MEM:(b + 1) * N_MEM, :].T.astype(BF16)
    v_ref[...] = kv[:, D_MODEL:].astype(BF16).reshape(batch, N_MEM, D_MODEL)


def _memkv(mem, mem_norm_g, w_kv, k_g):
    batch = mem.shape[0]
    return pl.pallas_call(
        _memkv_kernel,
        grid=(DEPTH,),
        in_specs=[
            pl.BlockSpec((batch, N_MEM, D_MODEL), lambda l: (0, 0, 0)),
            pl.BlockSpec((1, D_MODEL), lambda l: (0, 0)),
            pl.BlockSpec((None, D_MODEL, 2 * D_MODEL), lambda l: (l, 0, 0)),
            pl.BlockSpec((None, 1, XA_HEAD_DIM), lambda l: (l, 0, 0)),
        ],
        out_specs=[
            pl.BlockSpec((None, batch, D_MODEL, N_MEM), lambda l: (l, 0, 0, 0)),
            pl.BlockSpec((None, batch, N_MEM, D_MODEL), lambda l: (l, 0, 0, 0)),
        ],
        out_shape=[
            jax.ShapeDtypeStruct((DEPTH, batch, D_MODEL, N_MEM), BF16),
            jax.ShapeDtypeStruct((DEPTH, batch, N_MEM, D_MODEL), BF16),
        ],
        compiler_params=_params("arbitrary"),
        name="memkv",
    )(mem, mem_norm_g.reshape(1, D_MODEL), w_kv, k_g.reshape(DEPTH, 1, XA_HEAD_DIM))


def _pool_rows(x, r0, pos0, mg_ref, pw_ref, sc_ref, hpad_ref):
    rows = x.shape[0]
    h = _rms(x, mg_ref[...])
    hpad_ref[POOL_HALO + r0:POOL_HALO + r0 + rows, :] = h
    pos = pos0 + lax.broadcasted_iota(jnp.int32, (rows, 1), 0)
    s = hpad_ref[r0:r0 + POOL_HALO + rows, :]
    outs = []
    width = 1
    for gi, w in enumerate(POOL_WINDOWS):
        while width < w:
            s = s + pltpu.roll(s, width, 0)
            width *= 2
        cnt = jnp.minimum(pos + 1, w).astype(F32)
        d = s[POOL_HALO:, :POOL_GROUP_DIM] / cnt - h[:, gi * POOL_GROUP_DIM:(gi + 1) * POOL_GROUP_DIM]
        outs.append(_dot(d.astype(BF16), pw_ref[gi].astype(BF16)))
        if gi + 1 < len(POOL_WINDOWS):
            s = s[:, POOL_GROUP_DIM:]
    return x + jnp.concatenate(outs, axis=-1) * sc_ref[...]


def _xattn_kernel(*refs, pool):
    if pool:
        (x_ref, mg_ref, pw_ref, sc_ref, g_ref, wq_ref, qg_ref, kt_ref, v_ref, wo_ref, w1p_ref, w2p_ref,
         o_ref, w1b_ref, w2b_ref, hpad_ref) = refs
        t = pl.program_id(1)

        @pl.when(t == 0)
        def _():
            hpad_ref[0:POOL_HALO, :] = jnp.zeros((POOL_HALO, D_MODEL), F32)
    else:
        (x_ref, g_ref, wq_ref, qg_ref, kt_ref, v_ref, wo_ref, w1p_ref, w2p_ref,
         o_ref, w1b_ref, w2b_ref) = refs
    w1b_ref[...] = w1p_ref[...].astype(BF16)
    w2b_ref[...] = w2p_ref[...].astype(BF16)
    w_q = wq_ref[...].astype(BF16)
    w_o = wo_ref[...].astype(BF16)
    heads = [slice(hd * XA_HEAD_DIM, (hd + 1) * XA_HEAD_DIM) for hd in range(XA_HEADS)]
    for r0 in range(0, ROW_TILE, XA_SUB):
        rs = slice(r0, r0 + XA_SUB)
        x = x_ref[rs, :]
        if pool:
            x = _pool_rows(x, r0, t * ROW_TILE + r0, mg_ref, pw_ref, sc_ref, hpad_ref)
        h = _rms(x, g_ref[...]).astype(BF16)
        qs = [_dot(h, w_q[:, sl]) for sl in heads]
        qs = [_rms(q, qg_ref[...]).astype(BF16) for q in qs]
        ss = [_dot(q, kt_ref[sl, :]) for q, sl in zip(qs, heads)]
        ps = [jnp.exp2(s - jnp.max(s, axis=-1, keepdims=True)) for s in ss]
        dens = [jnp.sum(p, axis=-1, keepdims=True) for p in ps]
        os_ = [_dot(p.astype(BF16), v_ref[:, sl]) for p, sl in zip(ps, heads)]
        os_ = [(o / den).astype(BF16) for o, den in zip(os_, dens)]
        out = x
        for o, sl in zip(os_, heads):
            out = out + _dot(o, w_o[sl, :])
        o_ref[rs, :] = out
    if pool:
        hpad_ref[0:POOL_HALO, :] = hpad_ref[ROW_TILE:ROW_TILE + POOL_HALO, :]


def _xattn(x, layer, g, w_q, q_g, kt, v, w_o, w1, w2, pool=None):
    batch, seq, _ = x.shape
    n_t = seq // ROW_TILE
    steps = batch * n_t
    tile = pl.BlockSpec((None, ROW_TILE, D_MODEL), lambda b, t: (b, t, 0))
    in_specs = [
        pl.BlockSpec((None, 1, D_MODEL), lambda b, t: (layer, 0, 0)),
        _resident((None, D_MODEL, D_MODEL), lambda b, t: (layer, 0, 0)),
        pl.BlockSpec((None, 1, XA_HEAD_DIM), lambda b, t: (layer, 0, 0)),
        pl.BlockSpec((None, None, D_MODEL, N_MEM), lambda b, t: (layer, b, 0, 0)),
        pl.BlockSpec((None, None, N_MEM, D_MODEL), lambda b, t: (layer, b, 0, 0)),
        _resident((None, D_MODEL, D_MODEL), lambda b, t: (layer, 0, 0)),
        pl.BlockSpec((None, D_MODEL // steps, D_FF), lambda b, t: (layer, b * n_t + t, 0)),
        pl.BlockSpec((None, D_FF // steps, D_MODEL), lambda b, t: (layer, b * n_t + t, 0)),
    ]
    args = (g, w_q, q_g, kt, v, w_o, w1, w2)
    scratch = []
    if pool is not None:
        o = layer // 2
        in_specs = [
            pl.BlockSpec((None, 1, D_MODEL), lambda b, t: (layer, 0, 0)),
            pl.BlockSpec((None, len(POOL_WINDOWS), POOL_GROUP_DIM, POOL_GROUP_DIM),
                         lambda b, t: (o, 0, 0, 0)),
            pl.BlockSpec((None, 1, D_MODEL), lambda b, t: (o, 0, 0)),
        ] + in_specs
        args = tuple(pool) + args
        scratch = [pltpu.VMEM((POOL_HALO + ROW_TILE, D_MODEL), F32)]
    return pl.pallas_call(
        functools.partial(_xattn_kernel, pool=pool is not None),
        grid=(batch, seq // ROW_TILE),
        in_specs=[tile] + in_specs,
        out_specs=[
            tile,
            pl.BlockSpec((D_MODEL // steps, D_FF), lambda b, t: (b * n_t + t, 0)),
            pl.BlockSpec((D_FF // steps, D_MODEL), lambda b, t: (b * n_t + t, 0)),
        ],
        out_shape=[
            jax.ShapeDtypeStruct(x.shape, F32),
            jax.ShapeDtypeStruct((D_MODEL, D_FF), BF16),
            jax.ShapeDtypeStruct((D_FF, D_MODEL), BF16),
        ],
        scratch_shapes=scratch,
        compiler_params=_params("parallel", "arbitrary" if pool is not None else "parallel"),
        name="xattn" if pool is None else "pool_xattn",
    )(x, *args)


def _mlp_kernel(x_ref, g_ref, w1_hbm, w2_hbm, o_ref, w1_ref, w2_ref, sems):
    chunks = [slice(c * MLP_CHUNK, (c + 1) * MLP_CHUNK) for c in range(D_FF // MLP_CHUNK)]

    def copies(c):
        sl = chunks[c]
        return (pltpu.make_async_copy(w1_hbm.at[:, sl], w1_ref.at[:, sl], sems.at[0, c]),
                pltpu.make_async_copy(w2_hbm.at[sl, :], w2_ref.at[sl, :], sems.at[1, c]))

    def body(first):
        if first:
            for c in range(len(chunks)):
                for cp in copies(c):
                    cp.start()
        x = x_ref[...]
        h = _rms(x, g_ref[...]).astype(BF16)
        acc = x
        for c, sl in enumerate(chunks):
            if first:
                for cp in copies(c):
                    cp.wait()
            a = jnp.maximum(_dot(h, w1_ref[:, sl]), 0.0)
            acc = acc + _dot((a * a).astype(BF16), w2_ref[sl, :])
        o_ref[...] = acc

    first_step = pl.program_id(0) == 0
    pl.when(first_step)(functools.partial(body, True))
    pl.when(jnp.logical_not(first_step))(functools.partial(body, False))


def _mlp(x2, layer, g, w1b, w2b):
    rows = x2.shape[0]
    return pl.pallas_call(
        _mlp_kernel,
        grid=(rows // MLP_TILE,),
        in_specs=[
            pl.BlockSpec((MLP_TILE, D_MODEL), lambda i: (i, 0)),
            pl.BlockSpec((None, 1, D_MODEL), lambda i: (layer, 0, 0)),
            pl.BlockSpec(memory_space=pl.ANY),
            pl.BlockSpec(memory_space=pl.ANY),
        ],
        out_specs=pl.BlockSpec((MLP_TILE, D_MODEL), lambda i: (i, 0)),
        out_shape=jax.ShapeDtypeStruct(x2.shape, F32),
        scratch_shapes=[
            pltpu.VMEM((D_MODEL, D_FF), BF16),
            pltpu.VMEM((D_FF, D_MODEL), BF16),
            pltpu.SemaphoreType.DMA((2, D_FF // MLP_CHUNK)),
        ],
        compiler_params=_params("arbitrary"),
        name="mlp",
    )(x2, g, w1b, w2b)


def _inproj_kernel(x_ref, g_ref, w_ref, qg_ref, kg_ref, q_ref, k_ref, v_ref, xr_ref, gate_ref):
    h = _rms(x_ref[...], g_ref[...]).astype(BF16)
    scale = A_HEAD_DIM ** -0.5 * LOG2_E
    for j, (out_ref, gain_ref, mult) in enumerate(
            ((q_ref, qg_ref, scale), (k_ref, kg_ref, 1.0), (v_ref, None, None),
             (xr_ref, None, None), (gate_ref, None, None))):
        p = _dot(h, w_ref[:, j * MIX_HALF:(j + 1) * MIX_HALF].astype(BF16))
        if j >= 3:
            out_ref[...] = p
            continue
        for hd in range(A_HEADS):
            ph = p[:, hd * A_HEAD_DIM:(hd + 1) * A_HEAD_DIM]
            out_ref[hd] = ph if gain_ref is None else _rms(ph, gain_ref[...]) * mult


def _inproj(x2, layer, e, g, w_in, q_g, k_g):
    rows = x2.shape[0]
    half = jax.ShapeDtypeStruct((rows, MIX_HALF), F32)
    half_spec = pl.BlockSpec((ROW_TILE, MIX_HALF), lambda i: (i, 0))
    heads = jax.ShapeDtypeStruct((A_HEADS, rows, A_HEAD_DIM), F32)
    heads_spec = pl.BlockSpec((A_HEADS, ROW_TILE, A_HEAD_DIM), lambda i: (0, i, 0))
    gain_spec = pl.BlockSpec((None, 1, A_HEAD_DIM), lambda i: (e, 0, 0))
    return pl.pallas_call(
        _inproj_kernel,
        grid=(rows // ROW_TILE,),
        in_specs=[
            pl.BlockSpec((ROW_TILE, D_MODEL), lambda i: (i, 0)),
            pl.BlockSpec((None, 1, D_MODEL), lambda i: (layer, 0, 0)),
            _resident((None, D_MODEL, IN_COLS), lambda i: (e, 0, 0)),
            gain_spec, gain_spec,
        ],
        out_specs=[heads_spec] * 3 + [half_spec] * 2,
        out_shape=[heads] * 3 + [half] * 2,
        compiler_params=_params("parallel"),
        name="inproj",
    )(x2, g, w_in, q_g, k_g)


def _attn_kernel(q_ref, k_ref, v_ref, o_ref, m_ref, l_ref, acc_ref, *, seq):
    jq = lax.broadcasted_iota(jnp.int32, (A_BAND, A_BAND), 0)
    jk = lax.broadcasted_iota(jnp.int32, (A_BAND, A_BAND), 1)
    own_mask = jk <= jq
    jk2 = lax.broadcasted_iota(jnp.int32, (A_BAND, 2 * A_BAND), 1)
    jq2 = lax.broadcasted_iota(jnp.int32, (A_BAND, 2 * A_BAND), 0)
    both_mask = (jk2 >= jq2) & (jk2 <= jq2 + A_BAND)
    neg = -jnp.inf
    lanes = (A_BAND, A_HEAD_DIM)
    ones = jnp.ones(lanes, BF16)

    def rows(start, d):
        if d == 1:
            return pl.ds(pl.multiple_of(start, A_BAND), A_BAND)
        return pl.ds(start, A_BAND, stride=d)

    def load_kv(start, d):
        r = rows(start, d)
        return (k_ref[r, :].astype(BF16),
                jnp.concatenate([v_ref[r, :].astype(BF16), ones], axis=1))

    def process(jobs, d, mode):
        rs = [rows(start, d) for start, _, _ in jobs]
        qs = [q_ref[r, :].astype(BF16) for r in rs]
        scores = []
        for q, (_, own, prev) in zip(qs, jobs):
            if prev is None:
                s = jnp.where(own_mask, _dot_nt(q, own[0]), neg)
            else:
                s = _dot_nt(q, jnp.concatenate([prev[0], own[0]], axis=0))
                s = jnp.where(both_mask, s, neg)
            scores.append(s)
        ms = [jnp.broadcast_to(jnp.max(s, axis=-1, keepdims=True), lanes) for s in scores]
        if mode != "init":
            m_olds = [m_ref[r, :] for r in rs]
            ms = [jnp.maximum(m_old, m) for m_old, m in zip(m_olds, ms)]
        pvs = []
        for s, m, (_, own, prev) in zip(scores, ms, jobs):
            if prev is None:
                v = own[1]
            else:
                v = jnp.concatenate([prev[1], own[1]], axis=0)
                m = jnp.concatenate([m, m], axis=1)
            pvs.append(_dot(jnp.exp2(s - m).astype(BF16), v))
        for i, (r, m, pv) in enumerate(zip(rs, ms, pvs)):
            acc, den = pv[:, :A_HEAD_DIM], pv[:, A_HEAD_DIM:]
            if mode != "init":
                a_old = jnp.exp2(m_olds[i] - m)
                den = a_old * l_ref[r, :] + den
                acc = a_old * acc_ref[r, :] + acc
            if mode == "final":
                o_ref[r, :] = (acc / den).astype(o_ref.dtype)
            else:
                m_ref[r, :] = m
                l_ref[r, :] = den
                acc_ref[r, :] = acc

    def jobs_of(r, c0, n, d, first):
        pitch = A_BAND * d
        kv = [None if first else load_kv(r + (c0 - 1) * pitch, d)]
        kv += [load_kv(r + (c0 + j) * pitch, d) for j in range(n)]
        return [(r + (c0 + j) * pitch, kv[j + 1], kv[j]) for j in range(n)]

    def group(r, c0, n, d, mode, first):
        process(jobs_of(r, c0, n, d, first), d, mode)

    def run_pass(d, mode):
        n_chunks = seq // (A_BAND * d)
        n = min(n_chunks, A_CHUNKS_PER_STEP)
        if n_chunks == n:
            per_step = A_CHUNKS_PER_STEP // n

            def step(i, carry):
                jobs = []
                for j in range(per_step):
                    jobs += jobs_of(i * per_step + j, 0, n, d, True)
                process(jobs, d, mode)
                return carry

            lax.fori_loop(0, d // per_step, step, 0)
            return

        def residue(r, carry):
            group(r, 0, n, d, mode, True)

            def step(i, c):
                group(r, i * n, n, d, mode, False)
                return c

            return lax.fori_loop(1, n_chunks // n, step, carry)

        if d == 1:
            residue(0, 0)
        else:
            lax.fori_loop(0, d, residue, 0)

    modes = ("init",) + ("merge",) * (len(A_DILATIONS) - 2) + ("final",)
    for d, mode in zip(A_DILATIONS, modes):
        run_pass(d, mode)


def _attn(q, k, v):
    _, batch, seq, _ = q.shape
    assert seq % (A_BAND * max(A_DILATIONS)) == 0
    spec = pl.BlockSpec((None, None, seq, A_HEAD_DIM), lambda b, h: (h, b, 0, 0))
    return pl.pallas_call(
        functools.partial(_attn_kernel, seq=seq),
        grid=(batch, A_HEADS),
        in_specs=[spec, spec, spec],
        out_specs=spec,
        out_shape=jax.ShapeDtypeStruct(q.shape, BF16),
        scratch_shapes=[pltpu.VMEM((seq, A_HEAD_DIM), F32)] * 3,
        compiler_params=_params("parallel", "parallel"),
        name="dilated_attn",
    )(q, k, v)


def _gelu_tanh(x):
    c1 = -2.0 * 0.7978845608028654 * LOG2_E
    return x / (1.0 + jnp.exp2(x * (c1 + (c1 * 0.044715) * (x * x))))


def _rglru_out_kernel(x_ref, oa_ref, xr_ref, gate_ref, cw_ref, cb_ref, wg_ref, gab_ref, gxb_ref,
                      lam_ref, wout_ref, o_ref, xpad_ref, carry_ref):
    halo = SUBLANES
    group = (SUBLANES, RG_BLOCK_DIM)

    @pl.when(pl.program_id(1) == 0)
    def _():
        xpad_ref[0:halo, :] = jnp.zeros((halo, RG_WIDTH), F32)
        carry_ref[...] = jnp.zeros((SUBLANES, RG_WIDTH), F32)

    xpad_ref[halo:halo + RG_TILE, :] = xr_ref[...]
    w_out = wout_ref[...].astype(BF16)
    wg = [wg_ref[g].astype(BF16) for g in range(RG_BLOCKS)]
    lam = -lam_ref[...]
    softplus = jnp.maximum(lam, 0.0) + jnp.log1p(jnp.exp(-jnp.abs(lam)))
    grouped = (RG_SUB // SUBLANES, SUBLANES, RG_BLOCK_DIM)
    row = lax.broadcasted_iota(jnp.int32, grouped, 1)
    carry = [carry_ref[:, g * RG_BLOCK_DIM:(g + 1) * RG_BLOCK_DIM] for g in range(RG_BLOCKS)]

    for r0 in range(0, RG_TILE, RG_SUB):
        rs = slice(r0, r0 + RG_SUB)
        xc = cb_ref[...] + cw_ref[CONV_WIDTH - 1:CONV_WIDTH, :] * xr_ref[rs, :]
        for j in range(1, CONV_WIDTH):
            xc = xc + (cw_ref[CONV_WIDTH - 1 - j:CONV_WIDTH - j, :]
                       * xpad_ref[pl.ds(halo - j + r0, RG_SUB), :])
        mixed = [oa_ref[hd, rs, :] for hd in range(A_HEADS)]
        for g in range(RG_BLOCKS):
            sl = slice(g * RG_BLOCK_DIM, (g + 1) * RG_BLOCK_DIM)
            xg = xc[:, sl]
            rg = _dot(xg.astype(BF16), wg[g])
            r = jax.nn.sigmoid(rg[:, :RG_BLOCK_DIM] + gab_ref[:, sl])
            i = jax.nn.sigmoid(rg[:, RG_BLOCK_DIM:] + gxb_ref[:, sl])
            log_a = (-RG_C) * r * softplus[:, sl]
            a = jnp.exp(log_a)
            th = jnp.tanh(log_a)
            u = lax.rsqrt((th - 1.0) / (2.0 * th)) * (i * xg)
            a = a.reshape(grouped)
            u = u.reshape(grouped)
            k = 1
            while k < SUBLANES:
                valid = row >= k
                a_prev = pltpu.roll(a, k, 1)
                u_prev = pltpu.roll(u, k, 1)
                u = jnp.where(valid, a * u_prev + u, u)
                a = jnp.where(valid, a * a_prev, a)
                k *= 2
            a = a.reshape(RG_SUB, RG_BLOCK_DIM)
            u = u.reshape(RG_SUB, RG_BLOCK_DIM)
            hs = []
            c = carry[g]
            for gi in range(0, RG_SUB, SUBLANES):
                h = u[gi:gi + SUBLANES, :] + a[gi:gi + SUBLANES, :] * c
                hs.append(h)
                c = jnp.broadcast_to(h[SUBLANES - 1:SUBLANES, :], group)
            carry[g] = c
            mixed.append((jnp.concatenate(hs, axis=0) * _gelu_tanh(gate_ref[rs, sl])).astype(BF16))
        o_ref[rs, :] = x_ref[rs, :] + _dot(jnp.concatenate(mixed, axis=-1), w_out)

    for g in range(RG_BLOCKS):
        carry_ref[:, g * RG_BLOCK_DIM:(g + 1) * RG_BLOCK_DIM] = carry[g]
    xpad_ref[0:halo, :] = xr_ref[RG_TILE - halo:RG_TILE, :]


def _rglru_out(x, o_a, xr, gate, e, conv_w, conv_b, w_gates, ga_b, gx_b, lam, w_out):
    batch, seq, _ = x.shape
    full = pl.BlockSpec((None, RG_TILE, D_MODEL), lambda b, t: (b, t, 0))
    half = pl.BlockSpec((None, RG_TILE, RG_WIDTH), lambda b, t: (b, t, 0))
    vec = pl.BlockSpec((None, 1, RG_WIDTH), lambda b, t: (e, 0, 0))
    return pl.pallas_call(
        _rglru_out_kernel,
        grid=(batch, seq // RG_TILE),
        in_specs=[
            full,
            pl.BlockSpec((A_HEADS, None, RG_TILE, A_HEAD_DIM), lambda b, t: (0, b, t, 0)),
            half, half,
            pl.BlockSpec((None, CONV_WIDTH, RG_WIDTH), lambda b, t: (e, 0, 0)),
            vec,
            pl.BlockSpec((None, RG_BLOCKS, RG_BLOCK_DIM, 2 * RG_BLOCK_DIM), lambda b, t: (e, 0, 0, 0)),
            vec, vec, vec,
            _resident((None, D_MODEL, D_MODEL), lambda b, t: (e, 0, 0)),
        ],
        out_specs=full,
        out_shape=jax.ShapeDtypeStruct(x.shape, F32),
        scratch_shapes=[
            pltpu.VMEM((RG_TILE + SUBLANES, RG_WIDTH), F32),
            pltpu.VMEM((SUBLANES, RG_WIDTH), F32),
        ],
        compiler_params=_params("arbitrary", "arbitrary"),
        name="rglru_outproj",
    )(x, o_a, xr, gate, conv_w, conv_b, w_gates, ga_b, gx_b, lam, w_out)


def kernel(x, mem, mem_norm_g, mix_norm_g, xattn_norm_g, mlp_norm_g, ev_w_in, ev_q_norm_g, ev_k_norm_g, ev_conv_w, ev_conv_b, ev_gate_a_w, ev_gate_a_b, ev_gate_x_w, ev_gate_x_b, ev_lambda, ev_w_out, od_pool_w, od_scale, xa_w_q, xa_w_kv, xa_q_norm_g, xa_k_norm_g, xa_w_o, mlp_w1, mlp_w2):
    batch, seq, _ = x.shape
    rows = batch * seq

    def vec3(a):
        return a.reshape(a.shape[0], 1, a.shape[1])

    w_in, w_out = ev_w_in, ev_w_out
    w_gates = jnp.concatenate([ev_gate_a_w, ev_gate_x_w], axis=-1)
    pool_w = od_pool_w
    w_q, w_kv, w_o = xa_w_q, xa_w_kv, xa_w_o
    w1, w2 = mlp_w1, mlp_w2
    mix_g, xa_g, mlp_g = vec3(mix_norm_g), vec3(xattn_norm_g), vec3(mlp_norm_g)
    ev_qg, ev_kg = vec3(ev_q_norm_g), vec3(ev_k_norm_g)
    xa_qg = vec3(xa_q_norm_g)
    conv_b, ga_b, gx_b, lam = vec3(ev_conv_b), vec3(ev_gate_a_b), vec3(ev_gate_x_b), vec3(ev_lambda)
    od_sc = vec3(od_scale)

    kt, v_mem = _memkv(mem, mem_norm_g, w_kv, xa_k_norm_g)

    for l in range(DEPTH):
        if l % 2 == 0:
            e = l // 2
            q, k, v, xr, gate = _inproj(x.reshape(rows, D_MODEL), l, e, mix_g, w_in, ev_qg, ev_kg)
            half = (batch, seq, MIX_HALF)
            heads = (A_HEADS, batch, seq, A_HEAD_DIM)
            o_a = _attn(q.reshape(heads), k.reshape(heads), v.reshape(heads))
            x = _rglru_out(x, o_a, xr.reshape(half), gate.reshape(half), e, ev_conv_w, conv_b,
                           w_gates, ga_b, gx_b, lam, w_out)
            pool = None
        else:
            pool = (mix_g, pool_w, od_sc)
        x, w1b, w2b = _xattn(x, l, xa_g, w_q, xa_qg, kt, v_mem, w_o, w1, w2, pool)
        x = _mlp(x.reshape(rows, D_MODEL), l, mlp_g, w1b, w2b).reshape(batch, seq, D_MODEL)
    return x
```

```python
import functools

import jax
import jax.numpy as jnp
from jax import lax
from jax.experimental import pallas as pl
from jax.experimental.pallas import tpu as pltpu

D_MODEL = 1024
DEPTH = 4
N_MEM = 256
MIX_HALF = D_MODEL // 2
A_HEADS = 4
A_HEAD_DIM = MIX_HALF // A_HEADS
A_BAND = 128
A_DILATIONS = (16, 4, 1)
A_CHUNKS_PER_STEP = 8
RG_WIDTH = MIX_HALF
RG_BLOCKS = 4
RG_BLOCK_DIM = RG_WIDTH // RG_BLOCKS
RG_C = 8.0
CONV_WIDTH = 4
POOL_WINDOWS = (2, 4, 8, 16)
POOL_GROUP_DIM = D_MODEL // len(POOL_WINDOWS)
POOL_HALO = 16
XA_HEADS = 4
XA_HEAD_DIM = D_MODEL // XA_HEADS
D_FF = 4 * D_MODEL
IN_COLS = 3 * MIX_HALF + 2 * RG_WIDTH
EPS = 1e-6
LOG2_E = 1.4426950408889634

SUBLANES = 8
VMEM_LIMIT = 56 * 1024 * 1024

ROW_TILE = 1024
XA_SUB = 512
MLP_TILE = 1024
MLP_CHUNK = 1024
RG_TILE = 1024
RG_SUB = 256

BF16 = jnp.bfloat16
F32 = jnp.float32


def _rms(x, g):
    ms = jnp.mean(x * x, axis=-1, keepdims=True)
    return x * lax.rsqrt(ms + EPS) * g


def _dot(a, b):
    return jnp.dot(a, b, preferred_element_type=F32)


def _dot_nt(a, b):
    return lax.dot_general(a, b, (((1,), (1,)), ((), ())), preferred_element_type=F32)


def _resident(shape, index_map):
    return pl.BlockSpec(shape, index_map, pipeline_mode=pl.Buffered(1))


def _params(*sem):
    return pltpu.CompilerParams(dimension_semantics=sem, vmem_limit_bytes=VMEM_LIMIT)


def _memkv_kernel(mem_ref, mg_ref, wkv_ref, kg_ref, kt_ref, v_ref):
    batch = mem_ref.shape[0]
    mem = mem_ref[...].reshape(batch * N_MEM, D_MODEL)
    mem_n = _rms(mem, mg_ref[...]).astype(BF16)
    kv = _dot(mem_n, wkv_ref[...].astype(BF16))
    scale = XA_HEAD_DIM ** -0.5 * LOG2_E
    for h in range(XA_HEADS):
        sl = slice(h * XA_HEAD_DIM, (h + 1) * XA_HEAD_DIM)
        kh = _rms(kv[:, sl], kg_ref[...]) * scale
        for b in range(batch):
            kt_ref[b, sl, :] = kh[b * N_MEM:(b + 1) * N_MEM, :].T.astype(BF16)
    v_ref[...] = kv[:, D_MODEL:].astype(BF16).reshape(batch, N_MEM, D_MODEL)


def _memkv(mem, mem_norm_g, w_kv, k_g):
    batch = mem.shape[0]
    return pl.pallas_call(
        _memkv_kernel,
        grid=(DEPTH,),
        in_specs=[
            pl.BlockSpec((batch, N_MEM, D_MODEL), lambda l: (0, 0, 0)),
            pl.BlockSpec((1, D_MODEL), lambda l: (0, 0)),
            pl.BlockSpec((None, D_MODEL, 2 * D_MODEL), lambda l: (l, 0, 0)),
            pl.BlockSpec((None, 1, XA_HEAD_DIM), lambda l: (l, 0, 0)),
        ],
        out_specs=[
            pl.BlockSpec((None, batch, D_MODEL, N_MEM), lambda l: (l, 0, 0, 0)),
            pl.BlockSpec((None, batch, N_MEM, D_MODEL), lambda l: (l, 0, 0, 0)),
        ],
        out_shape=[
            jax.ShapeDtypeStruct((DEPTH, batch, D_MODEL, N_MEM), BF16),
            jax.ShapeDtypeStruct((DEPTH, batch, N_MEM, D_MODEL), BF16),
        ],
        compiler_params=_params("arbitrary"),
        name="memkv",
    )(mem, mem_norm_g.reshape(1, D_MODEL), w_kv, k_g.reshape(DEPTH, 1, XA_HEAD_DIM))


def _pool_rows(x, r0, pos0, mg_ref, pw_ref, sc_ref, hpad_ref):
    rows = x.shape[0]
    h = _rms(x, mg_ref[...])
    hpad_ref[POOL_HALO + r0:POOL_HALO + r0 + rows, :] = h
    pos = pos0 + lax.broadcasted_iota(jnp.int32, (rows, 1), 0)
    s = hpad_ref[r0:r0 + POOL_HALO + rows, :]
    outs = []
    width = 1
    for gi, w in enumerate(POOL_WINDOWS):
        while width < w:
            s = s + pltpu.roll(s, width, 0)
            width *= 2
        cnt = jnp.minimum(pos + 1, w).astype(F32)
        d = s[POOL_HALO:, :POOL_GROUP_DIM] / cnt - h[:, gi * POOL_GROUP_DIM:(gi + 1) * POOL_GROUP_DIM]
        outs.append(_dot(d.astype(BF16), pw_ref[gi].astype(BF16)))
        if gi + 1 < len(POOL_WINDOWS):
            s = s[:, POOL_GROUP_DIM:]
    return x + jnp.concatenate(outs, axis=-1) * sc_ref[...]


def _xattn_kernel(*refs, pool):
    if pool:
        (x_ref, mg_ref, pw_ref, sc_ref, g_ref, wq_ref, qg_ref, kt_ref, v_ref, wo_ref, w1p_ref, w2p_ref,
         o_ref, w1b_ref, w2b_ref, hpad_ref) = refs
        t = pl.program_id(1)

        @pl.when(t == 0)
        def _():
            hpad_ref[0:POOL_HALO, :] = jnp.zeros((POOL_HALO, D_MODEL), F32)
    else:
        (x_ref, g_ref, wq_ref, qg_ref, kt_ref, v_ref, wo_ref, w1p_ref, w2p_ref,
         o_ref, w1b_ref, w2b_ref) = refs
    w1b_ref[...] = w1p_ref[...].astype(BF16)
    w2b_ref[...] = w2p_ref[...].astype(BF16)
    w_q = wq_ref[...].astype(BF16)
    w_o = wo_ref[...].astype(BF16)
    heads = [slice(hd * XA_HEAD_DIM, (hd + 1) * XA_HEAD_DIM) for hd in range(XA_HEADS)]
    for r0 in range(0, ROW_TILE, XA_SUB):
        rs = slice(r0, r0 + XA_SUB)
        x = x_ref[rs, :]
        if pool:
            x = _pool_rows(x, r0, t * ROW_TILE + r0, mg_ref, pw_ref, sc_ref, hpad_ref)
        h = _rms(x, g_ref[...]).astype(BF16)
        qs = [_dot(h, w_q[:, sl]) for sl in heads]
        qs = [_rms(q, qg_ref[...]).astype(BF16) for q in qs]
        ss = [_dot(q, kt_ref[sl, :]) for q, sl in zip(qs, heads)]
        ps = [jnp.exp2(s - jnp.max(s, axis=-1, keepdims=True)) for s in ss]
        dens = [jnp.sum(p, axis=-1, keepdims=True) for p in ps]
        os_ = [_dot(p.astype(BF16), v_ref[:, sl]) for p, sl in zip(ps, heads)]
        os_ = [(o / den).astype(BF16) for o, den in zip(os_, dens)]
        out = x
        for o, sl in zip(os_, heads):
            out = out + _dot(o, w_o[sl, :])
        o_ref[rs, :] = out
    if pool:
        hpad_ref[0:POOL_HALO, :] = hpad_ref[ROW_TILE:ROW_TILE + POOL_HALO, :]


def _xattn(x, layer, g, w_q, q_g, kt, v, w_o, w1, w2, pool=None):
    batch, seq, _ = x.shape
    n_t = seq // ROW_TILE
    steps = batch * n_t
    tile = pl.BlockSpec((None, ROW_TILE, D_MODEL), lambda b, t: (b, t, 0))
    in_specs = [
        pl.BlockSpec((None, 1, D_MODEL), lambda b, t: (layer, 0, 0)),
        _resident((None, D_MODEL, D_MODEL), lambda b, t: (layer, 0, 0)),
        pl.BlockSpec((None, 1, XA_HEAD_DIM), lambda b, t: (layer, 0, 0)),
        pl.BlockSpec((None, None, D_MODEL, N_MEM), lambda b, t: (layer, b, 0, 0)),
        pl.BlockSpec((None, None, N_MEM, D_MODEL), lambda b, t: (layer, b, 0, 0)),
        _resident((None, D_MODEL, D_MODEL), lambda b, t: (layer, 0, 0)),
        pl.BlockSpec((None, D_MODEL // steps, D_FF), lambda b, t: (layer, b * n_t + t, 0)),
        pl.BlockSpec((None, D_FF // steps, D_MODEL), lambda b, t: (layer, b * n_t + t, 0)),
    ]
    args = (g, w_q, q_g, kt, v, w_o, w1, w2)
    scratch = []
    if pool is not None:
        o = layer // 2
        in_specs = [
            pl.BlockSpec((None, 1, D_MODEL), lambda b, t: (layer, 0, 0)),
            pl.BlockSpec((None, len(POOL_WINDOWS), POOL_GROUP_DIM, POOL_GROUP_DIM),
                         lambda b, t: (o, 0, 0, 0)),
            pl.BlockSpec((None, 1, D_MODEL), lambda b, t: (o, 0, 0)),
        ] + in_specs
        args = tuple(pool) + args
        scratch = [pltpu.VMEM((POOL_HALO + ROW_TILE, D_MODEL), F32)]
    return pl.pallas_call(
        functools.partial(_xattn_kernel, pool=pool is not None),
        grid=(batch, seq // ROW_TILE),
        in_specs=[tile] + in_specs,
        out_specs=[
            tile,
            pl.BlockSpec((D_MODEL // steps, D_FF), lambda b, t: (b * n_t + t, 0)),
            pl.BlockSpec((D_FF // steps, D_MODEL), lambda b, t: (b * n_t + t, 0)),
        ],
        out_shape=[
            jax.ShapeDtypeStruct(x.shape, F32),
            jax.ShapeDtypeStruct((D_MODEL, D_FF), BF16),
            jax.ShapeDtypeStruct((D_FF, D_MODEL), BF16),
        ],
        scratch_shapes=scratch,
        compiler_params=_params("parallel", "arbitrary" if pool is not None else "parallel"),
        name="xattn" if pool is None else "pool_xattn",
    )(x, *args)


def _mlp_kernel(x_ref, g_ref, w1_ref, w2_ref, o_ref):
    x = x_ref[...]
    h = _rms(x, g_ref[...]).astype(BF16)
    acc = x
    for c in range(D_FF // MLP_CHUNK):
        sl = slice(c * MLP_CHUNK, (c + 1) * MLP_CHUNK)
        a = jnp.maximum(_dot(h, w1_ref[:, sl]), 0.0)
        acc = acc + _dot((a * a).astype(BF16), w2_ref[sl, :])
    o_ref[...] = acc


def _mlp(x2, layer, g, w1b, w2b):
    rows = x2.shape[0]
    return pl.pallas_call(
        _mlp_kernel,
        grid=(rows // MLP_TILE,),
        in_specs=[
            pl.BlockSpec((MLP_TILE, D_MODEL), lambda i: (i, 0)),
            pl.BlockSpec((None, 1, D_MODEL), lambda i: (layer, 0, 0)),
            _resident((D_MODEL, D_FF), lambda i: (0, 0)),
            _resident((D_FF, D_MODEL), lambda i: (0, 0)),
        ],
        out_specs=pl.BlockSpec((MLP_TILE, D_MODEL), lambda i: (i, 0)),
        out_shape=jax.ShapeDtypeStruct(x2.shape, F32),
        compiler_params=_params("parallel"),
        name="mlp",
    )(x2, g, w1b, w2b)


def _inproj_kernel(x_ref, g_ref, w_ref, qg_ref, kg_ref, q_ref, k_ref, v_ref, xr_ref, gate_ref):
    h = _rms(x_ref[...], g_ref[...]).astype(BF16)
    scale = A_HEAD_DIM ** -0.5 * LOG2_E
    for j, (out_ref, gain_ref, mult) in (
            (4, (gate_ref, None, None)), (0, (q_ref, qg_ref, scale)), (1, (k_ref, kg_ref, 1.0)),
            (2, (v_ref, None, None)), (3, (xr_ref, None, None))):
        p = _dot(h, w_ref[:, j * MIX_HALF:(j + 1) * MIX_HALF].astype(BF16))
        if j >= 3:
            out_ref[...] = p if j == 3 else _gelu_tanh(p)
            continue
        for hd in range(A_HEADS):
            ph = p[:, hd * A_HEAD_DIM:(hd + 1) * A_HEAD_DIM]
            out_ref[hd] = ph if gain_ref is None else _rms(ph, gain_ref[...]) * mult


def _inproj(x2, layer, e, g, w_in, q_g, k_g):
    rows = x2.shape[0]
    half = jax.ShapeDtypeStruct((rows, MIX_HALF), F32)
    half_spec = pl.BlockSpec((ROW_TILE, MIX_HALF), lambda i: (i, 0))
    heads = jax.ShapeDtypeStruct((A_HEADS, rows, A_HEAD_DIM), F32)
    heads_spec = pl.BlockSpec((A_HEADS, ROW_TILE, A_HEAD_DIM), lambda i: (0, i, 0))
    gain_spec = pl.BlockSpec((None, 1, A_HEAD_DIM), lambda i: (e, 0, 0))
    return pl.pallas_call(
        _inproj_kernel,
        grid=(rows // ROW_TILE,),
        in_specs=[
            pl.BlockSpec((ROW_TILE, D_MODEL), lambda i: (i, 0)),
            pl.BlockSpec((None, 1, D_MODEL), lambda i: (layer, 0, 0)),
            _resident((None, D_MODEL, IN_COLS), lambda i: (e, 0, 0)),
            gain_spec, gain_spec,
        ],
        out_specs=[heads_spec] * 3 + [half_spec] * 2,
        out_shape=[heads] * 3 + [half] * 2,
        compiler_params=_params("parallel"),
        name="inproj",
    )(x2, g, w_in, q_g, k_g)


def _attn_kernel(q_ref, k_ref, v_ref, o_ref, m_ref, l_ref, acc_ref, *, seq):
    jq = lax.broadcasted_iota(jnp.int32, (A_BAND, A_BAND), 0)
    jk = lax.broadcasted_iota(jnp.int32, (A_BAND, A_BAND), 1)
    own_mask = jk <= jq
    jk2 = lax.broadcasted_iota(jnp.int32, (A_BAND, 2 * A_BAND), 1)
    jq2 = lax.broadcasted_iota(jnp.int32, (A_BAND, 2 * A_BAND), 0)
    both_mask = (jk2 >= jq2) & (jk2 <= jq2 + A_BAND)
    neg = -jnp.inf
    lanes = (A_BAND, A_HEAD_DIM)
    ones = jnp.ones(lanes, BF16)

    def rows(start, d):
        if d == 1:
            return pl.ds(pl.multiple_of(start, A_BAND), A_BAND)
        return pl.ds(start, A_BAND, stride=d)

    def load_kv(start, d):
        r = rows(start, d)
        return (k_ref[r, :].astype(BF16),
                jnp.concatenate([v_ref[r, :].astype(BF16), ones], axis=1))

    def process(jobs, d, mode):
        rs = [rows(start, d) for start, _, _ in jobs]
        qs = [q_ref[r, :].astype(BF16) for r in rs]
        scores = []
        for q, (_, own, prev) in zip(qs, jobs):
            if prev is None:
                s = jnp.where(own_mask, _dot_nt(q, own[0]), neg)
            else:
                s = _dot_nt(q, jnp.concatenate([prev[0], own[0]], axis=0))
                s = jnp.where(both_mask, s, neg)
            scores.append(s)
        ms = [jnp.broadcast_to(jnp.max(s, axis=-1, keepdims=True), lanes) for s in scores]
        if mode != "init":
            m_olds = [m_ref[r, :] for r in rs]
            ms = [jnp.maximum(m_old, m) for m_old, m in zip(m_olds, ms)]
        pvs = []
        for s, m, (_, own, prev) in zip(scores, ms, jobs):
            if prev is None:
                v = own[1]
            else:
                v = jnp.concatenate([prev[1], own[1]], axis=0)
                m = jnp.concatenate([m, m], axis=1)
            pvs.append(_dot(jnp.exp2(s - m).astype(BF16), v))
        for i, (r, m, pv) in enumerate(zip(rs, ms, pvs)):
            acc, den = pv[:, :A_HEAD_DIM], pv[:, A_HEAD_DIM:]
            if mode != "init":
                a_old = jnp.exp2(m_olds[i] - m)
                den = a_old * l_ref[r, :] + den
                acc = a_old * acc_ref[r, :] + acc
            if mode == "final":
                o_ref[r, :] = (acc / den).astype(o_ref.dtype)
            else:
                m_ref[r, :] = m
                l_ref[r, :] = den
                acc_ref[r, :] = acc

    def jobs_of(r, c0, n, d, first):
        pitch = A_BAND * d
        kv = [None if first else load_kv(r + (c0 - 1) * pitch, d)]
        kv += [load_kv(r + (c0 + j) * pitch, d) for j in range(n)]
        return [(r + (c0 + j) * pitch, kv[j + 1], kv[j]) for j in range(n)]

    def group(r, c0, n, d, mode, first):
        process(jobs_of(r, c0, n, d, first), d, mode)

    def run_pass(d, mode):
        n_chunks = seq // (A_BAND * d)
        n = min(n_chunks, A_CHUNKS_PER_STEP)
        if n_chunks == n:
            per_step = A_CHUNKS_PER_STEP // n

            def step(i, carry):
                jobs = []
                for j in range(per_step):
                    jobs += jobs_of(i * per_step + j, 0, n, d, True)
                process(jobs, d, mode)
                return carry

            lax.fori_loop(0, d // per_step, step, 0)
            return

        def residue(r, carry):
            group(r, 0, n, d, mode, True)

            def step(i, c):
                group(r, i * n, n, d, mode, False)
                return c

            return lax.fori_loop(1, n_chunks // n, step, carry)

        if d == 1:
            residue(0, 0)
        else:
            lax.fori_loop(0, d, residue, 0)

    modes = ("init",) + ("merge",) * (len(A_DILATIONS) - 2) + ("final",)
    for d, mode in zip(A_DILATIONS, modes):
        run_pass(d, mode)


def _attn(q, k, v):
    _, batch, seq, _ = q.shape
    assert seq % (A_BAND * max(A_DILATIONS)) == 0
    spec = pl.BlockSpec((None, None, seq, A_HEAD_DIM), lambda b, h: (h, b, 0, 0))
    return pl.pallas_call(
        functools.partial(_attn_kernel, seq=seq),
        grid=(batch, A_HEADS),
        in_specs=[spec, spec, spec],
        out_specs=spec,
        out_shape=jax.ShapeDtypeStruct(q.shape, BF16),
        scratch_shapes=[pltpu.VMEM((seq, A_HEAD_DIM), F32)] * 3,
        compiler_params=_params("parallel", "parallel"),
        name="dilated_attn",
    )(q, k, v)


def _gelu_tanh(x):
    c1 = -2.0 * 0.7978845608028654 * LOG2_E
    return x / (1.0 + jnp.exp2(x * (c1 + (c1 * 0.044715) * (x * x))))


def _rglru_out_kernel(x_ref, oa_ref, xr_ref, gate_ref, cw_ref, cb_ref, wg_ref, gab_ref, gxb_ref,
                      lam_ref, wout_ref, o_ref, xpad_ref, carry_ref):
    halo = SUBLANES
    group = (SUBLANES, RG_BLOCK_DIM)

    @pl.when(pl.program_id(1) == 0)
    def _():
        xpad_ref[0:halo, :] = jnp.zeros((halo, RG_WIDTH), F32)
        carry_ref[...] = jnp.zeros((SUBLANES, RG_WIDTH), F32)

    xpad_ref[halo:halo + RG_TILE, :] = xr_ref[...]
    w_out = wout_ref[...].astype(BF16)
    wg = [wg_ref[g].astype(BF16) for g in range(RG_BLOCKS)]
    lam = -lam_ref[...]
    softplus = jnp.maximum(lam, 0.0) + jnp.log1p(jnp.exp(-jnp.abs(lam)))
    grouped = (RG_SUB // SUBLANES, SUBLANES, RG_BLOCK_DIM)
    row = lax.broadcasted_iota(jnp.int32, grouped, 1)
    carry = [carry_ref[:, g * RG_BLOCK_DIM:(g + 1) * RG_BLOCK_DIM] for g in range(RG_BLOCKS)]

    for r0 in range(0, RG_TILE, RG_SUB):
        rs = slice(r0, r0 + RG_SUB)
        xc = cb_ref[...] + cw_ref[CONV_WIDTH - 1:CONV_WIDTH, :] * xr_ref[rs, :]
        for j in range(1, CONV_WIDTH):
            xc = xc + (cw_ref[CONV_WIDTH - 1 - j:CONV_WIDTH - j, :]
                       * xpad_ref[pl.ds(halo - j + r0, RG_SUB), :])
        mixed = [oa_ref[hd, rs, :] for hd in range(A_HEADS)]
        for g in range(RG_BLOCKS):
            sl = slice(g * RG_BLOCK_DIM, (g + 1) * RG_BLOCK_DIM)
            xg = xc[:, sl]
            rg = _dot(xg.astype(BF16), wg[g])
            r = jax.nn.sigmoid(rg[:, :RG_BLOCK_DIM] + gab_ref[:, sl])
            i = jax.nn.sigmoid(rg[:, RG_BLOCK_DIM:] + gxb_ref[:, sl])
            log_a = (-RG_C) * r * softplus[:, sl]
            a = jnp.exp(log_a)
            th = jnp.tanh(log_a)
            u = lax.rsqrt((th - 1.0) / (2.0 * th)) * (i * xg)
            a = a.reshape(grouped)
            u = u.reshape(grouped)
            k = 1
            while k < SUBLANES:
                valid = row >= k
                a_prev = pltpu.roll(a, k, 1)
                u_prev = pltpu.roll(u, k, 1)
                u = jnp.where(valid, a * u_prev + u, u)
                a = jnp.where(valid, a * a_prev, a)
                k *= 2
            a = a.reshape(RG_SUB, RG_BLOCK_DIM)
            u = u.reshape(RG_SUB, RG_BLOCK_DIM)
            hs = []
            c = carry[g]
            for gi in range(0, RG_SUB, SUBLANES):
                h = u[gi:gi + SUBLANES, :] + a[gi:gi + SUBLANES, :] * c
                hs.append(h)
                c = jnp.broadcast_to(h[SUBLANES - 1:SUBLANES, :], group)
            carry[g] = c
            mixed.append((jnp.concatenate(hs, axis=0) * gate_ref[rs, sl]).astype(BF16))
        o_ref[rs, :] = x_ref[rs, :] + _dot(jnp.concatenate(mixed, axis=-1), w_out)

    for g in range(RG_BLOCKS):
        carry_ref[:, g * RG_BLOCK_DIM:(g + 1) * RG_BLOCK_DIM] = carry[g]
    xpad_ref[0:halo, :] = xr_ref[RG_TILE - halo:RG_TILE, :]


def _rglru_out(x, o_a, xr, gate, e, conv_w, conv_b, w_gates, ga_b, gx_b, lam, w_out):
    batch, seq, _ = x.shape
    full = pl.BlockSpec((None, RG_TILE, D_MODEL), lambda b, t: (b, t, 0))
    half = pl.BlockSpec((None, RG_TILE, RG_WIDTH), lambda b, t: (b, t, 0))
    vec = pl.BlockSpec((None, 1, RG_WIDTH), lambda b, t: (e, 0, 0))
    return pl.pallas_call(
        _rglru_out_kernel,
        grid=(batch, seq // RG_TILE),
        in_specs=[
            full,
            pl.BlockSpec((A_HEADS, None, RG_TILE, A_HEAD_DIM), lambda b, t: (0, b, t, 0)),
            half, half,
            pl.BlockSpec((None, CONV_WIDTH, RG_WIDTH), lambda b, t: (e, 0, 0)),
            vec,
            pl.BlockSpec((None, RG_BLOCKS, RG_BLOCK_DIM, 2 * RG_BLOCK_DIM), lambda b, t: (e, 0, 0, 0)),
            vec, vec, vec,
            _resident((None, D_MODEL, D_MODEL), lambda b, t: (e, 0, 0)),
        ],
        out_specs=full,
        out_shape=jax.ShapeDtypeStruct(x.shape, F32),
        scratch_shapes=[
            pltpu.VMEM((RG_TILE + SUBLANES, RG_WIDTH), F32),
            pltpu.VMEM((SUBLANES, RG_WIDTH), F32),
        ],
        compiler_params=_params("arbitrary", "arbitrary"),
        name="rglru_outproj",
    )(x, o_a, xr, gate, conv_w, conv_b, w_gates, ga_b, gx_b, lam, w_out)


def kernel(x, mem, mem_norm_g, mix_norm_g, xattn_norm_g, mlp_norm_g, ev_w_in, ev_q_norm_g, ev_k_norm_g, ev_conv_w, ev_conv_b, ev_gate_a_w, ev_gate_a_b, ev_gate_x_w, ev_gate_x_b, ev_lambda, ev_w_out, od_pool_w, od_scale, xa_w_q, xa_w_kv, xa_q_norm_g, xa_k_norm_g, xa_w_o, mlp_w1, mlp_w2):
    batch, seq, _ = x.shape
    rows = batch * seq

    def vec3(a):
        return a.reshape(a.shape[0], 1, a.shape[1])

    w_in, w_out = ev_w_in, ev_w_out
    w_gates = jnp.concatenate([ev_gate_a_w, ev_gate_x_w], axis=-1)
    pool_w = od_pool_w
    w_q, w_kv, w_o = xa_w_q, xa_w_kv, xa_w_o
    w1, w2 = mlp_w1, mlp_w2
    mix_g, xa_g, mlp_g = vec3(mix_norm_g), vec3(xattn_norm_g), vec3(mlp_norm_g)
    ev_qg, ev_kg = vec3(ev_q_norm_g), vec3(ev_k_norm_g)
    xa_qg = vec3(xa_q_norm_g)
    conv_b, ga_b, gx_b, lam = vec3(ev_conv_b), vec3(ev_gate_a_b), vec3(ev_gate_x_b), vec3(ev_lambda)
    od_sc = vec3(od_scale)

    kt, v_mem = _memkv(mem, mem_norm_g, w_kv, xa_k_norm_g)

    for l in range(DEPTH):
        if l % 2 == 0:
            e = l // 2
            q, k, v, xr, gate = _inproj(x.reshape(rows, D_MODEL), l, e, mix_g, w_in, ev_qg, ev_kg)
            half = (batch, seq, MIX_HALF)
            heads = (A_HEADS, batch, seq, A_HEAD_DIM)
            o_a = _attn(q.reshape(heads), k.reshape(heads), v.reshape(heads))
            x = _rglru_out(x, o_a, xr.reshape(half), gate.reshape(half), e, ev_conv_w, conv_b,
                           w_gates, ga_b, gx_b, lam, w_out)
            pool = None
        else:
            pool = (mix_g, pool_w, od_sc)
        x, w1b, w2b = _xattn(x, l, xa_g, w_q, xa_qg, kt, v_mem, w_o, w1, w2, pool)
        x = _mlp(x.reshape(rows, D_MODEL), l, mlp_g, w1b, w2b).reshape(batch, seq, D_MODEL)
    return x
```

```python
import functools

import jax
import jax.numpy as jnp
from jax import lax
from jax.experimental import pallas as pl
from jax.experimental.pallas import tpu as pltpu

D_MODEL = 1024
DEPTH = 4
N_MEM = 256
MIX_HALF = D_MODEL // 2
A_HEADS = 4
A_HEAD_DIM = MIX_HALF // A_HEADS
A_BAND = 128
A_DILATIONS = (16, 4, 1)
A_CHUNKS_PER_STEP = {16: 8, 4: 8, 1: 16}
RG_WIDTH = MIX_HALF
RG_BLOCKS = 4
RG_BLOCK_DIM = RG_WIDTH // RG_BLOCKS
RG_C = 8.0
CONV_WIDTH = 4
POOL_WINDOWS = (2, 4, 8, 16)
POOL_GROUP_DIM = D_MODEL // len(POOL_WINDOWS)
POOL_HALO = 16
XA_HEADS = 4
XA_HEAD_DIM = D_MODEL // XA_HEADS
D_FF = 4 * D_MODEL
IN_COLS = 3 * MIX_HALF + 2 * RG_WIDTH
EPS = 1e-6
LOG2_E = 1.4426950408889634

SUBLANES = 8
VMEM_LIMIT = 56 * 1024 * 1024

ROW_TILE = 1024
XA_SUB = 512
MLP_TILE = 1024
MLP_CHUNK = 1024
RG_TILE = 1024
RG_SUB = 256

BF16 = jnp.bfloat16
F32 = jnp.float32


def _rms(x, g):
    ms = jnp.mean(x * x, axis=-1, keepdims=True)
    return x * lax.rsqrt(ms + EPS) * g


def _dot(a, b):
    return jnp.dot(a, b, preferred_element_type=F32)


def _dot_nt(a, b):
    return lax.dot_general(a, b, (((1,), (1,)), ((), ())), preferred_element_type=F32)


def _resident(shape, index_map):
    return pl.BlockSpec(shape, index_map, pipeline_mode=pl.Buffered(1))


def _params(*sem):
    return pltpu.CompilerParams(dimension_semantics=sem, vmem_limit_bytes=VMEM_LIMIT)


def _memkv_kernel(mem_ref, mg_ref, wkv_ref, kg_ref, kt_ref, v_ref):
    batch = mem_ref.shape[0]
    mem = mem_ref[...].reshape(batch * N_MEM, D_MODEL)
    mem_n = _rms(mem, mg_ref[...]).astype(BF16)
    kv = _dot(mem_n, wkv_ref[...].astype(BF16))
    scale = XA_HEAD_DIM ** -0.5 * LOG2_E
    for h in range(XA_HEADS):
        sl = slice(h * XA_HEAD_DIM, (h + 1) * XA_HEAD_DIM)
        kh = _rms(kv[:, sl], kg_ref[...]) * scale
        for b in range(batch):
            kt_ref[b, sl, :] = kh[b * N_MEM:(b + 1) * N_MEM, :].T.astype(BF16)
    v_ref[...] = kv[:, D_MODEL:].astype(BF16).reshape(batch, N_MEM, D_MODEL)


def _memkv(mem, mem_norm_g, w_kv, k_g):
    batch = mem.shape[0]
    return pl.pallas_call(
        _memkv_kernel,
        grid=(DEPTH,),
        in_specs=[
            pl.BlockSpec((batch, N_MEM, D_MODEL), lambda l: (0, 0, 0)),
            pl.BlockSpec((1, D_MODEL), lambda l: (0, 0)),
            pl.BlockSpec((None, D_MODEL, 2 * D_MODEL), lambda l: (l, 0, 0)),
            pl.BlockSpec((None, 1, XA_HEAD_DIM), lambda l: (l, 0, 0)),
        ],
        out_specs=[
            pl.BlockSpec((None, batch, D_MODEL, N_MEM), lambda l: (l, 0, 0, 0)),
            pl.BlockSpec((None, batch, N_MEM, D_MODEL), lambda l: (l, 0, 0, 0)),
        ],
        out_shape=[
            jax.ShapeDtypeStruct((DEPTH, batch, D_MODEL, N_MEM), BF16),
            jax.ShapeDtypeStruct((DEPTH, batch, N_MEM, D_MODEL), BF16),
        ],
        compiler_params=_params("arbitrary"),
        name="memkv",
    )(mem, mem_norm_g.reshape(1, D_MODEL), w_kv, k_g.reshape(DEPTH, 1, XA_HEAD_DIM))


def _pool_rows(x, r0, pos0, mg_ref, pw_ref, sc_ref, hpad_ref):
    rows = x.shape[0]
    h = _rms(x, mg_ref[...])
    hpad_ref[POOL_HALO + r0:POOL_HALO + r0 + rows, :] = h
    pos = pos0 + lax.broadcasted_iota(jnp.int32, (rows, 1), 0)
    s = hpad_ref[r0:r0 + POOL_HALO + rows, :]
    outs = []
    width = 1
    for gi, w in enumerate(POOL_WINDOWS):
        while width < w:
            s = s + pltpu.roll(s, width, 0)
            width *= 2
        cnt = jnp.minimum(pos + 1, w).astype(F32)
        d = s[POOL_HALO:, :POOL_GROUP_DIM] / cnt - h[:, gi * POOL_GROUP_DIM:(gi + 1) * POOL_GROUP_DIM]
        outs.append(_dot(d.astype(BF16), pw_ref[gi].astype(BF16)))
        if gi + 1 < len(POOL_WINDOWS):
            s = s[:, POOL_GROUP_DIM:]
    return x + jnp.concatenate(outs, axis=-1) * sc_ref[...]


def _xattn_kernel(*refs, pool):
    if pool:
        (x_ref, mg_ref, pw_ref, sc_ref, g_ref, wq_ref, qg_ref, kt_ref, v_ref, wo_ref, w1p_ref, w2p_ref,
         o_ref, w1b_ref, w2b_ref, hpad_ref) = refs
        t = pl.program_id(1)

        @pl.when(t == 0)
        def _():
            hpad_ref[0:POOL_HALO, :] = jnp.zeros((POOL_HALO, D_MODEL), F32)
    else:
        (x_ref, g_ref, wq_ref, qg_ref, kt_ref, v_ref, wo_ref, w1p_ref, w2p_ref,
         o_ref, w1b_ref, w2b_ref) = refs
    w1b_ref[...] = w1p_ref[...].astype(BF16)
    w2b_ref[...] = w2p_ref[...].astype(BF16)
    w_q = wq_ref[...].astype(BF16)
    w_o = wo_ref[...].astype(BF16)
    heads = [slice(hd * XA_HEAD_DIM, (hd + 1) * XA_HEAD_DIM) for hd in range(XA_HEADS)]
    for r0 in range(0, ROW_TILE, XA_SUB):
        rs = slice(r0, r0 + XA_SUB)
        x = x_ref[rs, :]
        if pool:
            x = _pool_rows(x, r0, t * ROW_TILE + r0, mg_ref, pw_ref, sc_ref, hpad_ref)
        h = _rms(x, g_ref[...]).astype(BF16)
        qs = [_dot(h, w_q[:, sl]) for sl in heads]
        qs = [_rms(q, qg_ref[...]).astype(BF16) for q in qs]
        ss = [_dot(q, kt_ref[sl, :]) for q, sl in zip(qs, heads)]
        ps = [jnp.exp2(s - jnp.max(s, axis=-1, keepdims=True)) for s in ss]
        dens = [jnp.sum(p, axis=-1, keepdims=True) for p in ps]
        os_ = [_dot(p.astype(BF16), v_ref[:, sl]) for p, sl in zip(ps, heads)]
        os_ = [(o / den).astype(BF16) for o, den in zip(os_, dens)]
        out = x
        for o, sl in zip(os_, heads):
            out = out + _dot(o, w_o[sl, :])
        o_ref[rs, :] = out
    if pool:
        hpad_ref[0:POOL_HALO, :] = hpad_ref[ROW_TILE:ROW_TILE + POOL_HALO, :]


def _xattn(x, layer, g, w_q, q_g, kt, v, w_o, w1, w2, pool=None):
    batch, seq, _ = x.shape
    n_t = seq // ROW_TILE
    steps = batch * n_t
    tile = pl.BlockSpec((None, ROW_TILE, D_MODEL), lambda b, t: (b, t, 0))
    in_specs = [
        pl.BlockSpec((None, 1, D_MODEL), lambda b, t: (layer, 0, 0)),
        _resident((None, D_MODEL, D_MODEL), lambda b, t: (layer, 0, 0)),
        pl.BlockSpec((None, 1, XA_HEAD_DIM), lambda b, t: (layer, 0, 0)),
        pl.BlockSpec((None, None, D_MODEL, N_MEM), lambda b, t: (layer, b, 0, 0)),
        pl.BlockSpec((None, None, N_MEM, D_MODEL), lambda b, t: (layer, b, 0, 0)),
        _resident((None, D_MODEL, D_MODEL), lambda b, t: (layer, 0, 0)),
        pl.BlockSpec((None, D_MODEL // steps, D_FF), lambda b, t: (layer, b * n_t + t, 0)),
        pl.BlockSpec((None, D_FF // steps, D_MODEL), lambda b, t: (layer, b * n_t + t, 0)),
    ]
    args = (g, w_q, q_g, kt, v, w_o, w1, w2)
    scratch = []
    if pool is not None:
        o = layer // 2
        in_specs = [
            pl.BlockSpec((None, 1, D_MODEL), lambda b, t: (layer, 0, 0)),
            pl.BlockSpec((None, len(POOL_WINDOWS), POOL_GROUP_DIM, POOL_GROUP_DIM),
                         lambda b, t: (o, 0, 0, 0)),
            pl.BlockSpec((None, 1, D_MODEL), lambda b, t: (o, 0, 0)),
        ] + in_specs
        args = tuple(pool) + args
        scratch = [pltpu.VMEM((POOL_HALO + ROW_TILE, D_MODEL), F32)]
    return pl.pallas_call(
        functools.partial(_xattn_kernel, pool=pool is not None),
        grid=(batch, seq // ROW_TILE),
        in_specs=[tile] + in_specs,
        out_specs=[
            tile,
            pl.BlockSpec((D_MODEL // steps, D_FF), lambda b, t: (b * n_t + t, 0)),
            pl.BlockSpec((D_FF // steps, D_MODEL), lambda b, t: (b * n_t + t, 0)),
        ],
        out_shape=[
            jax.ShapeDtypeStruct(x.shape, F32),
            jax.ShapeDtypeStruct((D_MODEL, D_FF), BF16),
            jax.ShapeDtypeStruct((D_FF, D_MODEL), BF16),
        ],
        scratch_shapes=scratch,
        compiler_params=_params("parallel", "arbitrary" if pool is not None else "parallel"),
        name="xattn" if pool is None else "pool_xattn",
    )(x, *args)


def _mlp_kernel(x_ref, g_ref, w1_ref, w2_ref, o_ref):
    x = x_ref[...]
    h = _rms(x, g_ref[...]).astype(BF16)
    acc = x
    for c in range(D_FF // MLP_CHUNK):
        sl = slice(c * MLP_CHUNK, (c + 1) * MLP_CHUNK)
        a = jnp.maximum(_dot(h, w1_ref[:, sl]), 0.0)
        acc = acc + _dot((a * a).astype(BF16), w2_ref[sl, :])
    o_ref[...] = acc


def _mlp(x2, layer, g, w1b, w2b):
    rows = x2.shape[0]
    return pl.pallas_call(
        _mlp_kernel,
        grid=(rows // MLP_TILE,),
        in_specs=[
            pl.BlockSpec((MLP_TILE, D_MODEL), lambda i: (i, 0)),
            pl.BlockSpec((None, 1, D_MODEL), lambda i: (layer, 0, 0)),
            _resident((D_MODEL, D_FF), lambda i: (0, 0)),
            _resident((D_FF, D_MODEL), lambda i: (0, 0)),
        ],
        out_specs=pl.BlockSpec((MLP_TILE, D_MODEL), lambda i: (i, 0)),
        out_shape=jax.ShapeDtypeStruct(x2.shape, F32),
        compiler_params=_params("parallel"),
        name="mlp",
    )(x2, g, w1b, w2b)


def _inproj_kernel(x_ref, g_ref, w_ref, qg_ref, kg_ref, q_ref, k_ref, v_ref, xr_ref, gate_ref):
    h = _rms(x_ref[...], g_ref[...]).astype(BF16)
    scale = A_HEAD_DIM ** -0.5 * LOG2_E
    for j, (out_ref, gain_ref, mult) in (
            (4, (gate_ref, None, None)), (0, (q_ref, qg_ref, scale)), (1, (k_ref, kg_ref, 1.0)),
            (2, (v_ref, None, None)), (3, (xr_ref, None, None))):
        p = _dot(h, w_ref[:, j * MIX_HALF:(j + 1) * MIX_HALF].astype(BF16))
        if j >= 3:
            out_ref[...] = p if j == 3 else _gelu_tanh(p).astype(out_ref.dtype)
            continue
        for hd in range(A_HEADS):
            ph = p[:, hd * A_HEAD_DIM:(hd + 1) * A_HEAD_DIM]
            out_ref[hd] = ph if gain_ref is None else _rms(ph, gain_ref[...]) * mult


def _inproj(x2, layer, e, g, w_in, q_g, k_g):
    rows = x2.shape[0]
    half = jax.ShapeDtypeStruct((rows, MIX_HALF), F32)
    half_spec = pl.BlockSpec((ROW_TILE, MIX_HALF), lambda i: (i, 0))
    heads = jax.ShapeDtypeStruct((A_HEADS, rows, A_HEAD_DIM), F32)
    heads_spec = pl.BlockSpec((A_HEADS, ROW_TILE, A_HEAD_DIM), lambda i: (0, i, 0))
    gain_spec = pl.BlockSpec((None, 1, A_HEAD_DIM), lambda i: (e, 0, 0))
    return pl.pallas_call(
        _inproj_kernel,
        grid=(rows // ROW_TILE,),
        in_specs=[
            pl.BlockSpec((ROW_TILE, D_MODEL), lambda i: (i, 0)),
            pl.BlockSpec((None, 1, D_MODEL), lambda i: (layer, 0, 0)),
            _resident((None, D_MODEL, IN_COLS), lambda i: (e, 0, 0)),
            gain_spec, gain_spec,
        ],
        out_specs=[heads_spec] * 3 + [half_spec] * 2,
        out_shape=[heads] * 3 + [half, jax.ShapeDtypeStruct((rows, MIX_HALF), BF16)],
        compiler_params=_params("parallel"),
        name="inproj",
    )(x2, g, w_in, q_g, k_g)


def _attn_kernel(q_ref, k_ref, v_ref, o_ref, m_ref, l_ref, acc_ref, *, seq):
    jq = lax.broadcasted_iota(jnp.int32, (A_BAND, A_BAND), 0)
    jk = lax.broadcasted_iota(jnp.int32, (A_BAND, A_BAND), 1)
    own_mask = jk <= jq
    jk2 = lax.broadcasted_iota(jnp.int32, (A_BAND, 2 * A_BAND), 1)
    jq2 = lax.broadcasted_iota(jnp.int32, (A_BAND, 2 * A_BAND), 0)
    both_mask = (jk2 >= jq2) & (jk2 <= jq2 + A_BAND)
    neg = -jnp.inf
    lanes = (A_BAND, A_HEAD_DIM)
    ones = jnp.ones(lanes, BF16)

    def rows(start, d):
        if d == 1:
            return pl.ds(pl.multiple_of(start, A_BAND), A_BAND)
        return pl.ds(start, A_BAND, stride=d)

    def load_kv(start, d):
        r = rows(start, d)
        return (k_ref[r, :].astype(BF16),
                jnp.concatenate([v_ref[r, :].astype(BF16), ones], axis=1))

    def process(jobs, d, mode):
        rs = [rows(start, d) for start, _, _ in jobs]
        qs = [q_ref[r, :].astype(BF16) for r in rs]
        scores = []
        for q, (_, own, prev) in zip(qs, jobs):
            if prev is None:
                s = jnp.where(own_mask, _dot_nt(q, own[0]), neg)
            else:
                s = _dot_nt(q, jnp.concatenate([prev[0], own[0]], axis=0))
                s = jnp.where(both_mask, s, neg)
            scores.append(s)
        ms = [jnp.broadcast_to(jnp.max(s, axis=-1, keepdims=True), lanes) for s in scores]
        if mode != "init":
            m_olds = [m_ref[r, :] for r in rs]
            ms = [jnp.maximum(m_old, m) for m_old, m in zip(m_olds, ms)]
        pvs = []
        for s, m, (_, own, prev) in zip(scores, ms, jobs):
            if prev is None:
                v = own[1]
            else:
                v = jnp.concatenate([prev[1], own[1]], axis=0)
                m = jnp.concatenate([m, m], axis=1)
            pvs.append(_dot(jnp.exp2(s - m).astype(BF16), v))
        for i, (r, m, pv) in enumerate(zip(rs, ms, pvs)):
            acc, den = pv[:, :A_HEAD_DIM], pv[:, A_HEAD_DIM:]
            if mode != "init":
                a_old = jnp.exp2(m_olds[i] - m)
                den = a_old * l_ref[r, :] + den
                acc = a_old * acc_ref[r, :] + acc
            if mode == "final":
                o_ref[r, :] = (acc / den).astype(o_ref.dtype)
            else:
                m_ref[r, :] = m
                l_ref[r, :] = den
                acc_ref[r, :] = acc

    def jobs_of(r, c0, n, d, first):
        pitch = A_BAND * d
        kv = [None if first else load_kv(r + (c0 - 1) * pitch, d)]
        kv += [load_kv(r + (c0 + j) * pitch, d) for j in range(n)]
        return [(r + (c0 + j) * pitch, kv[j + 1], kv[j]) for j in range(n)]

    def group(r, c0, n, d, mode, first):
        process(jobs_of(r, c0, n, d, first), d, mode)

    def run_pass(d, mode):
        n_chunks = seq // (A_BAND * d)
        per_loop_step = A_CHUNKS_PER_STEP[d]
        n = min(n_chunks, per_loop_step)
        if n_chunks == n:
            per_step = per_loop_step // n

            def step(i, carry):
                jobs = []
                for j in range(per_step):
                    jobs += jobs_of(i * per_step + j, 0, n, d, True)
                process(jobs, d, mode)
                return carry

            lax.fori_loop(0, d // per_step, step, 0)
            return

        def residue(r, carry):
            group(r, 0, n, d, mode, True)

            def step(i, c):
                group(r, i * n, n, d, mode, False)
                return c

            return lax.fori_loop(1, n_chunks // n, step, carry)

        if d == 1:
            residue(0, 0)
        else:
            lax.fori_loop(0, d, residue, 0)

    modes = ("init",) + ("merge",) * (len(A_DILATIONS) - 2) + ("final",)
    for d, mode in zip(A_DILATIONS, modes):
        run_pass(d, mode)


def _attn(q, k, v):
    _, batch, seq, _ = q.shape
    assert seq % (A_BAND * max(A_DILATIONS)) == 0
    spec = pl.BlockSpec((None, None, seq, A_HEAD_DIM), lambda b, h: (h, b, 0, 0))
    return pl.pallas_call(
        functools.partial(_attn_kernel, seq=seq),
        grid=(batch, A_HEADS),
        in_specs=[spec, spec, spec],
        out_specs=spec,
        out_shape=jax.ShapeDtypeStruct(q.shape, BF16),
        scratch_shapes=[pltpu.VMEM((seq, A_HEAD_DIM), F32)] * 3,
        compiler_params=_params("parallel", "parallel"),
        name="dilated_attn",
    )(q, k, v)


def _gelu_tanh(x):
    c1 = -2.0 * 0.7978845608028654 * LOG2_E
    return x / (1.0 + jnp.exp2(x * (c1 + (c1 * 0.044715) * (x * x))))


def _rglru_out_kernel(x_ref, oa_ref, xr_ref, gate_ref, cw_ref, cb_ref, wg_ref, gab_ref, gxb_ref,
                      lam_ref, wout_ref, o_ref, xpad_ref, carry_ref):
    halo = SUBLANES
    group = (SUBLANES, RG_BLOCK_DIM)

    @pl.when(pl.program_id(1) == 0)
    def _():
        xpad_ref[0:halo, :] = jnp.zeros((halo, RG_WIDTH), F32)
        carry_ref[...] = jnp.zeros((SUBLANES, RG_WIDTH), F32)

    xpad_ref[halo:halo + RG_TILE, :] = xr_ref[...]
    w_out = wout_ref[...].astype(BF16)
    wg = [wg_ref[g].astype(BF16) for g in range(RG_BLOCKS)]
    lam = -lam_ref[...]
    softplus = jnp.maximum(lam, 0.0) + jnp.log1p(jnp.exp(-jnp.abs(lam)))
    grouped = (RG_SUB // SUBLANES, SUBLANES, RG_BLOCK_DIM)
    row = lax.broadcasted_iota(jnp.int32, grouped, 1)
    carry = [carry_ref[:, g * RG_BLOCK_DIM:(g + 1) * RG_BLOCK_DIM] for g in range(RG_BLOCKS)]

    for r0 in range(0, RG_TILE, RG_SUB):
        rs = slice(r0, r0 + RG_SUB)
        xc = cb_ref[...] + cw_ref[CONV_WIDTH - 1:CONV_WIDTH, :] * xr_ref[rs, :]
        for j in range(1, CONV_WIDTH):
            xc = xc + (cw_ref[CONV_WIDTH - 1 - j:CONV_WIDTH - j, :]
                       * xpad_ref[pl.ds(halo - j + r0, RG_SUB), :])
        mixed = [oa_ref[hd, rs, :] for hd in range(A_HEADS)]
        for g in range(RG_BLOCKS):
            sl = slice(g * RG_BLOCK_DIM, (g + 1) * RG_BLOCK_DIM)
            xg = xc[:, sl]
            rg = _dot(xg.astype(BF16), wg[g])
            r = jax.nn.sigmoid(rg[:, :RG_BLOCK_DIM] + gab_ref[:, sl])
            i = jax.nn.sigmoid(rg[:, RG_BLOCK_DIM:] + gxb_ref[:, sl])
            log_a = (-RG_C) * r * softplus[:, sl]
            a = jnp.exp(log_a)
            th = jnp.tanh(log_a)
            u = lax.rsqrt((th - 1.0) / (2.0 * th)) * (i * xg)
            a = a.reshape(grouped)
            u = u.reshape(grouped)
            k = 1
            while k < SUBLANES:
                valid = row >= k
                a_prev = pltpu.roll(a, k, 1)
                u_prev = pltpu.roll(u, k, 1)
                u = jnp.where(valid, a * u_prev + u, u)
                a = jnp.where(valid, a * a_prev, a)
                k *= 2
            a = a.reshape(RG_SUB, RG_BLOCK_DIM)
            u = u.reshape(RG_SUB, RG_BLOCK_DIM)
            hs = []
            c = carry[g]
            for gi in range(0, RG_SUB, SUBLANES):
                h = u[gi:gi + SUBLANES, :] + a[gi:gi + SUBLANES, :] * c
                hs.append(h)
                c = jnp.broadcast_to(h[SUBLANES - 1:SUBLANES, :], group)
            carry[g] = c
            mixed.append(jnp.concatenate(hs, axis=0).astype(BF16) * gate_ref[rs, sl])
        o_ref[rs, :] = x_ref[rs, :] + _dot(jnp.concatenate(mixed, axis=-1), w_out)

    for g in range(RG_BLOCKS):
        carry_ref[:, g * RG_BLOCK_DIM:(g + 1) * RG_BLOCK_DIM] = carry[g]
    xpad_ref[0:halo, :] = xr_ref[RG_TILE - halo:RG_TILE, :]


def _rglru_out(x, o_a, xr, gate, e, conv_w, conv_b, w_gates, ga_b, gx_b, lam, w_out):
    batch, seq, _ = x.shape
    full = pl.BlockSpec((None, RG_TILE, D_MODEL), lambda b, t: (b, t, 0))
    half = pl.BlockSpec((None, RG_TILE, RG_WIDTH), lambda b, t: (b, t, 0))
    vec = pl.BlockSpec((None, 1, RG_WIDTH), lambda b, t: (e, 0, 0))
    return pl.pallas_call(
        _rglru_out_kernel,
        grid=(batch, seq // RG_TILE),
        in_specs=[
            full,
            pl.BlockSpec((A_HEADS, None, RG_TILE, A_HEAD_DIM), lambda b, t: (0, b, t, 0)),
            half, half,
            pl.BlockSpec((None, CONV_WIDTH, RG_WIDTH), lambda b, t: (e, 0, 0)),
            vec,
            pl.BlockSpec((None, RG_BLOCKS, RG_BLOCK_DIM, 2 * RG_BLOCK_DIM), lambda b, t: (e, 0, 0, 0)),
            vec, vec, vec,
            _resident((None, D_MODEL, D_MODEL), lambda b, t: (e, 0, 0)),
        ],
        out_specs=full,
        out_shape=jax.ShapeDtypeStruct(x.shape, F32),
        scratch_shapes=[
            pltpu.VMEM((RG_TILE + SUBLANES, RG_WIDTH), F32),
            pltpu.VMEM((SUBLANES, RG_WIDTH), F32),
        ],
        compiler_params=_params("arbitrary", "arbitrary"),
        name="rglru_outproj",
    )(x, o_a, xr, gate, conv_w, conv_b, w_gates, ga_b, gx_b, lam, w_out)


def kernel(x, mem, mem_norm_g, mix_norm_g, xattn_norm_g, mlp_norm_g, ev_w_in, ev_q_norm_g, ev_k_norm_g, ev_conv_w, ev_conv_b, ev_gate_a_w, ev_gate_a_b, ev_gate_x_w, ev_gate_x_b, ev_lambda, ev_w_out, od_pool_w, od_scale, xa_w_q, xa_w_kv, xa_q_norm_g, xa_k_norm_g, xa_w_o, mlp_w1, mlp_w2):
    batch, seq, _ = x.shape
    rows = batch * seq

    def vec3(a):
        return a.reshape(a.shape[0], 1, a.shape[1])

    w_in, w_out = ev_w_in, ev_w_out
    w_gates = jnp.concatenate([ev_gate_a_w, ev_gate_x_w], axis=-1)
    pool_w = od_pool_w
    w_q, w_kv, w_o = xa_w_q, xa_w_kv, xa_w_o
    w1, w2 = mlp_w1, mlp_w2
    mix_g, xa_g, mlp_g = vec3(mix_norm_g), vec3(xattn_norm_g), vec3(mlp_norm_g)
    ev_qg, ev_kg = vec3(ev_q_norm_g), vec3(ev_k_norm_g)
    xa_qg = vec3(xa_q_norm_g)
    conv_b, ga_b, gx_b, lam = vec3(ev_conv_b), vec3(ev_gate_a_b), vec3(ev_gate_x_b), vec3(ev_lambda)
    od_sc = vec3(od_scale)

    kt, v_mem = _memkv(mem, mem_norm_g, w_kv, xa_k_norm_g)

    for l in range(DEPTH):
        if l % 2 == 0:
            e = l // 2
            q, k, v, xr, gate = _inproj(x.reshape(rows, D_MODEL), l, e, mix_g, w_in, ev_qg, ev_kg)
            half = (batch, seq, MIX_HALF)
            heads = (A_HEADS, batch, seq, A_HEAD_DIM)
            o_a = _attn(q.reshape(heads), k.reshape(heads), v.reshape(heads))
            x = _rglru_out(x, o_a, xr.reshape(half), gate.reshape(half), e, ev_conv_w, conv_b,
                           w_gates, ga_b, gx_b, lam, w_out)
            pool = None
        else:
            pool = (mix_g, pool_w, od_sc)
        x, w1b, w2b = _xattn(x, l, xa_g, w_q, xa_qg, kt, v_mem, w_o, w1, w2, pool)
        x = _mlp(x.reshape(rows, D_MODEL), l, mlp_g, w1b, w2b).reshape(batch, seq, D_MODEL)
    return x
```

```python
import functools

import jax
import jax.numpy as jnp
from jax import lax
from jax.experimental import pallas as pl
from jax.experimental.pallas import tpu as pltpu

D_MODEL = 1024
DEPTH = 4
N_MEM = 256
MIX_HALF = D_MODEL // 2
A_HEADS = 4
A_HEAD_DIM = MIX_HALF // A_HEADS
A_BAND = 128
A_DILATIONS = (16, 4, 1)
A_CHUNKS_PER_STEP = {16: 8, 4: 8, 1: 16}
RG_WIDTH = MIX_HALF
RG_BLOCKS = 4
RG_BLOCK_DIM = RG_WIDTH // RG_BLOCKS
RG_C = 8.0
CONV_WIDTH = 4
POOL_WINDOWS = (2, 4, 8, 16)
POOL_GROUP_DIM = D_MODEL // len(POOL_WINDOWS)
POOL_HALO = 16
XA_HEADS = 4
XA_HEAD_DIM = D_MODEL // XA_HEADS
D_FF = 4 * D_MODEL
IN_COLS = 3 * MIX_HALF + 2 * RG_WIDTH
EPS = 1e-6
LOG2_E = 1.4426950408889634

SUBLANES = 8
VMEM_LIMIT = 56 * 1024 * 1024

ROW_TILE = 1024
XA_SUB = 512
MLP_TILE = 1024
MLP_CHUNK = 1024
RG_TILE = 1024
RG_SUB = 256

BF16 = jnp.bfloat16
F32 = jnp.float32


def _rms(x, g):
    ms = jnp.mean(x * x, axis=-1, keepdims=True)
    return x * lax.rsqrt(ms + EPS) * g


def _dot(a, b):
    return jnp.dot(a, b, preferred_element_type=F32)


def _dot_nt(a, b):
    return lax.dot_general(a, b, (((1,), (1,)), ((), ())), preferred_element_type=F32)


def _resident(shape, index_map):
    return pl.BlockSpec(shape, index_map, pipeline_mode=pl.Buffered(1))


def _params(*sem):
    return pltpu.CompilerParams(dimension_semantics=sem, vmem_limit_bytes=VMEM_LIMIT)


def _memkv_kernel(mem_ref, mg_ref, wkv_ref, kg_ref, kt_ref, v_ref):
    batch = mem_ref.shape[0]
    mem = mem_ref[...].reshape(batch * N_MEM, D_MODEL)
    mem_n = _rms(mem, mg_ref[...]).astype(BF16)
    kv = _dot(mem_n, wkv_ref[...].astype(BF16))
    scale = XA_HEAD_DIM ** -0.5 * LOG2_E
    for h in range(XA_HEADS):
        sl = slice(h * XA_HEAD_DIM, (h + 1) * XA_HEAD_DIM)
        kh = _rms(kv[:, sl], kg_ref[...]) * scale
        for b in range(batch):
            kt_ref[b, sl, :] = kh[b * N_MEM:(b + 1) * N_MEM, :].T.astype(BF16)
    v_ref[...] = kv[:, D_MODEL:].astype(BF16).reshape(batch, N_MEM, D_MODEL)


def _memkv(mem, mem_norm_g, w_kv, k_g):
    batch = mem.shape[0]
    return pl.pallas_call(
        _memkv_kernel,
        grid=(DEPTH,),
        in_specs=[
            pl.BlockSpec((batch, N_MEM, D_MODEL), lambda l: (0, 0, 0)),
            pl.BlockSpec((1, D_MODEL), lambda l: (0, 0)),
            pl.BlockSpec((None, D_MODEL, 2 * D_MODEL), lambda l: (l, 0, 0)),
            pl.BlockSpec((None, 1, XA_HEAD_DIM), lambda l: (l, 0, 0)),
        ],
        out_specs=[
            pl.BlockSpec((None, batch, D_MODEL, N_MEM), lambda l: (l, 0, 0, 0)),
            pl.BlockSpec((None, batch, N_MEM, D_MODEL), lambda l: (l, 0, 0, 0)),
        ],
        out_shape=[
            jax.ShapeDtypeStruct((DEPTH, batch, D_MODEL, N_MEM), BF16),
            jax.ShapeDtypeStruct((DEPTH, batch, N_MEM, D_MODEL), BF16),
        ],
        compiler_params=_params("arbitrary"),
        name="memkv",
    )(mem, mem_norm_g.reshape(1, D_MODEL), w_kv, k_g.reshape(DEPTH, 1, XA_HEAD_DIM))


def _pool_rows(x, r0, pos0, mg_ref, pw_ref, sc_ref, hpad_ref):
    rows = x.shape[0]
    h = _rms(x, mg_ref[...])
    hpad_ref[POOL_HALO + r0:POOL_HALO + r0 + rows, :] = h
    pos = pos0 + lax.broadcasted_iota(jnp.int32, (rows, 1), 0)
    s = hpad_ref[r0:r0 + POOL_HALO + rows, :]
    outs = []
    width = 1
    for gi, w in enumerate(POOL_WINDOWS):
        while width < w:
            s = s + pltpu.roll(s, width, 0)
            width *= 2
        cnt = jnp.minimum(pos + 1, w).astype(F32)
        d = s[POOL_HALO:, :POOL_GROUP_DIM] / cnt - h[:, gi * POOL_GROUP_DIM:(gi + 1) * POOL_GROUP_DIM]
        outs.append(_dot(d.astype(BF16), pw_ref[gi].astype(BF16)))
        if gi + 1 < len(POOL_WINDOWS):
            s = s[:, POOL_GROUP_DIM:]
    return x + jnp.concatenate(outs, axis=-1) * sc_ref[...]


def _xattn_kernel(*refs, pool):
    if pool:
        (x_ref, mg_ref, pw_ref, sc_ref, g_ref, wq_ref, qg_ref, kt_ref, v_ref, wo_ref, w1p_ref, w2p_ref,
         o_ref, w1b_ref, w2b_ref, hpad_ref) = refs
        t = pl.program_id(1)

        @pl.when(t == 0)
        def _():
            hpad_ref[0:POOL_HALO, :] = jnp.zeros((POOL_HALO, D_MODEL), F32)
    else:
        (x_ref, g_ref, wq_ref, qg_ref, kt_ref, v_ref, wo_ref, w1p_ref, w2p_ref,
         o_ref, w1b_ref, w2b_ref) = refs
    w1b_ref[...] = w1p_ref[...].astype(BF16)
    w2b_ref[...] = w2p_ref[...].astype(BF16)
    w_q = wq_ref[...].astype(BF16)
    w_o = wo_ref[...].astype(BF16)
    heads = [slice(hd * XA_HEAD_DIM, (hd + 1) * XA_HEAD_DIM) for hd in range(XA_HEADS)]
    for r0 in range(0, ROW_TILE, XA_SUB):
        rs = slice(r0, r0 + XA_SUB)
        x = x_ref[rs, :]
        if pool:
            x = _pool_rows(x, r0, t * ROW_TILE + r0, mg_ref, pw_ref, sc_ref, hpad_ref)
        h = _rms(x, g_ref[...]).astype(BF16)
        qs = [_dot(h, w_q[:, sl]) for sl in heads]
        qs = [_rms(q, qg_ref[...]).astype(BF16) for q in qs]
        ss = [_dot(q, kt_ref[sl, :]) for q, sl in zip(qs, heads)]
        ps = [jnp.exp2(s - jnp.max(s, axis=-1, keepdims=True)) for s in ss]
        dens = [jnp.sum(p, axis=-1, keepdims=True) for p in ps]
        os_ = [_dot(p.astype(BF16), v_ref[:, sl]) for p, sl in zip(ps, heads)]
        os_ = [(o / den).astype(BF16) for o, den in zip(os_, dens)]
        out = x
        for o, sl in zip(os_, heads):
            out = out + _dot(o, w_o[sl, :])
        o_ref[rs, :] = out
    if pool:
        hpad_ref[0:POOL_HALO, :] = hpad_ref[ROW_TILE:ROW_TILE + POOL_HALO, :]


def _xattn(x, layer, g, w_q, q_g, kt, v, w_o, w1, w2, pool=None):
    batch, seq, _ = x.shape
    n_t = seq // ROW_TILE
    steps = batch * n_t
    tile = pl.BlockSpec((None, ROW_TILE, D_MODEL), lambda b, t: (b, t, 0))
    in_specs = [
        pl.BlockSpec((None, 1, D_MODEL), lambda b, t: (layer, 0, 0)),
        _resident((None, D_MODEL, D_MODEL), lambda b, t: (layer, 0, 0)),
        pl.BlockSpec((None, 1, XA_HEAD_DIM), lambda b, t: (layer, 0, 0)),
        pl.BlockSpec((None, None, D_MODEL, N_MEM), lambda b, t: (layer, b, 0, 0)),
        pl.BlockSpec((None, None, N_MEM, D_MODEL), lambda b, t: (layer, b, 0, 0)),
        _resident((None, D_MODEL, D_MODEL), lambda b, t: (layer, 0, 0)),
        pl.BlockSpec((None, D_MODEL // steps, D_FF), lambda b, t: (layer, b * n_t + t, 0)),
        pl.BlockSpec((None, D_FF // steps, D_MODEL), lambda b, t: (layer, b * n_t + t, 0)),
    ]
    args = (g, w_q, q_g, kt, v, w_o, w1, w2)
    scratch = []
    if pool is not None:
        o = layer // 2
        in_specs = [
            pl.BlockSpec((None, 1, D_MODEL), lambda b, t: (layer, 0, 0)),
            pl.BlockSpec((None, len(POOL_WINDOWS), POOL_GROUP_DIM, POOL_GROUP_DIM),
                         lambda b, t: (o, 0, 0, 0)),
            pl.BlockSpec((None, 1, D_MODEL), lambda b, t: (o, 0, 0)),
        ] + in_specs
        args = tuple(pool) + args
        scratch = [pltpu.VMEM((POOL_HALO + ROW_TILE, D_MODEL), F32)]
    return pl.pallas_call(
        functools.partial(_xattn_kernel, pool=pool is not None),
        grid=(batch, seq // ROW_TILE),
        in_specs=[tile] + in_specs,
        out_specs=[
            tile,
            pl.BlockSpec((D_MODEL // steps, D_FF), lambda b, t: (b * n_t + t, 0)),
            pl.BlockSpec((D_FF // steps, D_MODEL), lambda b, t: (b * n_t + t, 0)),
        ],
        out_shape=[
            jax.ShapeDtypeStruct(x.shape, F32),
            jax.ShapeDtypeStruct((D_MODEL, D_FF), BF16),
            jax.ShapeDtypeStruct((D_FF, D_MODEL), BF16),
        ],
        scratch_shapes=scratch,
        compiler_params=_params("parallel", "arbitrary" if pool is not None else "parallel"),
        name="xattn" if pool is None else "pool_xattn",
    )(x, *args)


def _mlp_kernel(x_ref, g_ref, w1_ref, w2_ref, o_ref):
    x = x_ref[...]
    h = _rms(x, g_ref[...]).astype(BF16)
    acc = x
    for c in range(D_FF // MLP_CHUNK):
        sl = slice(c * MLP_CHUNK, (c + 1) * MLP_CHUNK)
        a = jnp.maximum(_dot(h, w1_ref[:, sl]), 0.0)
        acc = acc + _dot((a * a).astype(BF16), w2_ref[sl, :])
    o_ref[...] = acc


def _mlp(x2, layer, g, w1b, w2b):
    rows = x2.shape[0]
    return pl.pallas_call(
        _mlp_kernel,
        grid=(rows // MLP_TILE,),
        in_specs=[
            pl.BlockSpec((MLP_TILE, D_MODEL), lambda i: (i, 0)),
            pl.BlockSpec((None, 1, D_MODEL), lambda i: (layer, 0, 0)),
            _resident((D_MODEL, D_FF), lambda i: (0, 0)),
            _resident((D_FF, D_MODEL), lambda i: (0, 0)),
        ],
        out_specs=pl.BlockSpec((MLP_TILE, D_MODEL), lambda i: (i, 0)),
        out_shape=jax.ShapeDtypeStruct(x2.shape, F32),
        compiler_params=_params("parallel"),
        name="mlp",
    )(x2, g, w1b, w2b)


def _inproj_kernel(x_ref, g_ref, w_ref, qg_ref, kg_ref, q_ref, k_ref, v_ref, xr_ref, gate_ref):
    h = _rms(x_ref[...], g_ref[...]).astype(BF16)
    scale = A_HEAD_DIM ** -0.5 * LOG2_E
    for j, (out_ref, gain_ref, mult) in (
            (4, (gate_ref, None, None)), (0, (q_ref, qg_ref, scale)), (1, (k_ref, kg_ref, 1.0)),
            (2, (v_ref, None, None)), (3, (xr_ref, None, None))):
        p = _dot(h, w_ref[:, j * MIX_HALF:(j + 1) * MIX_HALF].astype(BF16))
        if j >= 3:
            out_ref[...] = p if j == 3 else _gelu_tanh(p).astype(out_ref.dtype)
            continue
        for hd in range(A_HEADS):
            ph = p[:, hd * A_HEAD_DIM:(hd + 1) * A_HEAD_DIM]
            out_ref[hd] = ph if gain_ref is None else _rms(ph, gain_ref[...]) * mult


def _inproj(x2, layer, e, g, w_in, q_g, k_g):
    rows = x2.shape[0]
    half = jax.ShapeDtypeStruct((rows, MIX_HALF), F32)
    half_spec = pl.BlockSpec((ROW_TILE, MIX_HALF), lambda i: (i, 0))
    heads = jax.ShapeDtypeStruct((A_HEADS, rows, A_HEAD_DIM), F32)
    heads_spec = pl.BlockSpec((A_HEADS, ROW_TILE, A_HEAD_DIM), lambda i: (0, i, 0))
    gain_spec = pl.BlockSpec((None, 1, A_HEAD_DIM), lambda i: (e, 0, 0))
    return pl.pallas_call(
        _inproj_kernel,
        grid=(rows // ROW_TILE,),
        in_specs=[
            pl.BlockSpec((ROW_TILE, D_MODEL), lambda i: (i, 0)),
            pl.BlockSpec((None, 1, D_MODEL), lambda i: (layer, 0, 0)),
            _resident((None, D_MODEL, IN_COLS), lambda i: (e, 0, 0)),
            gain_spec, gain_spec,
        ],
        out_specs=[heads_spec] * 3 + [half_spec] * 2,
        out_shape=[heads] * 3 + [half, jax.ShapeDtypeStruct((rows, MIX_HALF), BF16)],
        compiler_params=_params("parallel"),
        name="inproj",
    )(x2, g, w_in, q_g, k_g)


def _attn_kernel(q_ref, k_ref, v_ref, o_ref, m_ref, l_ref, acc_ref, *, seq):
    jq = lax.broadcasted_iota(jnp.int32, (A_BAND, A_BAND), 0)
    jk = lax.broadcasted_iota(jnp.int32, (A_BAND, A_BAND), 1)
    own_mask = jk <= jq
    jk2 = lax.broadcasted_iota(jnp.int32, (A_BAND, 2 * A_BAND), 1)
    jq2 = lax.broadcasted_iota(jnp.int32, (A_BAND, 2 * A_BAND), 0)
    both_mask = (jk2 >= jq2) & (jk2 <= jq2 + A_BAND)
    neg = -jnp.inf
    lanes = (A_BAND, A_HEAD_DIM)
    ones = jnp.ones(lanes, BF16)

    def rows(start, d):
        if d == 1:
            return pl.ds(pl.multiple_of(start, A_BAND), A_BAND)
        return pl.ds(start, A_BAND, stride=d)

    def load_kv(start, d):
        r = rows(start, d)
        return (k_ref[r, :].astype(BF16),
                jnp.concatenate([v_ref[r, :].astype(BF16), ones], axis=1))

    def process(jobs, d, mode):
        rs = [rows(start, d) for start, _, _ in jobs]
        qs = [q_ref[r, :].astype(BF16) for r in rs]
        scores = []
        for q, (_, own, prev) in zip(qs, jobs):
            if prev is None:
                s = jnp.where(own_mask, _dot_nt(q, own[0]), neg)
            else:
                s = _dot_nt(q, jnp.concatenate([prev[0], own[0]], axis=0))
                s = jnp.where(both_mask, s, neg)
            scores.append(s)
        ms = [jnp.broadcast_to(jnp.max(s, axis=-1, keepdims=True), lanes) for s in scores]
        reads, writes = mode
        if reads is not None:
            m_olds = [m_ref[r, :] for r in rs]
            ms = [jnp.maximum(m_old, m) for m_old, m in zip(m_olds, ms)]
        pvs = []
        for s, m, (_, own, prev) in zip(scores, ms, jobs):
            if prev is None:
                v = own[1]
            else:
                v = jnp.concatenate([prev[1], own[1]], axis=0)
                m = jnp.concatenate([m, m], axis=1)
            pvs.append(_dot(jnp.exp2(s - m).astype(BF16), v))
        for i, (r, m, pv) in enumerate(zip(rs, ms, pvs)):
            acc, den = pv[:, :A_HEAD_DIM], pv[:, A_HEAD_DIM:]
            if reads is not None:
                a_old = jnp.exp2(m_olds[i] - m)
                den = (a_old if reads == "lse" else a_old * l_ref[r, :]) + den
                acc = a_old * acc_ref[r, :] + acc
            if writes is None:
                o_ref[r, :] = (acc / den).astype(o_ref.dtype)
            elif writes == "lse":
                m_ref[r, :] = m + jnp.log2(den)
                acc_ref[r, :] = acc / den
            else:
                m_ref[r, :] = m
                l_ref[r, :] = den
                acc_ref[r, :] = acc

    def jobs_of(r, c0, n, d, first):
        pitch = A_BAND * d
        kv = [None if first else load_kv(r + (c0 - 1) * pitch, d)]
        kv += [load_kv(r + (c0 + j) * pitch, d) for j in range(n)]
        return [(r + (c0 + j) * pitch, kv[j + 1], kv[j]) for j in range(n)]

    def group(r, c0, n, d, mode, first):
        process(jobs_of(r, c0, n, d, first), d, mode)

    def run_pass(d, mode):
        n_chunks = seq // (A_BAND * d)
        per_loop_step = A_CHUNKS_PER_STEP[d]
        n = min(n_chunks, per_loop_step)
        if n_chunks == n:
            per_step = per_loop_step // n

            def step(i, carry):
                jobs = []
                for j in range(per_step):
                    jobs += jobs_of(i * per_step + j, 0, n, d, True)
                process(jobs, d, mode)
                return carry

            lax.fori_loop(0, d // per_step, step, 0)
            return

        def residue(r, carry):
            group(r, 0, n, d, mode, True)

            def step(i, c):
                group(r, i * n, n, d, mode, False)
                return c

            return lax.fori_loop(1, n_chunks // n, step, carry)

        if d == 1:
            residue(0, 0)
        else:
            lax.fori_loop(0, d, residue, 0)

    forms = ["lse"] + ["raw"] * (len(A_DILATIONS) - 2)
    for d, mode in zip(A_DILATIONS, zip([None] + forms, forms + [None])):
        run_pass(d, mode)


def _attn(q, k, v):
    _, batch, seq, _ = q.shape
    assert seq % (A_BAND * max(A_DILATIONS)) == 0
    spec = pl.BlockSpec((None, None, seq, A_HEAD_DIM), lambda b, h: (h, b, 0, 0))
    return pl.pallas_call(
        functools.partial(_attn_kernel, seq=seq),
        grid=(batch, A_HEADS),
        in_specs=[spec, spec, spec],
        out_specs=spec,
        out_shape=jax.ShapeDtypeStruct(q.shape, BF16),
        scratch_shapes=[pltpu.VMEM((seq, A_HEAD_DIM), F32)] * 3,
        compiler_params=_params("parallel", "parallel"),
        name="dilated_attn",
    )(q, k, v)


def _gelu_tanh(x):
    c1 = -2.0 * 0.7978845608028654 * LOG2_E
    return x / (1.0 + jnp.exp2(x * (c1 + (c1 * 0.044715) * (x * x))))


def _rglru_out_kernel(x_ref, oa_ref, xr_ref, gate_ref, cw_ref, cb_ref, wg_ref, gab_ref, gxb_ref,
                      lam_ref, wout_ref, o_ref, xpad_ref, carry_ref):
    halo = SUBLANES
    group = (SUBLANES, RG_BLOCK_DIM)

    @pl.when(pl.program_id(1) == 0)
    def _():
        xpad_ref[0:halo, :] = jnp.zeros((halo, RG_WIDTH), F32)
        carry_ref[...] = jnp.zeros((SUBLANES, RG_WIDTH), F32)

    xpad_ref[halo:halo + RG_TILE, :] = xr_ref[...]
    w_out = wout_ref[...].astype(BF16)
    wg = [wg_ref[g].astype(BF16) for g in range(RG_BLOCKS)]
    lam = -lam_ref[...]
    softplus = jnp.maximum(lam, 0.0) + jnp.log1p(jnp.exp(-jnp.abs(lam)))
    grouped = (RG_SUB // SUBLANES, SUBLANES, RG_BLOCK_DIM)
    row = lax.broadcasted_iota(jnp.int32, grouped, 1)
    carry = [carry_ref[:, g * RG_BLOCK_DIM:(g + 1) * RG_BLOCK_DIM] for g in range(RG_BLOCKS)]

    for r0 in range(0, RG_TILE, RG_SUB):
        rs = slice(r0, r0 + RG_SUB)
        xc = cb_ref[...] + cw_ref[CONV_WIDTH - 1:CONV_WIDTH, :] * xr_ref[rs, :]
        for j in range(1, CONV_WIDTH):
            xc = xc + (cw_ref[CONV_WIDTH - 1 - j:CONV_WIDTH - j, :]
                       * xpad_ref[pl.ds(halo - j + r0, RG_SUB), :])
        mixed = [oa_ref[hd, rs, :] for hd in range(A_HEADS)]
        for g in range(RG_BLOCKS):
            sl = slice(g * RG_BLOCK_DIM, (g + 1) * RG_BLOCK_DIM)
            xg = xc[:, sl]
            rg = _dot(xg.astype(BF16), wg[g])
            r = jax.nn.sigmoid(rg[:, :RG_BLOCK_DIM] + gab_ref[:, sl])
            i = jax.nn.sigmoid(rg[:, RG_BLOCK_DIM:] + gxb_ref[:, sl])
            log_a = (-RG_C) * r * softplus[:, sl]
            a = jnp.exp(log_a)
            th = jnp.tanh(log_a)
            u = lax.rsqrt((th - 1.0) / (2.0 * th)) * (i * xg)
            a = a.reshape(grouped)
            u = u.reshape(grouped)
            k = 1
            while k < SUBLANES:
                valid = row >= k
                a_prev = pltpu.roll(a, k, 1)
                u_prev = pltpu.roll(u, k, 1)
                u = jnp.where(valid, a * u_prev + u, u)
                a = jnp.where(valid, a * a_prev, a)
                k *= 2
            a = a.reshape(RG_SUB, RG_BLOCK_DIM)
            u = u.reshape(RG_SUB, RG_BLOCK_DIM)
            hs = []
            c = carry[g]
            for gi in range(0, RG_SUB, SUBLANES):
                h = u[gi:gi + SUBLANES, :] + a[gi:gi + SUBLANES, :] * c
                hs.append(h)
                c = jnp.broadcast_to(h[SUBLANES - 1:SUBLANES, :], group)
            carry[g] = c
            mixed.append(jnp.concatenate(hs, axis=0).astype(BF16) * gate_ref[rs, sl])
        o_ref[rs, :] = x_ref[rs, :] + _dot(jnp.concatenate(mixed, axis=-1), w_out)

    for g in range(RG_BLOCKS):
        carry_ref[:, g * RG_BLOCK_DIM:(g + 1) * RG_BLOCK_DIM] = carry[g]
    xpad_ref[0:halo, :] = xr_ref[RG_TILE - halo:RG_TILE, :]


def _rglru_out(x, o_a, xr, gate, e, conv_w, conv_b, w_gates, ga_b, gx_b, lam, w_out):
    batch, seq, _ = x.shape
    full = pl.BlockSpec((None, RG_TILE, D_MODEL), lambda b, t: (b, t, 0))
    half = pl.BlockSpec((None, RG_TILE, RG_WIDTH), lambda b, t: (b, t, 0))
    vec = pl.BlockSpec((None, 1, RG_WIDTH), lambda b, t: (e, 0, 0))
    return pl.pallas_call(
        _rglru_out_kernel,
        grid=(batch, seq // RG_TILE),
        in_specs=[
            full,
            pl.BlockSpec((A_HEADS, None, RG_TILE, A_HEAD_DIM), lambda b, t: (0, b, t, 0)),
            half, half,
            pl.BlockSpec((None, CONV_WIDTH, RG_WIDTH), lambda b, t: (e, 0, 0)),
            vec,
            pl.BlockSpec((None, RG_BLOCKS, RG_BLOCK_DIM, 2 * RG_BLOCK_DIM), lambda b, t: (e, 0, 0, 0)),
            vec, vec, vec,
            _resident((None, D_MODEL, D_MODEL), lambda b, t: (e, 0, 0)),
        ],
        out_specs=full,
        out_shape=jax.ShapeDtypeStruct(x.shape, F32),
        scratch_shapes=[
            pltpu.VMEM((RG_TILE + SUBLANES, RG_WIDTH), F32),
            pltpu.VMEM((SUBLANES, RG_WIDTH), F32),
        ],
        compiler_params=_params("arbitrary", "arbitrary"),
        name="rglru_outproj",
    )(x, o_a, xr, gate, conv_w, conv_b, w_gates, ga_b, gx_b, lam, w_out)


def kernel(x, mem, mem_norm_g, mix_norm_g, xattn_norm_g, mlp_norm_g, ev_w_in, ev_q_norm_g, ev_k_norm_g, ev_conv_w, ev_conv_b, ev_gate_a_w, ev_gate_a_b, ev_gate_x_w, ev_gate_x_b, ev_lambda, ev_w_out, od_pool_w, od_scale, xa_w_q, xa_w_kv, xa_q_norm_g, xa_k_norm_g, xa_w_o, mlp_w1, mlp_w2):
    batch, seq, _ = x.shape
    rows = batch * seq

    def vec3(a):
        return a.reshape(a.shape[0], 1, a.shape[1])

    w_in, w_out = ev_w_in, ev_w_out
    w_gates = jnp.concatenate([ev_gate_a_w, ev_gate_x_w], axis=-1)
    pool_w = od_pool_w
    w_q, w_kv, w_o = xa_w_q, xa_w_kv, xa_w_o
    w1, w2 = mlp_w1, mlp_w2
    mix_g, xa_g, mlp_g = vec3(mix_norm_g), vec3(xattn_norm_g), vec3(mlp_norm_g)
    ev_qg, ev_kg = vec3(ev_q_norm_g), vec3(ev_k_norm_g)
    xa_qg = vec3(xa_q_norm_g)
    conv_b, ga_b, gx_b, lam = vec3(ev_conv_b), vec3(ev_gate_a_b), vec3(ev_gate_x_b), vec3(ev_lambda)
    od_sc = vec3(od_scale)

    kt, v_mem = _memkv(mem, mem_norm_g, w_kv, xa_k_norm_g)

    for l in range(DEPTH):
        if l % 2 == 0:
            e = l // 2
            q, k, v, xr, gate = _inproj(x.reshape(rows, D_MODEL), l, e, mix_g, w_in, ev_qg, ev_kg)
            half = (batch, seq, MIX_HALF)
            heads = (A_HEADS, batch, seq, A_HEAD_DIM)
            o_a = _attn(q.reshape(heads), k.reshape(heads), v.reshape(heads))
            x = _rglru_out(x, o_a, xr.reshape(half), gate.reshape(half), e, ev_conv_w, conv_b,
                           w_gates, ga_b, gx_b, lam, w_out)
            pool = None
        else:
            pool = (mix_g, pool_w, od_sc)
        x, w1b, w2b = _xattn(x, l, xa_g, w_q, xa_qg, kt, v_mem, w_o, w1, w2, pool)
        x = _mlp(x.reshape(rows, D_MODEL), l, mlp_g, w1b, w2b).reshape(batch, seq, D_MODEL)
    return x
```

```python
import functools

import jax
import jax.numpy as jnp
from jax import lax
from jax.experimental import pallas as pl
from jax.experimental.pallas import tpu as pltpu

D_MODEL = 1024
DEPTH = 4
N_MEM = 256
MIX_HALF = D_MODEL // 2
A_HEADS = 4
A_HEAD_DIM = MIX_HALF // A_HEADS
A_BAND = 128
A_DILATIONS = (16, 4, 1)
A_CHUNKS_PER_STEP = {16: 8, 4: 8, 1: 16}
RG_WIDTH = MIX_HALF
RG_BLOCKS = 4
RG_BLOCK_DIM = RG_WIDTH // RG_BLOCKS
RG_C = 8.0
CONV_WIDTH = 4
POOL_WINDOWS = (2, 4, 8, 16)
POOL_GROUP_DIM = D_MODEL // len(POOL_WINDOWS)
POOL_BLOCK = 128
XA_HEADS = 4
XA_HEAD_DIM = D_MODEL // XA_HEADS
D_FF = 4 * D_MODEL
IN_COLS = 3 * MIX_HALF + 2 * RG_WIDTH
EPS = 1e-6
LOG2_E = 1.4426950408889634

SUBLANES = 8
VMEM_LIMIT = 56 * 1024 * 1024

ROW_TILE = 1024
XA_SUB = 512
MLP_TILE = 1024
MLP_CHUNK = 512
RG_TILE = 1024
RG_SUB = 256

BF16 = jnp.bfloat16
F32 = jnp.float32


def _rms(x, g):
    ms = jnp.mean(x * x, axis=-1, keepdims=True)
    return x * lax.rsqrt(ms + EPS) * g


def _dot(a, b):
    return jnp.dot(a, b, preferred_element_type=F32)


def _dot_nt(a, b):
    return lax.dot_general(a, b, (((1,), (1,)), ((), ())), preferred_element_type=F32)


def _resident(shape, index_map):
    return pl.BlockSpec(shape, index_map, pipeline_mode=pl.Buffered(1))


def _params(*sem):
    return pltpu.CompilerParams(dimension_semantics=sem, vmem_limit_bytes=VMEM_LIMIT)


def _memkv_kernel(mem_ref, mg_ref, wkv_ref, kg_ref, kt_ref, v_ref):
    batch = mem_ref.shape[0]
    mem = mem_ref[...].reshape(batch * N_MEM, D_MODEL)
    mem_n = _rms(mem, mg_ref[...]).astype(BF16)
    kv = _dot(mem_n, wkv_ref[...].astype(BF16))
    scale = XA_HEAD_DIM ** -0.5 * LOG2_E
    for h in range(XA_HEADS):
        sl = slice(h * XA_HEAD_DIM, (h + 1) * XA_HEAD_DIM)
        kh = _rms(kv[:, sl], kg_ref[...]) * scale
        for b in range(batch):
            kt_ref[b, sl, :] = kh[b * N_MEM:(b + 1) * N_MEM, :].T.astype(BF16)
    v_ref[...] = kv[:, D_MODEL:].astype(BF16).reshape(batch, N_MEM, D_MODEL)


def _memkv(mem, mem_norm_g, w_kv, k_g):
    batch = mem.shape[0]
    return pl.pallas_call(
        _memkv_kernel,
        grid=(DEPTH,),
        in_specs=[
            pl.BlockSpec((batch, N_MEM, D_MODEL), lambda l: (0, 0, 0)),
            pl.BlockSpec((1, D_MODEL), lambda l: (0, 0)),
            pl.BlockSpec((None, D_MODEL, 2 * D_MODEL), lambda l: (l, 0, 0)),
            pl.BlockSpec((None, 1, XA_HEAD_DIM), lambda l: (l, 0, 0)),
        ],
        out_specs=[
            pl.BlockSpec((None, batch, D_MODEL, N_MEM), lambda l: (l, 0, 0, 0)),
            pl.BlockSpec((None, batch, N_MEM, D_MODEL), lambda l: (l, 0, 0, 0)),
        ],
        out_shape=[
            jax.ShapeDtypeStruct((DEPTH, batch, D_MODEL, N_MEM), BF16),
            jax.ShapeDtypeStruct((DEPTH, batch, N_MEM, D_MODEL), BF16),
        ],
        compiler_params=_params("arbitrary"),
        name="memkv",
    )(mem, mem_norm_g.reshape(1, D_MODEL), w_kv, k_g.reshape(DEPTH, 1, XA_HEAD_DIM))


def _pool_bands():
    r = lax.broadcasted_iota(jnp.int32, (POOL_BLOCK, 2 * POOL_BLOCK), 0) + POOL_BLOCK
    c = lax.broadcasted_iota(jnp.int32, (POOL_BLOCK, 2 * POOL_BLOCK), 1)
    return [((c <= r) & (c > r - w)).astype(BF16) for w in POOL_WINDOWS]


def _pool_rows(x, pos0, bands, mg_ref, pw_ref, sc_ref, tail_ref):
    rows = x.shape[0]
    h = _rms(x, mg_ref[...])
    hi = h.astype(BF16)
    lo = (h - hi.astype(F32)).astype(BF16)
    ext = [jnp.concatenate([tail_ref[i], part], axis=0) for i, part in enumerate((hi, lo))]
    tail_ref[0] = hi[rows - POOL_BLOCK:, :]
    tail_ref[1] = lo[rows - POOL_BLOCK:, :]
    pos = pos0 + lax.broadcasted_iota(jnp.int32, (rows, 1), 0)
    outs = []
    for gi, w in enumerate(POOL_WINDOWS):
        sl = slice(gi * POOL_GROUP_DIM, (gi + 1) * POOL_GROUP_DIM)
        sums = []
        for r0 in range(0, rows, POOL_BLOCK):
            win = slice(r0, r0 + 2 * POOL_BLOCK)
            sums.append(_dot(bands[gi], ext[0][win, sl]) + _dot(bands[gi], ext[1][win, sl]))
        cnt = jnp.minimum(pos + 1, w).astype(F32)
        d = jnp.concatenate(sums, axis=0) / cnt - h[:, sl]
        outs.append(_dot(d.astype(BF16), pw_ref[gi].astype(BF16)))
    return x + jnp.concatenate(outs, axis=-1) * sc_ref[...]


def _xattn_kernel(*refs, pool):
    if pool:
        (x_ref, mg_ref, pw_ref, sc_ref, g_ref, wq_ref, qg_ref, kt_ref, v_ref, wo_ref, w1p_ref, w2p_ref,
         o_ref, w1b_ref, w2b_ref, tail_ref) = refs
        t = pl.program_id(1)
        bands = _pool_bands()

        @pl.when(t == 0)
        def _():
            tail_ref[...] = jnp.zeros(tail_ref.shape, BF16)
    else:
        (x_ref, g_ref, wq_ref, qg_ref, kt_ref, v_ref, wo_ref, w1p_ref, w2p_ref,
         o_ref, w1b_ref, w2b_ref) = refs
    w1b_ref[...] = w1p_ref[...].astype(BF16)
    w2b_ref[...] = w2p_ref[...].astype(BF16)
    w_q = wq_ref[...].astype(BF16)
    w_o = wo_ref[...].astype(BF16)
    heads = [slice(hd * XA_HEAD_DIM, (hd + 1) * XA_HEAD_DIM) for hd in range(XA_HEADS)]
    for r0 in range(0, ROW_TILE, XA_SUB):
        rs = slice(r0, r0 + XA_SUB)
        x = x_ref[rs, :]
        if pool:
            x = _pool_rows(x, t * ROW_TILE + r0, bands, mg_ref, pw_ref, sc_ref, tail_ref)
        h = _rms(x, g_ref[...]).astype(BF16)
        qs = [_dot(h, w_q[:, sl]) for sl in heads]
        qs = [_rms(q, qg_ref[...]).astype(BF16) for q in qs]
        ss = [_dot(q, kt_ref[sl, :]) for q, sl in zip(qs, heads)]
        ps = [jnp.exp2(s - jnp.max(s, axis=-1, keepdims=True)) for s in ss]
        dens = [jnp.sum(p, axis=-1, keepdims=True) for p in ps]
        os_ = [_dot(p.astype(BF16), v_ref[:, sl]) for p, sl in zip(ps, heads)]
        os_ = [(o / den).astype(BF16) for o, den in zip(os_, dens)]
        out = x
        for o, sl in zip(os_, heads):
            out = out + _dot(o, w_o[sl, :])
        o_ref[rs, :] = out


def _xattn(x, layer, g, w_q, q_g, kt, v, w_o, w1, w2, pool=None):
    batch, seq, _ = x.shape
    n_t = seq // ROW_TILE
    steps = batch * n_t
    tile = pl.BlockSpec((None, ROW_TILE, D_MODEL), lambda b, t: (b, t, 0))
    in_specs = [
        pl.BlockSpec((None, 1, D_MODEL), lambda b, t: (layer, 0, 0)),
        _resident((None, D_MODEL, D_MODEL), lambda b, t: (layer, 0, 0)),
        pl.BlockSpec((None, 1, XA_HEAD_DIM), lambda b, t: (layer, 0, 0)),
        pl.BlockSpec((None, None, D_MODEL, N_MEM), lambda b, t: (layer, b, 0, 0)),
        pl.BlockSpec((None, None, N_MEM, D_MODEL), lambda b, t: (layer, b, 0, 0)),
        _resident((None, D_MODEL, D_MODEL), lambda b, t: (layer, 0, 0)),
        pl.BlockSpec((None, D_MODEL // steps, D_FF), lambda b, t: (layer, b * n_t + t, 0)),
        pl.BlockSpec((None, D_FF // steps, D_MODEL), lambda b, t: (layer, b * n_t + t, 0)),
    ]
    args = (g, w_q, q_g, kt, v, w_o, w1, w2)
    scratch = []
    if pool is not None:
        o = layer // 2
        in_specs = [
            pl.BlockSpec((None, 1, D_MODEL), lambda b, t: (layer, 0, 0)),
            pl.BlockSpec((None, len(POOL_WINDOWS), POOL_GROUP_DIM, POOL_GROUP_DIM),
                         lambda b, t: (o, 0, 0, 0)),
            pl.BlockSpec((None, 1, D_MODEL), lambda b, t: (o, 0, 0)),
        ] + in_specs
        args = tuple(pool) + args
        scratch = [pltpu.VMEM((2, POOL_BLOCK, D_MODEL), BF16)]
    return pl.pallas_call(
        functools.partial(_xattn_kernel, pool=pool is not None),
        grid=(batch, seq // ROW_TILE),
        in_specs=[tile] + in_specs,
        out_specs=[
            tile,
            pl.BlockSpec((D_MODEL // steps, D_FF), lambda b, t: (b * n_t + t, 0)),
            pl.BlockSpec((D_FF // steps, D_MODEL), lambda b, t: (b * n_t + t, 0)),
        ],
        out_shape=[
            jax.ShapeDtypeStruct(x.shape, F32),
            jax.ShapeDtypeStruct((D_MODEL, D_FF), BF16),
            jax.ShapeDtypeStruct((D_FF, D_MODEL), BF16),
        ],
        scratch_shapes=scratch,
        compiler_params=_params("parallel", "arbitrary" if pool is not None else "parallel"),
        name="xattn" if pool is None else "pool_xattn",
    )(x, *args)


def _mlp_kernel(x_ref, g_ref, w1_ref, w2_ref, o_ref):
    x = x_ref[...]
    h = _rms(x, g_ref[...]).astype(BF16)
    acc = x
    for c in range(D_FF // MLP_CHUNK):
        sl = slice(c * MLP_CHUNK, (c + 1) * MLP_CHUNK)
        a = jnp.maximum(_dot(h, w1_ref[:, sl]), 0.0)
        acc = acc + _dot((a * a).astype(BF16), w2_ref[sl, :])
    o_ref[...] = acc


def _mlp(x2, layer, g, w1b, w2b):
    rows = x2.shape[0]
    return pl.pallas_call(
        _mlp_kernel,
        grid=(rows // MLP_TILE,),
        in_specs=[
            pl.BlockSpec((MLP_TILE, D_MODEL), lambda i: (i, 0)),
            pl.BlockSpec((None, 1, D_MODEL), lambda i: (layer, 0, 0)),
            _resident((D_MODEL, D_FF), lambda i: (0, 0)),
            _resident((D_FF, D_MODEL), lambda i: (0, 0)),
        ],
        out_specs=pl.BlockSpec((MLP_TILE, D_MODEL), lambda i: (i, 0)),
        out_shape=jax.ShapeDtypeStruct(x2.shape, F32),
        compiler_params=_params("parallel"),
        name="mlp",
    )(x2, g, w1b, w2b)


def _inproj_kernel(x_ref, g_ref, w_ref, qg_ref, kg_ref, q_ref, k_ref, v_ref, xr_ref, gate_ref):
    h = _rms(x_ref[...], g_ref[...]).astype(BF16)
    scale = A_HEAD_DIM ** -0.5 * LOG2_E
    for j, (out_ref, gain_ref, mult) in (
            (4, (gate_ref, None, None)), (0, (q_ref, qg_ref, scale)), (1, (k_ref, kg_ref, 1.0)),
            (2, (v_ref, None, None)), (3, (xr_ref, None, None))):
        p = _dot(h, w_ref[:, j * MIX_HALF:(j + 1) * MIX_HALF].astype(BF16))
        if j >= 3:
            out_ref[...] = p if j == 3 else _gelu_tanh(p).astype(out_ref.dtype)
            continue
        for hd in range(A_HEADS):
            ph = p[:, hd * A_HEAD_DIM:(hd + 1) * A_HEAD_DIM]
            out_ref[hd] = ph if gain_ref is None else _rms(ph, gain_ref[...]) * mult


def _inproj(x2, layer, e, g, w_in, q_g, k_g):
    rows = x2.shape[0]
    half = jax.ShapeDtypeStruct((rows, MIX_HALF), F32)
    half_spec = pl.BlockSpec((ROW_TILE, MIX_HALF), lambda i: (i, 0))
    heads = jax.ShapeDtypeStruct((A_HEADS, rows, A_HEAD_DIM), F32)
    heads_spec = pl.BlockSpec((A_HEADS, ROW_TILE, A_HEAD_DIM), lambda i: (0, i, 0))
    gain_spec = pl.BlockSpec((None, 1, A_HEAD_DIM), lambda i: (e, 0, 0))
    return pl.pallas_call(
        _inproj_kernel,
        grid=(rows // ROW_TILE,),
        in_specs=[
            pl.BlockSpec((ROW_TILE, D_MODEL), lambda i: (i, 0)),
            pl.BlockSpec((None, 1, D_MODEL), lambda i: (layer, 0, 0)),
            _resident((None, D_MODEL, IN_COLS), lambda i: (e, 0, 0)),
            gain_spec, gain_spec,
        ],
        out_specs=[heads_spec] * 3 + [half_spec] * 2,
        out_shape=[heads] * 3 + [half, jax.ShapeDtypeStruct((rows, MIX_HALF), BF16)],
        compiler_params=_params("parallel"),
        name="inproj",
    )(x2, g, w_in, q_g, k_g)


def _attn_kernel(q_ref, k_ref, v_ref, o_ref, m_ref, l_ref, acc_ref, *, seq):
    jq = lax.broadcasted_iota(jnp.int32, (A_BAND, A_BAND), 0)
    jk = lax.broadcasted_iota(jnp.int32, (A_BAND, A_BAND), 1)
    own_mask = jk <= jq
    jk2 = lax.broadcasted_iota(jnp.int32, (A_BAND, 2 * A_BAND), 1)
    jq2 = lax.broadcasted_iota(jnp.int32, (A_BAND, 2 * A_BAND), 0)
    both_mask = (jk2 >= jq2) & (jk2 <= jq2 + A_BAND)
    neg = -jnp.inf
    lanes = (A_BAND, A_HEAD_DIM)
    ones = jnp.ones(lanes, BF16)

    def rows(start, d):
        if d == 1:
            return pl.ds(pl.multiple_of(start, A_BAND), A_BAND)
        return pl.ds(start, A_BAND, stride=d)

    def load_kv(start, d):
        r = rows(start, d)
        return (k_ref[r, :].astype(BF16),
                jnp.concatenate([v_ref[r, :].astype(BF16), ones], axis=1))

    def process(jobs, d, mode):
        rs = [rows(start, d) for start, _, _ in jobs]
        qs = [q_ref[r, :].astype(BF16) for r in rs]
        scores = []
        for q, (_, own, prev) in zip(qs, jobs):
            if prev is None:
                s = jnp.where(own_mask, _dot_nt(q, own[0]), neg)
            else:
                s = _dot_nt(q, jnp.concatenate([prev[0], own[0]], axis=0))
                s = jnp.where(both_mask, s, neg)
            scores.append(s)
        ms = [jnp.broadcast_to(jnp.max(s, axis=-1, keepdims=True), lanes) for s in scores]
        reads, writes = mode
        if reads is not None:
            m_olds = [m_ref[r, :] for r in rs]
            ms = [jnp.maximum(m_old, m) for m_old, m in zip(m_olds, ms)]
        pvs = []
        for s, m, (_, own, prev) in zip(scores, ms, jobs):
            if prev is None:
                v = own[1]
            else:
                v = jnp.concatenate([prev[1], own[1]], axis=0)
                m = jnp.concatenate([m, m], axis=1)
            pvs.append(_dot(jnp.exp2(s - m).astype(BF16), v))
        for i, (r, m, pv) in enumerate(zip(rs, ms, pvs)):
            acc, den = pv[:, :A_HEAD_DIM], pv[:, A_HEAD_DIM:]
            if reads is not None:
                a_old = jnp.exp2(m_olds[i] - m)
                den = (a_old if reads == "lse" else a_old * l_ref[r, :]) + den
                acc = a_old * acc_ref[r, :] + acc
            if writes is None:
                o_ref[r, :] = (acc / den).astype(o_ref.dtype)
            elif writes == "lse":
                m_ref[r, :] = m + jnp.log2(den)
                acc_ref[r, :] = acc / den
            else:
                m_ref[r, :] = m
                l_ref[r, :] = den
                acc_ref[r, :] = acc

    def jobs_of(r, c0, n, d, first):
        pitch = A_BAND * d
        kv = [None if first else load_kv(r + (c0 - 1) * pitch, d)]
        kv += [load_kv(r + (c0 + j) * pitch, d) for j in range(n)]
        return [(r + (c0 + j) * pitch, kv[j + 1], kv[j]) for j in range(n)]

    def group(r, c0, n, d, mode, first):
        process(jobs_of(r, c0, n, d, first), d, mode)

    def run_pass(d, mode):
        n_chunks = seq // (A_BAND * d)
        per_loop_step = A_CHUNKS_PER_STEP[d]
        n = min(n_chunks, per_loop_step)
        if n_chunks == n:
            per_step = per_loop_step // n

            def step(i, carry):
                jobs = []
                for j in range(per_step):
                    jobs += jobs_of(i * per_step + j, 0, n, d, True)
                process(jobs, d, mode)
                return carry

            lax.fori_loop(0, d // per_step, step, 0)
            return

        def residue(r, carry):
            group(r, 0, n, d, mode, True)

            def step(i, c):
                group(r, i * n, n, d, mode, False)
                return c

            return lax.fori_loop(1, n_chunks // n, step, carry)

        if d == 1:
            residue(0, 0)
        else:
            lax.fori_loop(0, d, residue, 0)

    forms = ["lse"] + ["raw"] * (len(A_DILATIONS) - 2)
    for d, mode in zip(A_DILATIONS, zip([None] + forms, forms + [None])):
        run_pass(d, mode)


def _attn(q, k, v):
    _, batch, seq, _ = q.shape
    assert seq % (A_BAND * max(A_DILATIONS)) == 0
    spec = pl.BlockSpec((None, None, seq, A_HEAD_DIM), lambda b, h: (h, b, 0, 0))
    return pl.pallas_call(
        functools.partial(_attn_kernel, seq=seq),
        grid=(batch, A_HEADS),
        in_specs=[spec, spec, spec],
        out_specs=spec,
        out_shape=jax.ShapeDtypeStruct(q.shape, BF16),
        scratch_shapes=[pltpu.VMEM((seq, A_HEAD_DIM), F32)] * 3,
        compiler_params=_params("parallel", "parallel"),
        name="dilated_attn",
    )(q, k, v)


def _gelu_tanh(x):
    c1 = -2.0 * 0.7978845608028654 * LOG2_E
    return x / (1.0 + jnp.exp2(x * (c1 + (c1 * 0.044715) * (x * x))))


def _rglru_out_kernel(x_ref, oa_ref, xr_ref, gate_ref, cw_ref, cb_ref, wg_ref, gab_ref, gxb_ref,
                      lam_ref, wout_ref, o_ref, xpad_ref, carry_ref):
    halo = SUBLANES
    group = (SUBLANES, RG_BLOCK_DIM)

    @pl.when(pl.program_id(1) == 0)
    def _():
        xpad_ref[0:halo, :] = jnp.zeros((halo, RG_WIDTH), F32)
        carry_ref[...] = jnp.zeros((SUBLANES, RG_WIDTH), F32)

    xpad_ref[halo:halo + RG_TILE, :] = xr_ref[...]
    w_out = wout_ref[...].astype(BF16)
    wg = [wg_ref[g].astype(BF16) for g in range(RG_BLOCKS)]
    lam = -lam_ref[...]
    softplus = jnp.maximum(lam, 0.0) + jnp.log1p(jnp.exp(-jnp.abs(lam)))
    grouped = (RG_SUB // SUBLANES, SUBLANES, RG_BLOCK_DIM)
    row = lax.broadcasted_iota(jnp.int32, grouped, 1)
    carry = [carry_ref[:, g * RG_BLOCK_DIM:(g + 1) * RG_BLOCK_DIM] for g in range(RG_BLOCKS)]

    for r0 in range(0, RG_TILE, RG_SUB):
        rs = slice(r0, r0 + RG_SUB)
        xc = cb_ref[...] + cw_ref[CONV_WIDTH - 1:CONV_WIDTH, :] * xr_ref[rs, :]
        for j in range(1, CONV_WIDTH):
            xc = xc + (cw_ref[CONV_WIDTH - 1 - j:CONV_WIDTH - j, :]
                       * xpad_ref[pl.ds(halo - j + r0, RG_SUB), :])
        mixed = [oa_ref[hd, rs, :] for hd in range(A_HEADS)]
        for g in range(RG_BLOCKS):
            sl = slice(g * RG_BLOCK_DIM, (g + 1) * RG_BLOCK_DIM)
            xg = xc[:, sl]
            rg = _dot(xg.astype(BF16), wg[g])
            r = jax.nn.sigmoid(rg[:, :RG_BLOCK_DIM] + gab_ref[:, sl])
            i = jax.nn.sigmoid(rg[:, RG_BLOCK_DIM:] + gxb_ref[:, sl])
            log_a = (-RG_C) * r * softplus[:, sl]
            a = jnp.exp(log_a)
            th = jnp.tanh(log_a)
            u = lax.rsqrt((th - 1.0) / (2.0 * th)) * (i * xg)
            a = a.reshape(grouped)
            u = u.reshape(grouped)
            k = 1
            while k < SUBLANES:
                valid = row >= k
                a_prev = pltpu.roll(a, k, 1)
                u_prev = pltpu.roll(u, k, 1)
                u = jnp.where(valid, a * u_prev + u, u)
                a = jnp.where(valid, a * a_prev, a)
                k *= 2
            a = a.reshape(RG_SUB, RG_BLOCK_DIM)
            u = u.reshape(RG_SUB, RG_BLOCK_DIM)
            hs = []
            c = carry[g]
            for gi in range(0, RG_SUB, SUBLANES):
                h = u[gi:gi + SUBLANES, :] + a[gi:gi + SUBLANES, :] * c
                hs.append(h)
                c = jnp.broadcast_to(h[SUBLANES - 1:SUBLANES, :], group)
            carry[g] = c
            mixed.append(jnp.concatenate(hs, axis=0).astype(BF16) * gate_ref[rs, sl])
        o_ref[rs, :] = x_ref[rs, :] + _dot(jnp.concatenate(mixed, axis=-1), w_out)

    for g in range(RG_BLOCKS):
        carry_ref[:, g * RG_BLOCK_DIM:(g + 1) * RG_BLOCK_DIM] = carry[g]
    xpad_ref[0:halo, :] = xr_ref[RG_TILE - halo:RG_TILE, :]


def _rglru_out(x, o_a, xr, gate, e, conv_w, conv_b, w_gates, ga_b, gx_b, lam, w_out):
    batch, seq, _ = x.shape
    full = pl.BlockSpec((None, RG_TILE, D_MODEL), lambda b, t: (b, t, 0))
    half = pl.BlockSpec((None, RG_TILE, RG_WIDTH), lambda b, t: (b, t, 0))
    vec = pl.BlockSpec((None, 1, RG_WIDTH), lambda b, t: (e, 0, 0))
    return pl.pallas_call(
        _rglru_out_kernel,
        grid=(batch, seq // RG_TILE),
        in_specs=[
            full,
            pl.BlockSpec((A_HEADS, None, RG_TILE, A_HEAD_DIM), lambda b, t: (0, b, t, 0)),
            half, half,
            pl.BlockSpec((None, CONV_WIDTH, RG_WIDTH), lambda b, t: (e, 0, 0)),
            vec,
            pl.BlockSpec((None, RG_BLOCKS, RG_BLOCK_DIM, 2 * RG_BLOCK_DIM), lambda b, t: (e, 0, 0, 0)),
            vec, vec, vec,
            _resident((None, D_MODEL, D_MODEL), lambda b, t: (e, 0, 0)),
        ],
        out_specs=full,
        out_shape=jax.ShapeDtypeStruct(x.shape, F32),
        scratch_shapes=[
            pltpu.VMEM((RG_TILE + SUBLANES, RG_WIDTH), F32),
            pltpu.VMEM((SUBLANES, RG_WIDTH), F32),
        ],
        compiler_params=_params("arbitrary", "arbitrary"),
        name="rglru_outproj",
    )(x, o_a, xr, gate, conv_w, conv_b, w_gates, ga_b, gx_b, lam, w_out)


def kernel(x, mem, mem_norm_g, mix_norm_g, xattn_norm_g, mlp_norm_g, ev_w_in, ev_q_norm_g, ev_k_norm_g, ev_conv_w, ev_conv_b, ev_gate_a_w, ev_gate_a_b, ev_gate_x_w, ev_gate_x_b, ev_lambda, ev_w_out, od_pool_w, od_scale, xa_w_q, xa_w_kv, xa_q_norm_g, xa_k_norm_g, xa_w_o, mlp_w1, mlp_w2):
    batch, seq, _ = x.shape
    rows = batch * seq

    def vec3(a):
        return a.reshape(a.shape[0], 1, a.shape[1])

    w_in, w_out = ev_w_in, ev_w_out
    w_gates = jnp.concatenate([ev_gate_a_w, ev_gate_x_w], axis=-1)
    pool_w = od_pool_w
    w_q, w_kv, w_o = xa_w_q, xa_w_kv, xa_w_o
    w1, w2 = mlp_w1, mlp_w2
    mix_g, xa_g, mlp_g = vec3(mix_norm_g), vec3(xattn_norm_g), vec3(mlp_norm_g)
    ev_qg, ev_kg = vec3(ev_q_norm_g), vec3(ev_k_norm_g)
    xa_qg = vec3(xa_q_norm_g)
    conv_b, ga_b, gx_b, lam = vec3(ev_conv_b), vec3(ev_gate_a_b), vec3(ev_gate_x_b), vec3(ev_lambda)
    od_sc = vec3(od_scale)

    kt, v_mem = _memkv(mem, mem_norm_g, w_kv, xa_k_norm_g)

    for l in range(DEPTH):
        if l % 2 == 0:
            e = l // 2
            q, k, v, xr, gate = _inproj(x.reshape(rows, D_MODEL), l, e, mix_g, w_in, ev_qg, ev_kg)
            half = (batch, seq, MIX_HALF)
            heads = (A_HEADS, batch, seq, A_HEAD_DIM)
            o_a = _attn(q.reshape(heads), k.reshape(heads), v.reshape(heads))
            x = _rglru_out(x, o_a, xr.reshape(half), gate.reshape(half), e, ev_conv_w, conv_b,
                           w_gates, ga_b, gx_b, lam, w_out)
            pool = None
        else:
            pool = (mix_g, pool_w, od_sc)
        x, w1b, w2b = _xattn(x, l, xa_g, w_q, xa_qg, kt, v_mem, w_o, w1, w2, pool)
        x = _mlp(x.reshape(rows, D_MODEL), l, mlp_g, w1b, w2b).reshape(batch, seq, D_MODEL)
    return x
```

```python
import functools

import jax
import jax.numpy as jnp
from jax import lax
from jax.experimental import pallas as pl
from jax.experimental.pallas import tpu as pltpu

D_MODEL = 1024
DEPTH = 4
N_MEM = 256
MIX_HALF = D_MODEL // 2
A_HEADS = 4
A_HEAD_DIM = MIX_HALF // A_HEADS
A_BAND = 128
A_DILATIONS = (16, 4, 1)
A_CHUNKS_PER_STEP = {16: 8, 4: 8, 1: 16}
RG_WIDTH = MIX_HALF
RG_BLOCKS = 4
RG_BLOCK_DIM = RG_WIDTH // RG_BLOCKS
RG_C = 8.0
CONV_WIDTH = 4
POOL_WINDOWS = (2, 4, 8, 16)
POOL_GROUP_DIM = D_MODEL // len(POOL_WINDOWS)
POOL_HALO = 16
XA_HEADS = 4
XA_HEAD_DIM = D_MODEL // XA_HEADS
D_FF = 4 * D_MODEL
IN_COLS = 3 * MIX_HALF + 2 * RG_WIDTH
EPS = 1e-6
LOG2_E = 1.4426950408889634

SUBLANES = 8
VMEM_LIMIT = 56 * 1024 * 1024

ROW_TILE = 1024
XA_SUB = 512
MLP_TILE = 1024
MLP_CHUNK = 512
RG_TILE = 1024
RG_SUB = 256

BF16 = jnp.bfloat16
F32 = jnp.float32


def _rms(x, g):
    ms = jnp.mean(x * x, axis=-1, keepdims=True)
    return x * lax.rsqrt(ms + EPS) * g


def _dot(a, b):
    return jnp.dot(a, b, preferred_element_type=F32)


def _dot_nt(a, b):
    return lax.dot_general(a, b, (((1,), (1,)), ((), ())), preferred_element_type=F32)


def _resident(shape, index_map):
    return pl.BlockSpec(shape, index_map, pipeline_mode=pl.Buffered(1))


def _params(*sem):
    return pltpu.CompilerParams(dimension_semantics=sem, vmem_limit_bytes=VMEM_LIMIT)


def _memkv_kernel(mem_ref, mg_ref, wkv_ref, kg_ref, kt_ref, v_ref):
    batch = mem_ref.shape[0]
    mem = mem_ref[...].reshape(batch * N_MEM, D_MODEL)
    mem_n = _rms(mem, mg_ref[...]).astype(BF16)
    kv = _dot(mem_n, wkv_ref[...].astype(BF16))
    scale = XA_HEAD_DIM ** -0.5 * LOG2_E
    for h in range(XA_HEADS):
        sl = slice(h * XA_HEAD_DIM, (h + 1) * XA_HEAD_DIM)
        kh = _rms(kv[:, sl], kg_ref[...]) * scale
        for b in range(batch):
            kt_ref[b, sl, :] = kh[b * N_MEM:(b + 1) * N_MEM, :].T.astype(BF16)
    v_ref[...] = kv[:, D_MODEL:].astype(BF16).reshape(batch, N_MEM, D_MODEL)


def _memkv(mem, mem_norm_g, w_kv, k_g):
    batch = mem.shape[0]
    return pl.pallas_call(
        _memkv_kernel,
        grid=(DEPTH,),
        in_specs=[
            pl.BlockSpec((batch, N_MEM, D_MODEL), lambda l: (0, 0, 0)),
            pl.BlockSpec((1, D_MODEL), lambda l: (0, 0)),
            pl.BlockSpec((None, D_MODEL, 2 * D_MODEL), lambda l: (l, 0, 0)),
            pl.BlockSpec((None, 1, XA_HEAD_DIM), lambda l: (l, 0, 0)),
        ],
        out_specs=[
            pl.BlockSpec((None, batch, D_MODEL, N_MEM), lambda l: (l, 0, 0, 0)),
            pl.BlockSpec((None, batch, N_MEM, D_MODEL), lambda l: (l, 0, 0, 0)),
        ],
        out_shape=[
            jax.ShapeDtypeStruct((DEPTH, batch, D_MODEL, N_MEM), BF16),
            jax.ShapeDtypeStruct((DEPTH, batch, N_MEM, D_MODEL), BF16),
        ],
        compiler_params=_params("arbitrary"),
        name="memkv",
    )(mem, mem_norm_g.reshape(1, D_MODEL), w_kv, k_g.reshape(DEPTH, 1, XA_HEAD_DIM))


def _pool_rows(x, r0, pos0, mg_ref, pw_ref, sc_ref, hpad_ref):
    rows = x.shape[0]
    h = _rms(x, mg_ref[...])
    hpad_ref[POOL_HALO + r0:POOL_HALO + r0 + rows, :] = h
    pos = pos0 + lax.broadcasted_iota(jnp.int32, (rows, 1), 0)
    s = hpad_ref[r0:r0 + POOL_HALO + rows, :]
    outs = []
    width = 1
    for gi, w in enumerate(POOL_WINDOWS):
        while width < w:
            s = s + pltpu.roll(s, width, 0)
            width *= 2
        cnt = jnp.minimum(pos + 1, w).astype(F32)
        d = s[POOL_HALO:, :POOL_GROUP_DIM] / cnt - h[:, gi * POOL_GROUP_DIM:(gi + 1) * POOL_GROUP_DIM]
        outs.append(_dot(d.astype(BF16), pw_ref[gi].astype(BF16)))
        if gi + 1 < len(POOL_WINDOWS):
            s = s[:, POOL_GROUP_DIM:]
    return x + jnp.concatenate(outs, axis=-1) * sc_ref[...]


def _xattn_kernel(*refs, pool):
    if pool:
        (x_ref, mg_ref, pw_ref, sc_ref, g_ref, wq_ref, qg_ref, kt_ref, v_ref, wo_ref, w1p_ref, w2p_ref,
         o_ref, w1b_ref, w2b_ref, hpad_ref) = refs
        t = pl.program_id(1)

        @pl.when(t == 0)
        def _():
            hpad_ref[0:POOL_HALO, :] = jnp.zeros((POOL_HALO, D_MODEL), F32)
    else:
        (x_ref, g_ref, wq_ref, qg_ref, kt_ref, v_ref, wo_ref, w1p_ref, w2p_ref,
         o_ref, w1b_ref, w2b_ref) = refs
    w1b_ref[...] = w1p_ref[...].astype(BF16)
    w2b_ref[...] = w2p_ref[...].astype(BF16)
    w_q = wq_ref[...].astype(BF16)
    w_o = wo_ref[...].astype(BF16)
    heads = [slice(hd * XA_HEAD_DIM, (hd + 1) * XA_HEAD_DIM) for hd in range(XA_HEADS)]
    for r0 in range(0, ROW_TILE, XA_SUB):
        rs = slice(r0, r0 + XA_SUB)
        x = x_ref[rs, :]
        if pool:
            x = _pool_rows(x, r0, t * ROW_TILE + r0, mg_ref, pw_ref, sc_ref, hpad_ref)
        h = _rms(x, g_ref[...]).astype(BF16)
        qs = [_dot(h, w_q[:, sl]) for sl in heads]
        qs = [_rms(q, qg_ref[...]).astype(BF16) for q in qs]
        ss = [_dot(q, kt_ref[sl, :]) for q, sl in zip(qs, heads)]
        ps = [jnp.exp2(s - jnp.max(s, axis=-1, keepdims=True)) for s in ss]
        dens = [jnp.sum(p, axis=-1, keepdims=True) for p in ps]
        os_ = [_dot(p.astype(BF16), v_ref[:, sl]) for p, sl in zip(ps, heads)]
        os_ = [(o / den).astype(BF16) for o, den in zip(os_, dens)]
        out = x
        for o, sl in zip(os_, heads):
            out = out + _dot(o, w_o[sl, :])
        o_ref[rs, :] = out
    if pool:
        hpad_ref[0:POOL_HALO, :] = hpad_ref[ROW_TILE:ROW_TILE + POOL_HALO, :]


def _xattn(x, layer, g, w_q, q_g, kt, v, w_o, w1, w2, pool=None):
    batch, seq, _ = x.shape
    n_t = seq // ROW_TILE
    steps = batch * n_t
    tile = pl.BlockSpec((None, ROW_TILE, D_MODEL), lambda b, t: (b, t, 0))
    in_specs = [
        pl.BlockSpec((None, 1, D_MODEL), lambda b, t: (layer, 0, 0)),
        _resident((None, D_MODEL, D_MODEL), lambda b, t: (layer, 0, 0)),
        pl.BlockSpec((None, 1, XA_HEAD_DIM), lambda b, t: (layer, 0, 0)),
        pl.BlockSpec((None, None, D_MODEL, N_MEM), lambda b, t: (layer, b, 0, 0)),
        pl.BlockSpec((None, None, N_MEM, D_MODEL), lambda b, t: (layer, b, 0, 0)),
        _resident((None, D_MODEL, D_MODEL), lambda b, t: (layer, 0, 0)),
        pl.BlockSpec((None, D_MODEL // steps, D_FF), lambda b, t: (layer, b * n_t + t, 0)),
        pl.BlockSpec((None, D_FF // steps, D_MODEL), lambda b, t: (layer, b * n_t + t, 0)),
    ]
    args = (g, w_q, q_g, kt, v, w_o, w1, w2)
    scratch = []
    if pool is not None:
        o = layer // 2
        in_specs = [
            pl.BlockSpec((None, 1, D_MODEL), lambda b, t: (layer, 0, 0)),
            pl.BlockSpec((None, len(POOL_WINDOWS), POOL_GROUP_DIM, POOL_GROUP_DIM),
                         lambda b, t: (o, 0, 0, 0)),
            pl.BlockSpec((None, 1, D_MODEL), lambda b, t: (o, 0, 0)),
        ] + in_specs
        args = tuple(pool) + args
        scratch = [pltpu.VMEM((POOL_HALO + ROW_TILE, D_MODEL), F32)]
    return pl.pallas_call(
        functools.partial(_xattn_kernel, pool=pool is not None),
        grid=(batch, seq // ROW_TILE),
        in_specs=[tile] + in_specs,
        out_specs=[
            tile,
            pl.BlockSpec((D_MODEL // steps, D_FF), lambda b, t: (b * n_t + t, 0)),
            pl.BlockSpec((D_FF // steps, D_MODEL), lambda b, t: (b * n_t + t, 0)),
        ],
        out_shape=[
            jax.ShapeDtypeStruct(x.shape, F32),
            jax.ShapeDtypeStruct((D_MODEL, D_FF), BF16),
            jax.ShapeDtypeStruct((D_FF, D_MODEL), BF16),
        ],
        scratch_shapes=scratch,
        compiler_params=_params("parallel", "arbitrary" if pool is not None else "parallel"),
        name="xattn" if pool is None else "pool_xattn",
    )(x, *args)


def _mlp_kernel(x_ref, g_ref, w1_ref, w2_ref, o_ref):
    x = x_ref[...]
    h = _rms(x, g_ref[...]).astype(BF16)
    acc = x
    for c in range(D_FF // MLP_CHUNK):
        sl = slice(c * MLP_CHUNK, (c + 1) * MLP_CHUNK)
        a = jnp.maximum(_dot(h, w1_ref[:, sl]), 0.0)
        acc = acc + _dot((a * a).astype(BF16), w2_ref[sl, :])
    o_ref[...] = acc


def _mlp(x2, layer, g, w1b, w2b):
    rows = x2.shape[0]
    return pl.pallas_call(
        _mlp_kernel,
        grid=(rows // MLP_TILE,),
        in_specs=[
            pl.BlockSpec((MLP_TILE, D_MODEL), lambda i: (i, 0)),
            pl.BlockSpec((None, 1, D_MODEL), lambda i: (layer, 0, 0)),
            _resident((D_MODEL, D_FF), lambda i: (0, 0)),
            _resident((D_FF, D_MODEL), lambda i: (0, 0)),
        ],
        out_specs=pl.BlockSpec((MLP_TILE, D_MODEL), lambda i: (i, 0)),
        out_shape=jax.ShapeDtypeStruct(x2.shape, F32),
        compiler_params=_params("parallel"),
        name="mlp",
    )(x2, g, w1b, w2b)


def _inproj_kernel(x_ref, g_ref, w_ref, qg_ref, kg_ref, q_ref, k_ref, v_ref, xr_ref, gate_ref):
    h = _rms(x_ref[...], g_ref[...]).astype(BF16)
    scale = A_HEAD_DIM ** -0.5 * LOG2_E
    for j, (out_ref, gain_ref, mult) in (
            (4, (gate_ref, None, None)), (0, (q_ref, qg_ref, scale)), (1, (k_ref, kg_ref, 1.0)),
            (2, (v_ref, None, None)), (3, (xr_ref, None, None))):
        p = _dot(h, w_ref[:, j * MIX_HALF:(j + 1) * MIX_HALF].astype(BF16))
        if j >= 3:
            out_ref[...] = p if j == 3 else _gelu_tanh(p).astype(out_ref.dtype)
            continue
        for hd in range(A_HEADS):
            ph = p[:, hd * A_HEAD_DIM:(hd + 1) * A_HEAD_DIM]
            out_ref[hd] = ph if gain_ref is None else _rms(ph, gain_ref[...]) * mult


def _inproj(x2, layer, e, g, w_in, q_g, k_g):
    rows = x2.shape[0]
    half = jax.ShapeDtypeStruct((rows, MIX_HALF), F32)
    half_spec = pl.BlockSpec((ROW_TILE, MIX_HALF), lambda i: (i, 0))
    heads = jax.ShapeDtypeStruct((A_HEADS, rows, A_HEAD_DIM), F32)
    heads_spec = pl.BlockSpec((A_HEADS, ROW_TILE, A_HEAD_DIM), lambda i: (0, i, 0))
    gain_spec = pl.BlockSpec((None, 1, A_HEAD_DIM), lambda i: (e, 0, 0))
    return pl.pallas_call(
        _inproj_kernel,
        grid=(rows // ROW_TILE,),
        in_specs=[
            pl.BlockSpec((ROW_TILE, D_MODEL), lambda i: (i, 0)),
            pl.BlockSpec((None, 1, D_MODEL), lambda i: (layer, 0, 0)),
            _resident((None, D_MODEL, IN_COLS), lambda i: (e, 0, 0)),
            gain_spec, gain_spec,
        ],
        out_specs=[heads_spec] * 3 + [half_spec] * 2,
        out_shape=[heads] * 3 + [half, jax.ShapeDtypeStruct((rows, MIX_HALF), BF16)],
        compiler_params=_params("parallel"),
        name="inproj",
    )(x2, g, w_in, q_g, k_g)


def _attn_kernel(q_ref, k_ref, v_ref, o_ref, m_ref, l_ref, acc_ref, *, seq):
    jq = lax.broadcasted_iota(jnp.int32, (A_BAND, A_BAND), 0)
    jk = lax.broadcasted_iota(jnp.int32, (A_BAND, A_BAND), 1)
    own_mask = jk <= jq
    jk2 = lax.broadcasted_iota(jnp.int32, (A_BAND, 2 * A_BAND), 1)
    jq2 = lax.broadcasted_iota(jnp.int32, (A_BAND, 2 * A_BAND), 0)
    both_mask = (jk2 >= jq2) & (jk2 <= jq2 + A_BAND)
    neg = -jnp.inf
    lanes = (A_BAND, A_HEAD_DIM)
    ones = jnp.ones(lanes, BF16)

    def rows(start, d):
        if d == 1:
            return pl.ds(pl.multiple_of(start, A_BAND), A_BAND)
        return pl.ds(start, A_BAND, stride=d)

    def load_kv(start, d):
        r = rows(start, d)
        return (k_ref[r, :].astype(BF16),
                jnp.concatenate([v_ref[r, :].astype(BF16), ones], axis=1))

    def process(jobs, d, mode):
        rs = [rows(start, d) for start, _, _ in jobs]
        qs = [q_ref[r, :].astype(BF16) for r in rs]
        scores = []
        for q, (_, own, prev) in zip(qs, jobs):
            if prev is None:
                s = jnp.where(own_mask, _dot_nt(q, own[0]), neg)
            else:
                s = _dot_nt(q, jnp.concatenate([prev[0], own[0]], axis=0))
                s = jnp.where(both_mask, s, neg)
            scores.append(s)
        ms = [jnp.broadcast_to(jnp.max(s, axis=-1, keepdims=True), lanes) for s in scores]
        reads, writes = mode
        if reads is not None:
            m_olds = [m_ref[r, :] for r in rs]
            ms = [jnp.maximum(m_old, m) for m_old, m in zip(m_olds, ms)]
        pvs = []
        for s, m, (_, own, prev) in zip(scores, ms, jobs):
            if prev is None:
                v = own[1]
            else:
                v = jnp.concatenate([prev[1], own[1]], axis=0)
                m = jnp.concatenate([m, m], axis=1)
            pvs.append(_dot(jnp.exp2(s - m).astype(BF16), v))
        for i, (r, m, pv) in enumerate(zip(rs, ms, pvs)):
            acc, den = pv[:, :A_HEAD_DIM], pv[:, A_HEAD_DIM:]
            if reads is not None:
                a_old = jnp.exp2(m_olds[i] - m)
                den = (a_old if reads == "lse" else a_old * l_ref[r, :]) + den
                acc = a_old * acc_ref[r, :] + acc
            if writes is None:
                o_ref[r, :] = (acc / den).astype(o_ref.dtype)
            elif writes == "lse":
                m_ref[r, :] = m + jnp.log2(den)
                acc_ref[r, :] = acc / den
            else:
                m_ref[r, :] = m
                l_ref[r, :] = den
                acc_ref[r, :] = acc

    def jobs_of(r, c0, n, d, first):
        pitch = A_BAND * d
        kv = [None if first else load_kv(r + (c0 - 1) * pitch, d)]
        kv += [load_kv(r + (c0 + j) * pitch, d) for j in range(n)]
        return [(r + (c0 + j) * pitch, kv[j + 1], kv[j]) for j in range(n)]

    def group(r, c0, n, d, mode, first):
        process(jobs_of(r, c0, n, d, first), d, mode)

    def run_pass(d, mode):
        n_chunks = seq // (A_BAND * d)
        per_loop_step = A_CHUNKS_PER_STEP[d]
        n = min(n_chunks, per_loop_step)
        if n_chunks == n:
            per_step = per_loop_step // n

            def step(i, carry):
                jobs = []
                for j in range(per_step):
                    jobs += jobs_of(i * per_step + j, 0, n, d, True)
                process(jobs, d, mode)
                return carry

            lax.fori_loop(0, d // per_step, step, 0)
            return

        def residue(r, carry):
            group(r, 0, n, d, mode, True)

            def step(i, c):
                group(r, i * n, n, d, mode, False)
                return c

            return lax.fori_loop(1, n_chunks // n, step, carry)

        if d == 1:
            residue(0, 0)
        else:
            lax.fori_loop(0, d, residue, 0)

    forms = ["lse"] + ["raw"] * (len(A_DILATIONS) - 2)
    for d, mode in zip(A_DILATIONS, zip([None] + forms, forms + [None])):
        run_pass(d, mode)


def _attn(q, k, v):
    _, batch, seq, _ = q.shape
    assert seq % (A_BAND * max(A_DILATIONS)) == 0
    spec = pl.BlockSpec((None, None, seq, A_HEAD_DIM), lambda b, h: (h, b, 0, 0))
    return pl.pallas_call(
        functools.partial(_attn_kernel, seq=seq),
        grid=(batch, A_HEADS),
        in_specs=[spec, spec, spec],
        out_specs=spec,
        out_shape=jax.ShapeDtypeStruct(q.shape, BF16),
        scratch_shapes=[pltpu.VMEM((seq, A_HEAD_DIM), F32)] * 3,
        compiler_params=_params("parallel", "parallel"),
        name="dilated_attn",
    )(q, k, v)


def _gelu_tanh(x):
    c1 = -2.0 * 0.7978845608028654 * LOG2_E
    return x / (1.0 + jnp.exp2(x * (c1 + (c1 * 0.044715) * (x * x))))


def _rglru_out_kernel(x_ref, oa_ref, xr_ref, gate_ref, cw_ref, cb_ref, wg_ref, gab_ref, gxb_ref,
                      lam_ref, wout_ref, o_ref, xpad_ref, carry_ref):
    halo = SUBLANES
    group = (SUBLANES, RG_BLOCK_DIM)

    @pl.when(pl.program_id(1) == 0)
    def _():
        xpad_ref[0:halo, :] = jnp.zeros((halo, RG_WIDTH), F32)
        carry_ref[...] = jnp.zeros((SUBLANES, RG_WIDTH), F32)

    xpad_ref[halo:halo + RG_TILE, :] = xr_ref[...]
    w_out = wout_ref[...].astype(BF16)
    wg = [wg_ref[g].astype(BF16) for g in range(RG_BLOCKS)]
    lam = -lam_ref[...]
    softplus = jnp.maximum(lam, 0.0) + jnp.log1p(jnp.exp(-jnp.abs(lam)))
    grouped = (RG_SUB // SUBLANES, SUBLANES, RG_BLOCK_DIM)
    row = lax.broadcasted_iota(jnp.int32, grouped, 1)
    carry = [carry_ref[:, g * RG_BLOCK_DIM:(g + 1) * RG_BLOCK_DIM] for g in range(RG_BLOCKS)]

    for r0 in range(0, RG_TILE, RG_SUB):
        rs = slice(r0, r0 + RG_SUB)
        xc = cb_ref[...] + cw_ref[CONV_WIDTH - 1:CONV_WIDTH, :] * xr_ref[rs, :]
        for j in range(1, CONV_WIDTH):
            xc = xc + (cw_ref[CONV_WIDTH - 1 - j:CONV_WIDTH - j, :]
                       * xpad_ref[pl.ds(halo - j + r0, RG_SUB), :])
        mixed = [oa_ref[hd, rs, :] for hd in range(A_HEADS)]
        for g in range(RG_BLOCKS):
            sl = slice(g * RG_BLOCK_DIM, (g + 1) * RG_BLOCK_DIM)
            xg = xc[:, sl]
            rg = _dot(xg.astype(BF16), wg[g])
            r = jax.nn.sigmoid(rg[:, :RG_BLOCK_DIM] + gab_ref[:, sl])
            i = jax.nn.sigmoid(rg[:, RG_BLOCK_DIM:] + gxb_ref[:, sl])
            log_a = (-RG_C) * r * softplus[:, sl]
            a = jnp.exp(log_a)
            th = jnp.tanh(log_a)
            u = lax.rsqrt((th - 1.0) / (2.0 * th)) * (i * xg)
            a = a.reshape(grouped)
            u = u.reshape(grouped)
            k = 1
            while k < SUBLANES:
                valid = row >= k
                a_prev = pltpu.roll(a, k, 1)
                u_prev = pltpu.roll(u, k, 1)
                u = jnp.where(valid, a * u_prev + u, u)
                a = jnp.where(valid, a * a_prev, a)
                k *= 2
            a = a.reshape(RG_SUB, RG_BLOCK_DIM)
            u = u.reshape(RG_SUB, RG_BLOCK_DIM)
            hs = []
            c = carry[g]
            for gi in range(0, RG_SUB, SUBLANES):
                h = u[gi:gi + SUBLANES, :] + a[gi:gi + SUBLANES, :] * c
                hs.append(h)
                c = jnp.broadcast_to(h[SUBLANES - 1:SUBLANES, :], group)
            carry[g] = c
            mixed.append(jnp.concatenate(hs, axis=0).astype(BF16) * gate_ref[rs, sl])
        o_ref[rs, :] = x_ref[rs, :] + _dot(jnp.concatenate(mixed, axis=-1), w_out)

    for g in range(RG_BLOCKS):
        carry_ref[:, g * RG_BLOCK_DIM:(g + 1) * RG_BLOCK_DIM] = carry[g]
    xpad_ref[0:halo, :] = xr_ref[RG_TILE - halo:RG_TILE, :]


def _rglru_out(x, o_a, xr, gate, e, conv_w, conv_b, w_gates, ga_b, gx_b, lam, w_out):
    batch, seq, _ = x.shape
    full = pl.BlockSpec((None, RG_TILE, D_MODEL), lambda b, t: (b, t, 0))
    half = pl.BlockSpec((None, RG_TILE, RG_WIDTH), lambda b, t: (b, t, 0))
    vec = pl.BlockSpec((None, 1, RG_WIDTH), lambda b, t: (e, 0, 0))
    return pl.pallas_call(
        _rglru_out_kernel,
        grid=(batch, seq // RG_TILE),
        in_specs=[
            full,
            pl.BlockSpec((A_HEADS, None, RG_TILE, A_HEAD_DIM), lambda b, t: (0, b, t, 0)),
            half, half,
            pl.BlockSpec((None, CONV_WIDTH, RG_WIDTH), lambda b, t: (e, 0, 0)),
            vec,
            pl.BlockSpec((None, RG_BLOCKS, RG_BLOCK_DIM, 2 * RG_BLOCK_DIM), lambda b, t: (e, 0, 0, 0)),
            vec, vec, vec,
            _resident((None, D_MODEL, D_MODEL), lambda b, t: (e, 0, 0)),
        ],
        out_specs=full,
        out_shape=jax.ShapeDtypeStruct(x.shape, F32),
        scratch_shapes=[
            pltpu.VMEM((RG_TILE + SUBLANES, RG_WIDTH), F32),
            pltpu.VMEM((SUBLANES, RG_WIDTH), F32),
        ],
        compiler_params=_params("arbitrary", "arbitrary"),
        name="rglru_outproj",
    )(x, o_a, xr, gate, conv_w, conv_b, w_gates, ga_b, gx_b, lam, w_out)


def kernel(x, mem, mem_norm_g, mix_norm_g, xattn_norm_g, mlp_norm_g, ev_w_in, ev_q_norm_g, ev_k_norm_g, ev_conv_w, ev_conv_b, ev_gate_a_w, ev_gate_a_b, ev_gate_x_w, ev_gate_x_b, ev_lambda, ev_w_out, od_pool_w, od_scale, xa_w_q, xa_w_kv, xa_q_norm_g, xa_k_norm_g, xa_w_o, mlp_w1, mlp_w2):
    batch, seq, _ = x.shape
    rows = batch * seq

    def vec3(a):
        return a.reshape(a.shape[0], 1, a.shape[1])

    w_in, w_out = ev_w_in, ev_w_out
    w_gates = jnp.concatenate([ev_gate_a_w, ev_gate_x_w], axis=-1)
    pool_w = od_pool_w
    w_q, w_kv, w_o = xa_w_q, xa_w_kv, xa_w_o
    w1, w2 = mlp_w1, mlp_w2
    mix_g, xa_g, mlp_g = vec3(mix_norm_g), vec3(xattn_norm_g), vec3(mlp_norm_g)
    ev_qg, ev_kg = vec3(ev_q_norm_g), vec3(ev_k_norm_g)
    xa_qg = vec3(xa_q_norm_g)
    conv_b, ga_b, gx_b, lam = vec3(ev_conv_b), vec3(ev_gate_a_b), vec3(ev_gate_x_b), vec3(ev_lambda)
    od_sc = vec3(od_scale)

    kt, v_mem = _memkv(mem, mem_norm_g, w_kv, xa_k_norm_g)

    for l in range(DEPTH):
        if l % 2 == 0:
            e = l // 2
            q, k, v, xr, gate = _inproj(x.reshape(rows, D_MODEL), l, e, mix_g, w_in, ev_qg, ev_kg)
            half = (batch, seq, MIX_HALF)
            heads = (A_HEADS, batch, seq, A_HEAD_DIM)
            o_a = _attn(q.reshape(heads), k.reshape(heads), v.reshape(heads))
            x = _rglru_out(x, o_a, xr.reshape(half), gate.reshape(half), e, ev_conv_w, conv_b,
                           w_gates, ga_b, gx_b, lam, w_out)
            pool = None
        else:
            pool = (mix_g, pool_w, od_sc)
        x, w1b, w2b = _xattn(x, l, xa_g, w_q, xa_qg, kt, v_mem, w_o, w1, w2, pool)
        x = _mlp(x.reshape(rows, D_MODEL), l, mlp_g, w1b, w2b).reshape(batch, seq, D_MODEL)
    return x
```
